```python
import jax, jax.numpy as jnp
from jax import lax
import numpy as np

D_MODEL = 4096
BATCH = 1
SEQ = 8192
DEPTH = 2

HEAD_DIM = 128
MIX_WIDTH = D_MODEL
N_MIXERS = 4
HEADS_PER_MIXER = MIX_WIDTH // (N_MIXERS * HEAD_DIM)
GROUP_WIDTH = HEADS_PER_MIXER * HEAD_DIM
Q_LORA_RANK = 896
KV_LORA_RANK = 512
MLA_NOPE_DIM = 128
MLA_ROPE_DIM = 64
ROPE_THETA = 500000.0
PARTIAL_ROPE_DIM = HEAD_DIM // 4
DILATED_PAIRS = ((128, 1), (512, 4), (2048, 16))
Q_BLOCK = 128
DIL_ALIGN = max(d for _, d in DILATED_PAIRS) * Q_BLOCK
D_FF = 4 * D_MODEL
EPS = 1e-6
NEG_INF = -1e30

IN_SIZES = (Q_LORA_RANK, KV_LORA_RANK, MLA_ROPE_DIM,
            3 * GROUP_WIDTH, 3 * GROUP_WIDTH, 3 * GROUP_WIDTH, HEADS_PER_MIXER)
IN_COLS = sum(IN_SIZES)
SPLIT_POINTS = tuple(int(v) for v in np.cumsum(IN_SIZES)[:-1])

kernel_name = "hymba_mla_dilated_stickbreak_fox_trunk"


def rmsnorm(x, g):
    x32 = x.astype(jnp.float32)
    y = x32 * lax.rsqrt(jnp.mean(x32 * x32, axis=-1, keepdims=True) + EPS)
    return (y * g.astype(jnp.float32)).astype(x.dtype)


def rope(t, pos):
    r = t.shape[-1]
    half = r // 2
    inv = ROPE_THETA ** (-jnp.arange(half, dtype=jnp.float32) * (2.0 / r))
    ang = pos.astype(jnp.float32)[:, None] * inv[None, :]
    cos = jnp.cos(ang)[:, None, :]
    sin = jnp.sin(ang)[:, None, :]
    t32 = t.astype(jnp.float32)
    t1, t2 = t32[..., :half], t32[..., half:]
    return jnp.concatenate([t1 * cos - t2 * sin, t1 * sin + t2 * cos], axis=-1).astype(t.dtype)


def partial_rope(t, pos):
    return jnp.concatenate([rope(t[..., :PARTIAL_ROPE_DIM], pos), t[..., PARTIAL_ROPE_DIM:]], axis=-1)


def to_blocks(t):
    b, s, h, d = t.shape
    return t.reshape(b, s // Q_BLOCK, Q_BLOCK, h, d).transpose(1, 0, 2, 3, 4)


def from_blocks(t):
    nb, b, q, h, d = t.shape
    return t.transpose(1, 0, 2, 3, 4).reshape(b, nb * q, h, d)


def mla_attention(q_nope, q_rope, k_nope, k_rope, v):
    s_len = k_nope.shape[1]
    kpos = jnp.arange(s_len)
    scale = (MLA_NOPE_DIM + MLA_ROPE_DIM) ** -0.5

    def blk(args):
        qn, qr, i = args
        s = (jnp.einsum('bqhd,bkhd->bhqk', qn, k_nope)
             + jnp.einsum('bqhd,bkd->bhqk', qr, k_rope)).astype(jnp.float32) * scale
        qpos = i * Q_BLOCK + jnp.arange(Q_BLOCK)
        s = jnp.where(kpos[None, :] <= qpos[:, None], s, NEG_INF)
        p = jax.nn.softmax(s, axis=-1)
        return jnp.einsum('bhqk,bkhd->bqhd', p, v)

    nb = s_len // Q_BLOCK
    out = lax.map(blk, (to_blocks(q_nope), to_blocks(q_rope), jnp.arange(nb)))
    return from_blocks(out)


def dilated_branch(q, k, v, window, dil):
    b, sp, h, d = q.shape
    sub_len = sp // dil
    nb = sub_len // Q_BLOCK
    band = window // dil
    scale = HEAD_DIM ** -0.5

    def to_sub(t):
        return t.reshape(b, sub_len, dil, h, d).transpose(0, 2, 1, 3, 4).reshape(b * dil, nb, Q_BLOCK, h, d)

    def with_prev(t):
        prev = jnp.pad(t[:, :-1], ((0, 0), (1, 0), (0, 0), (0, 0), (0, 0)))
        return jnp.concatenate([prev, t], axis=2)

    qs = to_sub(q)
    kb = with_prev(to_sub(k))
    vb = with_prev(to_sub(v))
    s = jnp.einsum('bnqhd,bnkhd->bnhqk', qs, kb).astype(jnp.float32) * scale
    i = jnp.arange(Q_BLOCK)[:, None]
    j = jnp.arange(2 * Q_BLOCK)[None, :]
    n = jnp.arange(nb)[:, None, None]
    dist = Q_BLOCK + i - j
    valid = (dist >= 0) & (dist <= band) & ((n > 0) | (j >= Q_BLOCK))
    s = jnp.where(valid[None, :, None], s, NEG_INF)
    m = jnp.max(s, axis=-1, keepdims=True)
    p = jnp.exp(s - m)
    l = jnp.sum(p, axis=-1, keepdims=True)
    o = jnp.einsum('bnhqk,bnkhd->bnhqd', p, vb) / l
    lse = (m + jnp.log(l))[..., 0]
    o = o.transpose(0, 1, 3, 2, 4).reshape(b, dil, sub_len, h, d).transpose(0, 2, 1, 3, 4).reshape(b, sp, h, d)
    lse = lse.transpose(0, 1, 3, 2).reshape(b, dil, sub_len, h).transpose(0, 2, 1, 3).reshape(b, sp, h)
    return o, lse


def dilated_attention(q, k, v):
    s_len = q.shape[1]
    sp = -(-s_len // DIL_ALIGN) * DIL_ALIGN
    padw = ((0, 0), (0, sp - s_len), (0, 0), (0, 0))
    qp, kp, vp = jnp.pad(q, padw), jnp.pad(k, padw), jnp.pad(v, padw)
    outs, lses = [], []
    for window, dil in DILATED_PAIRS:
        o, lse = dilated_branch(qp, kp, vp, window, dil)
        outs.append(o)
        lses.append(lse)
    w = jax.nn.softmax(jnp.stack(lses, axis=0), axis=0)
    o = jnp.sum(w[..., None] * jnp.stack(outs, axis=0), axis=0)
    return o[:, :s_len]


def stick_breaking_attention(q, k, v):
    s_len = k.shape[1]
    kpos = jnp.arange(s_len)
    scale = HEAD_DIM ** -0.5

    def blk(args):
        qb, i = args
        z = jnp.einsum('bqhd,bkhd->bhqk', qb, k).astype(jnp.float32) * scale
        qpos = i * Q_BLOCK + jnp.arange(Q_BLOCK)
        mask = kpos[None, :] < qpos[:, None]
        log_1m = jnp.where(mask, jax.nn.log_sigmoid(-z), 0.0)
        after = lax.cumsum(log_1m, axis=3, reverse=True) - log_1m
        a = jnp.where(mask, jnp.exp(jax.nn.log_sigmoid(z) + after), 0.0)
        return jnp.einsum('bhqk,bkhd->bqhd', a, v)

    nb = s_len // Q_BLOCK
    return from_blocks(lax.map(blk, (to_blocks(q), jnp.arange(nb))))


def forgetting_attention(q, k, v, logf_cum):
    b, s_len, h, _ = k.shape
    kpos = jnp.arange(s_len)
    scale = HEAD_DIM ** -0.5
    nb = s_len // Q_BLOCK
    ck = jnp.transpose(logf_cum, (0, 2, 1))[:, :, None, :]
    cq = logf_cum.reshape(b, nb, Q_BLOCK, h).transpose(1, 0, 3, 2)

    def blk(args):
        qb, cqb, i = args
        s = jnp.einsum('bqhd,bkhd->bhqk', qb, k).astype(jnp.float32) * scale + (cqb[..., None] - ck)
        qpos = i * Q_BLOCK + jnp.arange(Q_BLOCK)
        s = jnp.where(kpos[None, :] <= qpos[:, None], s, NEG_INF)
        p = jax.nn.softmax(s, axis=-1)
        return jnp.einsum('bhqk,bkhd->bqhd', p, v)

    return from_blocks(lax.map(blk, (to_blocks(q), cq, jnp.arange(nb))))


def setup_inputs(seed: int = 0) -> dict:
    key = jax.random.key(seed)
    ks = jax.random.split(key, 16)
    f32 = jnp.float32

    def nrm(k, shape, fan_in):
        return jax.random.normal(k, shape, f32) * (fan_in ** -0.5)

    def gain(k, shape):
        return 1.0 + 0.05 * jax.random.normal(k, shape, f32)

    return {
        "x": jax.random.normal(ks[0], (BATCH, SEQ, D_MODEL), f32),
        "g_attn": gain(ks[1], (DEPTH, D_MODEL)),
        "w_in": nrm(ks[2], (DEPTH, D_MODEL, IN_COLS), D_MODEL),
        "g_q": gain(ks[3], (DEPTH, Q_LORA_RANK)),
        "g_kv": gain(ks[4], (DEPTH, KV_LORA_RANK)),
        "w_uq": nrm(ks[5], (DEPTH, Q_LORA_RANK, HEADS_PER_MIXER * (MLA_NOPE_DIM + MLA_ROPE_DIM)), Q_LORA_RANK),
        "w_uk": nrm(ks[6], (DEPTH, KV_LORA_RANK, HEADS_PER_MIXER * MLA_NOPE_DIM), KV_LORA_RANK),
        "w_uv": nrm(ks[7], (DEPTH, KV_LORA_RANK, GROUP_WIDTH), KV_LORA_RANK),
        "b_f": jax.random.uniform(ks[8], (DEPTH, HEADS_PER_MIXER), f32, minval=1.0, maxval=5.0),
        "g_out": gain(ks[9], (DEPTH, MIX_WIDTH)),
        "w_o": nrm(ks[10], (DEPTH, MIX_WIDTH, D_MODEL), MIX_WIDTH),
        "g_mlp": gain(ks[11], (DEPTH, D_MODEL)),
        "w_up": nrm(ks[12], (DEPTH, D_MODEL, D_FF), D_MODEL),
        "w_down": nrm(ks[13], (DEPTH, D_FF, D_MODEL), D_FF),
        "g_final": gain(ks[14], (D_MODEL,)),
    }


def reference(x, g_attn, w_in, g_q, g_kv, w_uq, w_uk, w_uv, b_f, g_out, w_o, g_mlp, w_up, w_down, g_final):
    b, s_len, _ = x.shape
    pos = jnp.arange(s_len)

    def heads(t, d=HEAD_DIM):
        return t.reshape(b, s_len, -1, d)

    for l in range(DEPTH):
        h = rmsnorm(x, g_attn[l])
        proj = h @ w_in[l]
        p_cq, p_ckv, p_kr, p_b, p_c, p_d, p_f = jnp.split(proj, SPLIT_POINTS, axis=-1)

        qa = heads(rmsnorm(p_cq, g_q[l]) @ w_uq[l], MLA_NOPE_DIM + MLA_ROPE_DIM)
        q_nope = qa[..., :MLA_NOPE_DIM]
        q_rope = rope(qa[..., MLA_NOPE_DIM:], pos)
        c_kv = rmsnorm(p_ckv, g_kv[l])
        k_nope = heads(c_kv @ w_uk[l], MLA_NOPE_DIM)
        v_a = heads(c_kv @ w_uv[l])
        k_rope = rope(p_kr[:, :, None, :], pos)[:, :, 0, :]
        o_a = mla_attention(q_nope, q_rope, k_nope, k_rope, v_a)

        qb, kb, vb = jnp.split(p_b, 3, axis=-1)
        o_b = dilated_attention(partial_rope(heads(qb), pos), partial_rope(heads(kb), pos), heads(vb))

        qc, kc, vc = jnp.split(p_c, 3, axis=-1)
        o_c = stick_breaking_attention(heads(qc), heads(kc), heads(vc))

        qd, kd, vd = jnp.split(p_d, 3, axis=-1)
        logf = jax.nn.log_sigmoid(p_f.astype(jnp.float32) + b_f[l].astype(jnp.float32))
        o_d = forgetting_attention(heads(qd), heads(kd), heads(vd), jnp.cumsum(logf, axis=1))

        groups = []
        for gi, o in enumerate((o_a, o_b, o_c, o_d)):
            o = o.reshape(b, s_len, GROUP_WIDTH).astype(x.dtype)
            groups.append(rmsnorm(o, g_out[l, gi * GROUP_WIDTH:(gi + 1) * GROUP_WIDTH]))
        x = x + jnp.concatenate(groups, axis=-1) @ w_o[l]

        h = rmsnorm(x, g_mlp[l])
        x = x + jnp.square(jax.nn.relu(h @ w_up[l])) @ w_down[l]

    return rmsnorm(x, g_final)
```

```python
import functools
import math

import jax
import jax.numpy as jnp
import numpy as np
from jax import lax
from jax.experimental import pallas as pl
from jax.experimental.pallas import tpu as pltpu

F32 = jnp.float32
BF16 = jnp.bfloat16

HEAD_DIM = 128
N_HEADS = 8
GROUP_WIDTH = N_HEADS * HEAD_DIM
Q_LORA_RANK = 896
KV_LORA_RANK = 512
MLA_ROPE_DIM = 64
ROPE_THETA = 500000.0
PARTIAL_ROPE_DIM = HEAD_DIM // 4
DILATED_PAIRS = ((128, 1), (512, 4), (2048, 16))
EPS = 1e-6
NEG_INF = -1e30

LANES = 128
VMEM_LIMIT_BYTES = 56 * 1024 * 1024

F32_EXP_UNDERFLOW = 104.0


def _cparams(semantics):
    return pltpu.CompilerParams(dimension_semantics=semantics, vmem_limit_bytes=VMEM_LIMIT_BYTES)


def _rms(x, g):
    return x * lax.rsqrt(jnp.mean(x * x, axis=-1, keepdims=True) + EPS) * g


def _rmsnorm_kernel(x_ref, g_ref, o_ref):
    o_ref[...] = _rms(x_ref[...], g_ref[...]).astype(o_ref.dtype)


def rmsnorm(x, g, out_dtype, tm=256):
    m, d = x.shape
    return pl.pallas_call(
        _rmsnorm_kernel,
        grid=(m // tm,),
        in_specs=[pl.BlockSpec((tm, d), lambda i: (i, 0)), pl.BlockSpec((1, d), lambda i: (0, 0))],
        out_specs=pl.BlockSpec((tm, d), lambda i: (i, 0)),
        out_shape=jax.ShapeDtypeStruct((m, d), out_dtype),
        compiler_params=_cparams(("parallel",)),
        name="rmsnorm",
    )(x, g.reshape(1, d))


def _mla_norms_kernel(x_ref, gq_ref, gkv_ref, oq_ref, okv_ref):
    oq_ref[...] = _rms(x_ref[:, :Q_LORA_RANK], gq_ref[...]).astype(oq_ref.dtype)
    okv_ref[...] = _rms(x_ref[:, Q_LORA_RANK:Q_LORA_RANK + KV_LORA_RANK], gkv_ref[...]).astype(okv_ref.dtype)


def mla_norms(seg_a, g_q, g_kv, tm=512):
    m, w = seg_a.shape
    return pl.pallas_call(
        _mla_norms_kernel,
        grid=(m // tm,),
        in_specs=[pl.BlockSpec((tm, w), lambda i: (i, 0)),
                  pl.BlockSpec((1, Q_LORA_RANK), lambda i: (0, 0)),
                  pl.BlockSpec((1, KV_LORA_RANK), lambda i: (0, 0))],
        out_specs=[pl.BlockSpec((tm, Q_LORA_RANK), lambda i: (i, 0)),
                   pl.BlockSpec((tm, KV_LORA_RANK), lambda i: (i, 0))],
        out_shape=[jax.ShapeDtypeStruct((m, Q_LORA_RANK), BF16),
                   jax.ShapeDtypeStruct((m, KV_LORA_RANK), BF16)],
        compiler_params=_cparams(("parallel",)),
        name="mla_norms",
    )(seg_a, g_q.reshape(1, -1), g_kv.reshape(1, -1))


def _group_norm_kernel(a_ref, b_ref, c_ref, d_ref, g_ref, o_ref):
    for gi, r in enumerate((a_ref, b_ref, c_ref, d_ref)):
        sl = slice(gi * GROUP_WIDTH, (gi + 1) * GROUP_WIDTH)
        o_ref[:, sl] = _rms(r[...], g_ref[:, sl]).astype(o_ref.dtype)


def group_norm_concat(o_a, o_b, o_c, o_d, g, tm=256):
    m = o_a.shape[0]
    spec = pl.BlockSpec((tm, GROUP_WIDTH), lambda i: (i, 0))
    return pl.pallas_call(
        _group_norm_kernel,
        grid=(m // tm,),
        in_specs=[spec, spec, spec, spec, pl.BlockSpec((1, 4 * GROUP_WIDTH), lambda i: (0, 0))],
        out_specs=pl.BlockSpec((tm, 4 * GROUP_WIDTH), lambda i: (i, 0)),
        out_shape=jax.ShapeDtypeStruct((m, 4 * GROUP_WIDTH), BF16),
        compiler_params=_cparams(("parallel",)),
        name="group_norm",
    )(o_a, o_b, o_c, o_d, g.reshape(1, -1))


def _mm_kernel(*refs, nk, act, has_res):
    a_ref, w_ref = refs[0], refs[1]
    res_ref = refs[2] if has_res else None
    o_ref = refs[2 + has_res]

    def finish(acc):
        if act == "relu2":
            acc = jnp.square(jnp.maximum(acc, 0.0))
        if has_res:
            acc = res_ref[...] + acc
        o_ref[...] = acc.astype(o_ref.dtype)

    if nk == 1:
        finish(jnp.dot(a_ref[...], w_ref[...], preferred_element_type=F32))
    else:
        acc_ref = refs[3 + has_res]
        k = pl.program_id(2)

        @pl.when(k == 0)
        def _():
            acc_ref[...] = jnp.zeros_like(acc_ref)

        acc_ref[...] += jnp.dot(a_ref[...], w_ref[...], preferred_element_type=F32)

        @pl.when(k == nk - 1)
        def _():
            finish(acc_ref[...])


def _pick(n, prefs):
    for p in prefs:
        if n % p == 0:
            return p
    return n


def matmul(a, w, out_dtype, act=None, res=None, tm=None, tn=None, tk=None, name="matmul"):
    m, kdim = a.shape
    _, n = w.shape
    tm = tm or _pick(m, (1024, 512, 256))
    tn = tn or _pick(n, (1024, 768, 512, 256, 128))
    has_res = res is not None
    tk = tk or (kdim if (kdim <= 4096 and not has_res) else _pick(kdim, (2048, 1024)))
    nk = kdim // tk
    in_specs = [pl.BlockSpec((tm, tk), lambda i, j, k: (i, k)),
                pl.BlockSpec((tk, tn), lambda i, j, k: (k, j))]
    args = [a, w]
    if has_res:
        in_specs.append(pl.BlockSpec((tm, tn), lambda i, j, k: (i, j)))
        args.append(res)
    scratch = [pltpu.VMEM((tm, tn), F32)] if nk > 1 else []
    return pl.pallas_call(
        functools.partial(_mm_kernel, nk=nk, act=act, has_res=has_res),
        grid=(m // tm, n // tn, nk),
        in_specs=in_specs,
        out_specs=pl.BlockSpec((tm, tn), lambda i, j, k: (i, j)),
        out_shape=jax.ShapeDtypeStruct((m, n), out_dtype),
        scratch_shapes=scratch,
        compiler_params=_cparams(("parallel", "parallel", "arbitrary")),
        name=name,
    )(*args)


def _rope_tables(s_len, r, period, pass_through):
    half = r // 2
    inv = ROPE_THETA ** (-jnp.arange(half, dtype=F32) * (2.0 / r))
    ang = jnp.arange(s_len, dtype=F32)[:, None] * inv[None, :]
    cos, sin = jnp.cos(ang), jnp.sin(ang)
    fill = jnp.ones if pass_through else jnp.zeros
    zeros = jnp.zeros((s_len, period - r), F32)
    zh = jnp.zeros((s_len, half), F32)
    c = jnp.concatenate([cos, cos, fill((s_len, period - r), F32)], axis=1)
    sa = jnp.concatenate([-sin, zh, zeros], axis=1)
    sb = jnp.concatenate([zh, sin, zeros], axis=1)
    return c, sa, sb


def _rope(x, c, sa, sb, half):
    w = x.shape[-1]
    reps = w // c.shape[-1]
    if reps > 1:
        c, sa, sb = (jnp.tile(t, (1, reps)) for t in (c, sa, sb))
    return x * c + pltpu.roll(x, w - half, 1) * sa + pltpu.roll(x, half, 1) * sb


def _rope_kernel(x_ref, c_ref, sa_ref, sb_ref, o_ref, *, half):
    o_ref[...] = _rope(x_ref[...], c_ref[...], sa_ref[...], sb_ref[...], half).astype(o_ref.dtype)


def rope_cast(x, tables, half, tm=512):
    m, w = x.shape
    period = tables[0].shape[1]
    tspec = pl.BlockSpec((tm, period), lambda i: (i, 0))
    return pl.pallas_call(
        functools.partial(_rope_kernel, half=half),
        grid=(m // tm,),
        in_specs=[pl.BlockSpec((tm, w), lambda i: (i, 0)), tspec, tspec, tspec],
        out_specs=pl.BlockSpec((tm, w), lambda i: (i, 0)),
        out_shape=jax.ShapeDtypeStruct((m, w), BF16),
        compiler_params=_cparams(("parallel",)),
        name="rope_cast",
    )(x, *tables)


def _mla_cat_kernel(qa_ref, kn_ref, kr_ref, c_ref, sa_ref, sb_ref, q_ref, k_ref):
    half = MLA_ROPE_DIM // 2
    c, sa, sb = c_ref[...], sa_ref[...], sb_ref[...]
    qr = _rope(qa_ref[:, GROUP_WIDTH:], c, sa, sb, half).astype(BF16)
    kr = _rope(kr_ref[...], c, sa, sb, half).astype(BF16)
    for h in range(N_HEADS):
        lo = h * 2 * HEAD_DIM
        q_ref[:, lo:lo + HEAD_DIM] = qa_ref[:, h * HEAD_DIM:(h + 1) * HEAD_DIM].astype(BF16)
        q_ref[:, lo + HEAD_DIM:lo + 2 * HEAD_DIM] = qr[:, h * HEAD_DIM:(h + 1) * HEAD_DIM]
        k_ref[:, lo:lo + HEAD_DIM] = kn_ref[:, h * HEAD_DIM:(h + 1) * HEAD_DIM]
        k_ref[:, lo + HEAD_DIM:lo + 2 * HEAD_DIM] = kr


def mla_cat(qa, kv, seg_a, tables, tm=512):
    m = qa.shape[0]
    kr_block = (Q_LORA_RANK + KV_LORA_RANK) // LANES
    tspec = pl.BlockSpec((tm, LANES), lambda i: (i, 0))
    out_w = N_HEADS * 2 * HEAD_DIM
    return pl.pallas_call(
        _mla_cat_kernel,
        grid=(m // tm,),
        in_specs=[pl.BlockSpec((tm, 2 * GROUP_WIDTH), lambda i: (i, 0)),
                  pl.BlockSpec((tm, GROUP_WIDTH), lambda i: (i, 0)),
                  pl.BlockSpec((tm, LANES), lambda i: (i, kr_block)),
                  tspec, tspec, tspec],
        out_specs=[pl.BlockSpec((tm, out_w), lambda i: (i, 0)), pl.BlockSpec((tm, out_w), lambda i: (i, 0))],
        out_shape=[jax.ShapeDtypeStruct((m, out_w), BF16), jax.ShapeDtypeStruct((m, out_w), BF16)],
        compiler_params=_cparams(("parallel",)),
        name="mla_cat",
    )(qa, kv, seg_a, *tables)


def _split3(x):
    hi = x.astype(BF16)
    r1 = x - hi.astype(F32)
    mid = r1.astype(BF16)
    lo = (r1 - mid.astype(F32)).astype(BF16)
    return hi, mid, lo


def _forget_kernel(x_ref, b_ref, u_ref, o_ref, carry_ref, *, row0):
    @pl.when(pl.program_id(0) == 0)
    def _():
        carry_ref[...] = jnp.zeros_like(carry_ref)

    xt = jnp.transpose(x_ref[...])
    y = -(xt[row0:row0 + N_HEADS, :] + b_ref[...])
    logf = -(jnp.maximum(y, 0.0) + jnp.log(1.0 + jnp.exp(-jnp.abs(y))))
    u = u_ref[...]
    cs = sum(jnp.dot(p, u, preferred_element_type=F32) for p in _split3(logf))
    c = cs + carry_ref[:, 0:1]
    o_ref[...] = c
    carry_ref[...] = jnp.broadcast_to(c[:, -1:], carry_ref.shape)


def forget_cumsum(seg_a, b_f, tc=512):
    m = seg_a.shape[0]
    blk = (Q_LORA_RANK + KV_LORA_RANK) // LANES
    u = jnp.triu(jnp.ones((tc, tc), F32)).astype(BF16)
    return pl.pallas_call(
        functools.partial(_forget_kernel, row0=MLA_ROPE_DIM),
        grid=(m // tc,),
        in_specs=[pl.BlockSpec((tc, LANES), lambda i: (i, blk)),
                  pl.BlockSpec((N_HEADS, 1), lambda i: (0, 0)),
                  pl.BlockSpec((tc, tc), lambda i: (0, 0))],
        out_specs=pl.BlockSpec((N_HEADS, tc), lambda i: (0, i)),
        out_shape=jax.ShapeDtypeStruct((N_HEADS, m), F32),
        scratch_shapes=[pltpu.VMEM((N_HEADS, LANES), F32)],
        compiler_params=_cparams(("arbitrary",)),
        name="forget_cumsum",
    )(seg_a, b_f.reshape(N_HEADS, 1), u)


def _flash_kernel(*refs, bq, bk, scale, window_blocks, has_tiles, has_decay):
    q_ref, k_ref, v_ref = refs[:3]
    idx = 3
    tile_ref = cq_ref = ck_ref = None
    if has_tiles:
        tile_ref = refs[idx]
        idx += 1
    if has_decay:
        cq_ref, ck_ref = refs[idx], refs[idx + 1]
        idx += 2
    o_ref = refs[idx]

    qi = pl.program_id(1)
    q = q_ref[...]
    if has_decay:
        cq = jnp.transpose(jnp.broadcast_to(cq_ref[...], (LANES, bq)))
        cq = jnp.tile(cq, (1, bk // LANES))

    def step(j, carry, tile):
        m, l, acc = carry
        start = pl.multiple_of(j * bk, bk)
        k = k_ref[pl.ds(start, bk), :]
        v = v_ref[pl.ds(start, bk), :]
        s = lax.dot_general(q, k, (((1,), (1,)), ((), ())), preferred_element_type=F32) * scale
        if has_decay:
            s = s + (cq - ck_ref[pl.ds(j, 1), :])
        if tile is not None:
            s = s + tile
        m_new = jnp.maximum(m, jnp.max(s, axis=-1, keepdims=True))
        alpha = jnp.exp(m - m_new)
        p = jnp.exp(s - m_new)
        l = alpha * l + jnp.sum(p, axis=-1, keepdims=True)
        acc = alpha * acc + jnp.dot(p.astype(BF16), v, preferred_element_type=F32)
        return m_new, l, acc

    carry = (jnp.full((bq, 1), NEG_INF, F32), jnp.zeros((bq, 1), F32), jnp.zeros((bq, v_ref.shape[-1]), F32))
    if window_blocks is None:
        lo = 0
        body = lambda j, c: step(j, c, None)
    else:
        lo = jnp.maximum(qi - (window_blocks - 1), 0)
        body = lambda j, c: step(j, c, tile_ref[qi - j])
    carry = lax.fori_loop(lo, qi, body, carry)
    m, l, acc = step(qi, carry, tile_ref[0])
    o_ref[...] = (acc / l).astype(o_ref.dtype)


def flash_attention(q_arr, q_cb, k_arr, k_cb, v_arr, v_cb, *, dqk, scale, tiles, window_blocks=None,
                    decay=None, bq=512):
    s_len = q_arr.shape[0]
    bk = bq
    nq = s_len // bq
    in_specs = [pl.BlockSpec((bq, dqk), lambda h, i: (i, q_cb + h)),
                pl.BlockSpec((s_len, dqk), lambda h, i: (0, k_cb + h)),
                pl.BlockSpec((s_len, HEAD_DIM), lambda h, i: (0, v_cb + h)),
                pl.BlockSpec(tiles.shape, lambda h, i: (0, 0, 0))]
    args = [q_arr, k_arr, v_arr, tiles]
    if decay is not None:
        in_specs += [pl.BlockSpec((None, None, 1, bq), lambda h, i: (h, i, 0, 0)),
                     pl.BlockSpec((None, nq, bk), lambda h, i: (h, 0, 0))]
        args += [decay.reshape(N_HEADS, nq, 1, bq), decay.reshape(N_HEADS, nq, bk)]
    return pl.pallas_call(
        functools.partial(_flash_kernel, bq=bq, bk=bk, scale=scale, window_blocks=window_blocks,
                          has_tiles=True, has_decay=decay is not None),
        grid=(N_HEADS, nq),
        in_specs=in_specs,
        out_specs=pl.BlockSpec((bq, HEAD_DIM), lambda h, i: (i, h)),
        out_shape=jax.ShapeDtypeStruct((s_len, GROUP_WIDTH), F32),
        compiler_params=_cparams(("parallel", "arbitrary")),
        name="flash_attention",
    )(*args)


def _causal_tile(bq):
    d = jnp.arange(bq)[:, None] - jnp.arange(bq)[None, :]
    return jnp.where(d >= 0, 0.0, NEG_INF).astype(F32)[None]


def _dilated_tiles(bq):
    max_window = max(w for w, _ in DILATED_PAIRS)
    n = -(-max_window // bq) + 1
    d = (jnp.arange(n)[:, None, None] * bq + jnp.arange(bq)[None, :, None] - jnp.arange(bq)[None, None, :])
    count = sum(((d >= 0) & (d <= w) & (d % dil == 0)).astype(F32) for w, dil in DILATED_PAIRS)
    return jnp.where(count > 0, jnp.log(jnp.maximum(count, 1.0)), NEG_INF).astype(F32), n


def _stick_kernel(q_ref, k_ref, v_ref, t_ref, o_ref, *, bq, bk, scale):
    qi = pl.program_id(1)
    q = q_ref[...]
    tri = t_ref[...]

    def block(j, r, acc, diagonal):
        start = pl.multiple_of(j * bk, bk)
        k = k_ref[pl.ds(start, bk), :]
        v = v_ref[pl.ds(start, bk), :]
        z = lax.dot_general(q, k, (((1,), (1,)), ((), ())), preferred_element_type=F32) * scale
        sp = jnp.maximum(z, 0.0) + jnp.log(1.0 + jnp.exp(-jnp.abs(z)))
        log_1m = -sp
        if diagonal:
            mask = (lax.broadcasted_iota(jnp.int32, (bq, bk), 1) < lax.broadcasted_iota(jnp.int32, (bq, bk), 0))
            log_1m = jnp.where(mask, log_1m, 0.0)
        hi = log_1m.astype(BF16)
        lo = (log_1m - hi.astype(F32)).astype(BF16)
        after = (jnp.dot(hi, tri, preferred_element_type=F32) + jnp.dot(lo, tri, preferred_element_type=F32)) + r
        a = jnp.exp((z - sp) + after)
        if diagonal:
            a = jnp.where(mask, a, 0.0)
        acc = acc + jnp.dot(a.astype(BF16), v, preferred_element_type=F32)
        r = r + jnp.sum(log_1m, axis=-1, keepdims=True)
        return r, acc

    r0 = jnp.zeros((bq, 1), F32)
    acc0 = jnp.zeros((bq, v_ref.shape[-1]), F32)
    r, acc = block(qi, r0, acc0, True)

    def cond(c):
        j, r, _ = c
        return jnp.logical_and(j >= 0, jnp.max(r) > -F32_EXP_UNDERFLOW)

    def body(c):
        j, r, acc = c
        r, acc = block(j, r, acc, False)
        return j - 1, r, acc

    _, _, acc = lax.while_loop(cond, body, (qi - 1, r, acc))
    o_ref[...] = acc.astype(o_ref.dtype)


def stick_breaking_attention(qkv, q_cb, k_cb, v_cb, *, scale, bq=256):
    s_len = qkv.shape[0]
    bk = bq
    tri = jnp.tril(jnp.ones((bk, bk), F32), -1).astype(BF16)
    return pl.pallas_call(
        functools.partial(_stick_kernel, bq=bq, bk=bk, scale=scale),
        grid=(N_HEADS, s_len // bq),
        in_specs=[pl.BlockSpec((bq, HEAD_DIM), lambda h, i: (i, q_cb + h)),
                  pl.BlockSpec((s_len, HEAD_DIM), lambda h, i: (0, k_cb + h)),
                  pl.BlockSpec((s_len, HEAD_DIM), lambda h, i: (0, v_cb + h)),
                  pl.BlockSpec((bk, bk), lambda h, i: (0, 0))],
        out_specs=pl.BlockSpec((bq, HEAD_DIM), lambda h, i: (i, h)),
        out_shape=jax.ShapeDtypeStruct((s_len, GROUP_WIDTH), F32),
        compiler_params=_cparams(("parallel", "arbitrary")),
        name="stick_breaking",
    )(qkv, qkv, qkv, tri)


def _prep_layer_weights(w_in, w_uq, w_uk, w_uv):
    d_model = w_in.shape[0]
    n_a = Q_LORA_RANK + KV_LORA_RANK + MLA_ROPE_DIM
    gate0 = n_a + 9 * GROUP_WIDTH
    pad = jnp.zeros((d_model, LANES - MLA_ROPE_DIM - N_HEADS), w_in.dtype)
    w_a = jnp.concatenate([w_in[:, :n_a], w_in[:, gate0:gate0 + N_HEADS], pad], axis=1).astype(BF16)
    w_bqk = w_in[:, n_a:n_a + 2 * GROUP_WIDTH].astype(BF16)
    w_rest = w_in[:, n_a + 2 * GROUP_WIDTH:gate0].astype(BF16)
    wq = w_uq.reshape(Q_LORA_RANK, N_HEADS, HEAD_DIM + MLA_ROPE_DIM)
    wq_nope = wq[:, :, :HEAD_DIM].reshape(Q_LORA_RANK, GROUP_WIDTH)
    wq_rope = jnp.pad(wq[:, :, HEAD_DIM:], ((0, 0), (0, 0), (0, HEAD_DIM - MLA_ROPE_DIM))).reshape(Q_LORA_RANK, GROUP_WIDTH)
    w_q = jnp.concatenate([wq_nope, wq_rope], axis=1).astype(BF16)
    w_kv = jnp.concatenate([w_uk, w_uv], axis=1).astype(BF16)
    return w_a, w_bqk, w_rest, w_q, w_kv


def kernel(x, g_attn, w_in, g_q, g_kv, w_uq, w_uk, w_uv, b_f, g_out, w_o, g_mlp, w_up, w_down, g_final):
    b, s_len, d_model = x.shape
    assert b == 1 and s_len % 2048 == 0
    depth = w_in.shape[0]
    xf = x.reshape(s_len, d_model)

    bq = 512
    mla_tables = _rope_tables(s_len, MLA_ROPE_DIM, LANES, pass_through=False)
    dil_tables = _rope_tables(s_len, PARTIAL_ROPE_DIM, HEAD_DIM, pass_through=True)
    causal = _causal_tile(bq)
    dil_tiles, dil_blocks = _dilated_tiles(bq)
    scale = HEAD_DIM ** -0.5
    mla_scale = (HEAD_DIM + MLA_ROPE_DIM) ** -0.5
    nh = N_HEADS

    for l in range(depth):
        w_a, w_bqk, w_rest, w_q, w_kv = _prep_layer_weights(w_in[l], w_uq[l], w_uk[l], w_uv[l])

        h = rmsnorm(xf, g_attn[l], BF16)
        seg_a = matmul(h, w_a, F32)
        bqk = matmul(h, w_bqk, F32)
        rest = matmul(h, w_rest, BF16)

        cq_n, ckv_n = mla_norms(seg_a, g_q[l], g_kv[l])
        qa = matmul(cq_n, w_q, F32)
        kv = matmul(ckv_n, w_kv, BF16)
        q_cat, k_cat = mla_cat(qa, kv, seg_a, mla_tables)
        o_a = flash_attention(q_cat, 0, k_cat, 0, kv, nh, dqk=2 * HEAD_DIM, scale=mla_scale, tiles=causal, bq=bq)

        bqk_r = rope_cast(bqk, dil_tables, PARTIAL_ROPE_DIM // 2)
        o_b = flash_attention(bqk_r, 0, bqk_r, nh, rest, 0, dqk=HEAD_DIM, scale=scale, tiles=dil_tiles,
                              window_blocks=dil_blocks, bq=bq)

        o_c = stick_breaking_attention(rest, nh, 2 * nh, 3 * nh, scale=scale)

        c_f = forget_cumsum(seg_a, b_f[l])
        o_d = flash_attention(rest, 4 * nh, rest, 5 * nh, rest, 6 * nh, dqk=HEAD_DIM, scale=scale, tiles=causal,
                              decay=c_f, bq=bq)

        mix = group_norm_concat(o_a, o_b, o_c, o_d, g_out[l])
        xf = matmul(mix, w_o[l].astype(BF16), F32, res=xf)

        h2 = rmsnorm(xf, g_mlp[l], BF16)
        u = matmul(h2, w_up[l].astype(BF16), BF16, act="relu2")
        xf = matmul(u, w_down[l].astype(BF16), F32, res=xf)

    return rmsnorm(xf, g_final, F32).reshape(b, s_len, d_model)
```

```python
import functools
import math

import jax
import jax.numpy as jnp
from jax import lax
from jax.experimental import pallas as pl
from jax.experimental.pallas import tpu as pltpu

F32 = jnp.float32
BF16 = jnp.bfloat16

HEAD_DIM = 128
N_HEADS = 8
GROUP_WIDTH = N_HEADS * HEAD_DIM
Q_LORA_RANK = 896
KV_LORA_RANK = 512
MLA_ROPE_DIM = 64
ROPE_THETA = 500000.0
PARTIAL_ROPE_DIM = HEAD_DIM // 4
DILATED_PAIRS = ((128, 1), (512, 4), (2048, 16))
EPS = 1e-6
NEG_INF = -1e30

LANES = 128
VMEM_LIMIT_BYTES = 56 * 1024 * 1024

F32_EXP_UNDERFLOW = 104.0
LOG2E = math.log2(math.e)

FLASH_BQ, FLASH_BK = 1024, 1024
DILATED_BQ, DILATED_BK = 512, 512


def _cparams(semantics):
    return pltpu.CompilerParams(dimension_semantics=semantics, vmem_limit_bytes=VMEM_LIMIT_BYTES)


def _rms(x, g):
    return x * lax.rsqrt(jnp.mean(x * x, axis=-1, keepdims=True) + EPS) * g


def _rmsnorm_kernel(x_ref, g_ref, o_ref):
    o_ref[...] = _rms(x_ref[...], g_ref[...]).astype(o_ref.dtype)


def rmsnorm(x, g, out_dtype, tm=256):
    m, d = x.shape
    return pl.pallas_call(
        _rmsnorm_kernel,
        grid=(m // tm,),
        in_specs=[pl.BlockSpec((tm, d), lambda i: (i, 0)), pl.BlockSpec((1, d), lambda i: (0, 0))],
        out_specs=pl.BlockSpec((tm, d), lambda i: (i, 0)),
        out_shape=jax.ShapeDtypeStruct((m, d), out_dtype),
        compiler_params=_cparams(("parallel",)),
        name="rmsnorm",
    )(x, g.reshape(1, d))


def _mla_norms_kernel(x_ref, gq_ref, gkv_ref, oq_ref, okv_ref):
    oq_ref[...] = _rms(x_ref[:, :Q_LORA_RANK], gq_ref[...]).astype(oq_ref.dtype)
    okv_ref[...] = _rms(x_ref[:, Q_LORA_RANK:Q_LORA_RANK + KV_LORA_RANK], gkv_ref[...]).astype(okv_ref.dtype)


def mla_norms(seg_a, g_q, g_kv, tm=512):
    m, w = seg_a.shape
    return pl.pallas_call(
        _mla_norms_kernel,
        grid=(m // tm,),
        in_specs=[pl.BlockSpec((tm, w), lambda i: (i, 0)),
                  pl.BlockSpec((1, Q_LORA_RANK), lambda i: (0, 0)),
                  pl.BlockSpec((1, KV_LORA_RANK), lambda i: (0, 0))],
        out_specs=[pl.BlockSpec((tm, Q_LORA_RANK), lambda i: (i, 0)),
                   pl.BlockSpec((tm, KV_LORA_RANK), lambda i: (i, 0))],
        out_shape=[jax.ShapeDtypeStruct((m, Q_LORA_RANK), BF16),
                   jax.ShapeDtypeStruct((m, KV_LORA_RANK), BF16)],
        compiler_params=_cparams(("parallel",)),
        name="mla_norms",
    )(seg_a, g_q.reshape(1, -1), g_kv.reshape(1, -1))


def _group_norm_kernel(a_ref, b_ref, c_ref, d_ref, g_ref, o_ref):
    for gi, r in enumerate((a_ref, b_ref, c_ref, d_ref)):
        sl = slice(gi * GROUP_WIDTH, (gi + 1) * GROUP_WIDTH)
        o_ref[:, sl] = _rms(r[...], g_ref[:, sl]).astype(o_ref.dtype)


def group_norm_concat(o_a, o_b, o_c, o_d, g, tm=256):
    m = o_a.shape[0]
    spec = pl.BlockSpec((tm, GROUP_WIDTH), lambda i: (i, 0))
    return pl.pallas_call(
        _group_norm_kernel,
        grid=(m // tm,),
        in_specs=[spec, spec, spec, spec, pl.BlockSpec((1, 4 * GROUP_WIDTH), lambda i: (0, 0))],
        out_specs=pl.BlockSpec((tm, 4 * GROUP_WIDTH), lambda i: (i, 0)),
        out_shape=jax.ShapeDtypeStruct((m, 4 * GROUP_WIDTH), BF16),
        compiler_params=_cparams(("parallel",)),
        name="group_norm",
    )(o_a, o_b, o_c, o_d, g.reshape(1, -1))


def _mm_kernel(*refs, nk, act, has_res):
    a_ref, w_ref = refs[0], refs[1]
    res_ref = refs[2] if has_res else None
    o_ref = refs[2 + has_res]

    def finish(acc):
        if act == "relu2":
            acc = jnp.square(jnp.maximum(acc, 0.0))
        if has_res:
            acc = res_ref[...] + acc
        o_ref[...] = acc.astype(o_ref.dtype)

    if nk == 1:
        finish(jnp.dot(a_ref[...], w_ref[...], preferred_element_type=F32))
    elif act is None and o_ref.dtype == F32:
        k = pl.program_id(2)

        @pl.when(k == 0)
        def _():
            o_ref[...] = res_ref[...] if has_res else jnp.zeros_like(o_ref)

        o_ref[...] += jnp.dot(a_ref[...], w_ref[...], preferred_element_type=F32)
    else:
        acc_ref = refs[3 + has_res]
        k = pl.program_id(2)

        @pl.when(k == 0)
        def _():
            acc_ref[...] = jnp.zeros_like(acc_ref)

        acc_ref[...] += jnp.dot(a_ref[...], w_ref[...], preferred_element_type=F32)

        @pl.when(k == nk - 1)
        def _():
            finish(acc_ref[...])


def _pick(n, prefs):
    for p in prefs:
        if n % p == 0:
            return p
    return n


def matmul(a, w, out_dtype, act=None, res=None, layer=None, tm=None, tn=None, tk=None, name="matmul"):
    m, kdim = a.shape
    n = w.shape[-1]
    tm = tm or _pick(m, (1024, 512, 256))
    tn = tn or _pick(n, (1024, 768, 512, 256, 128))
    has_res = res is not None
    tk = tk or _pick(kdim, (4096, 2048, 1024))
    nk = kdim // tk
    w_spec = (pl.BlockSpec((tk, tn), lambda i, j, k: (k, j)) if layer is None else
              pl.BlockSpec((None, tk, tn), lambda i, j, k: (layer, k, j)))
    in_specs = [pl.BlockSpec((tm, tk), lambda i, j, k: (i, k)), w_spec]
    args = [a, w]
    if has_res:
        in_specs.append(pl.BlockSpec((tm, tn), lambda i, j, k: (i, j)))
        args.append(res)
    scratch = [pltpu.VMEM((tm, tn), F32)] if (nk > 1 and not (act is None and out_dtype == F32)) else []
    return pl.pallas_call(
        functools.partial(_mm_kernel, nk=nk, act=act, has_res=has_res),
        grid=(m // tm, n // tn, nk),
        in_specs=in_specs,
        out_specs=pl.BlockSpec((tm, tn), lambda i, j, k: (i, j)),
        out_shape=jax.ShapeDtypeStruct((m, n), out_dtype),
        scratch_shapes=scratch,
        compiler_params=_cparams(("parallel", "parallel", "arbitrary")),
        name=name,
    )(*args)


def _mmw_kernel(*refs, transposed, act, has_res, rope_half, row_chunk):
    a_ref, w_ref = refs[0], refs[1]
    idx = 2
    res_ref = tab_refs = None
    if has_res:
        res_ref = refs[idx]
        idx += 1
    if rope_half:
        tab_refs = refs[idx:idx + 3]
        idx += 3
    o_ref, wbf_ref = refs[idx], refs[idx + 1]

    @pl.when(pl.program_id(1) == 0)
    def _():
        def cast_rows(r, c):
            rows = pl.ds(pl.multiple_of(r * row_chunk, row_chunk), row_chunk)
            wbf_ref[rows, :] = w_ref[0, rows, :].astype(BF16)
            return c
        lax.fori_loop(0, w_ref.shape[1] // row_chunk, cast_rows, 0)

    dims = (((1,), (1,)), ((), ())) if transposed else (((1,), (0,)), ((), ()))
    acc = lax.dot_general(a_ref[...], wbf_ref[...], dims, preferred_element_type=F32)
    if act == "relu2":
        acc = jnp.square(jnp.maximum(acc, 0.0))
    if rope_half:
        acc = _rope(acc, *(t[...] for t in tab_refs), rope_half)
    if has_res:
        acc = res_ref[...] + acc
    o_ref[...] = acc.astype(o_ref.dtype)


def matmul_f32w(a, w_stack, layer, col0, n, out_dtype, act=None, res=None, rope=None, transposed=False,
                tm=1024, tn=512, name="matmul_f32w"):
    m, kdim = a.shape
    has_res = res is not None
    if transposed:
        w_block, w_shape = (pl.Element(1), pl.Element(tn), pl.Element(kdim)), (tn, kdim)
        w_index = lambda j, i: (layer, (col0 // 8 + j * (tn // 8)) * 8, 0)
    else:
        assert col0 % LANES == 0
        w_block, w_shape = (pl.Element(1), pl.Element(kdim), pl.Element(tn)), (kdim, tn)
        w_index = lambda j, i: (layer, 0, (col0 // LANES + j * (tn // LANES)) * LANES)
    in_specs = [pl.BlockSpec((tm, kdim), lambda j, i: (i, 0)), pl.BlockSpec(w_block, w_index)]
    args = [a, w_stack]
    if has_res:
        in_specs.append(pl.BlockSpec((tm, tn), lambda j, i: (i, j)))
        args.append(res)
    rope_half = 0
    if rope is not None:
        tables, rope_half = rope
        in_specs += [pl.BlockSpec((tm, tables[0].shape[1]), lambda j, i: (i, 0))] * 3
        args += list(tables)
    return pl.pallas_call(
        functools.partial(_mmw_kernel, transposed=transposed, act=act, has_res=has_res, rope_half=rope_half,
                          row_chunk=min(256, w_shape[0])),
        grid=(n // tn, m // tm),
        in_specs=in_specs,
        out_specs=pl.BlockSpec((tm, tn), lambda j, i: (i, j)),
        out_shape=jax.ShapeDtypeStruct((m, n), out_dtype),
        scratch_shapes=[pltpu.VMEM(w_shape, BF16)],
        compiler_params=_cparams(("parallel", "arbitrary")),
        name=name,
    )(*args)


def _rope_tables(s_len, r, period, pass_through):
    half = r // 2
    inv = ROPE_THETA ** (-jnp.arange(half, dtype=F32) * (2.0 / r))
    ang = jnp.arange(s_len, dtype=F32)[:, None] * inv[None, :]
    cos, sin = jnp.cos(ang), jnp.sin(ang)
    fill = jnp.ones if pass_through else jnp.zeros
    zeros = jnp.zeros((s_len, period - r), F32)
    zh = jnp.zeros((s_len, half), F32)
    c = jnp.concatenate([cos, cos, fill((s_len, period - r), F32)], axis=1)
    sa = jnp.concatenate([-sin, zh, zeros], axis=1)
    sb = jnp.concatenate([zh, sin, zeros], axis=1)
    return c, sa, sb


def _rope(x, c, sa, sb, half):
    w = x.shape[-1]
    reps = w // c.shape[-1]
    if reps > 1:
        c, sa, sb = (jnp.tile(t, (1, reps)) for t in (c, sa, sb))
    return x * c + pltpu.roll(x, w - half, 1) * sa + pltpu.roll(x, half, 1) * sb


def _mla_cat_kernel(qa_ref, kn_ref, kr_ref, c_ref, sa_ref, sb_ref, q_ref, k_ref):
    half = MLA_ROPE_DIM // 2
    c, sa, sb = c_ref[...], sa_ref[...], sb_ref[...]
    qr = _rope(qa_ref[:, GROUP_WIDTH:], c, sa, sb, half).astype(BF16)
    kr = _rope(kr_ref[...], c, sa, sb, half).astype(BF16)
    for h in range(N_HEADS):
        lo = h * 2 * HEAD_DIM
        q_ref[:, lo:lo + HEAD_DIM] = qa_ref[:, h * HEAD_DIM:(h + 1) * HEAD_DIM].astype(BF16)
        q_ref[:, lo + HEAD_DIM:lo + 2 * HEAD_DIM] = qr[:, h * HEAD_DIM:(h + 1) * HEAD_DIM]
        k_ref[:, lo:lo + HEAD_DIM] = kn_ref[:, h * HEAD_DIM:(h + 1) * HEAD_DIM]
        k_ref[:, lo + HEAD_DIM:lo + 2 * HEAD_DIM] = kr


def mla_cat(qa, kv, seg_a, tables, tm=512):
    m = qa.shape[0]
    kr_block = (Q_LORA_RANK + KV_LORA_RANK) // LANES
    tspec = pl.BlockSpec((tm, LANES), lambda i: (i, 0))
    out_w = N_HEADS * 2 * HEAD_DIM
    return pl.pallas_call(
        _mla_cat_kernel,
        grid=(m // tm,),
        in_specs=[pl.BlockSpec((tm, 2 * GROUP_WIDTH), lambda i: (i, 0)),
                  pl.BlockSpec((tm, GROUP_WIDTH), lambda i: (i, 0)),
                  pl.BlockSpec((tm, LANES), lambda i: (i, kr_block)),
                  tspec, tspec, tspec],
        out_specs=[pl.BlockSpec((tm, out_w), lambda i: (i, 0)), pl.BlockSpec((tm, out_w), lambda i: (i, 0))],
        out_shape=[jax.ShapeDtypeStruct((m, out_w), BF16), jax.ShapeDtypeStruct((m, out_w), BF16)],
        compiler_params=_cparams(("parallel",)),
        name="mla_cat",
    )(qa, kv, seg_a, *tables)


def _split3(x):
    hi = x.astype(BF16)
    r1 = x - hi.astype(F32)
    mid = r1.astype(BF16)
    lo = (r1 - mid.astype(F32)).astype(BF16)
    return hi, mid, lo


def _forget_kernel(x_ref, b_ref, u_ref, o_ref, carry_ref, *, out_scale):
    @pl.when(pl.program_id(0) == 0)
    def _():
        carry_ref[...] = jnp.zeros_like(carry_ref)

    xt = jnp.transpose(x_ref[...])
    y = -(xt[LANES - N_HEADS:, :] + b_ref[...])
    logf = -(jnp.maximum(y, 0.0) + jnp.log(1.0 + jnp.exp(-jnp.abs(y))))
    u = u_ref[...]
    cs = sum(jnp.dot(p, u, preferred_element_type=F32) for p in _split3(logf))
    c = cs + carry_ref[:, 0:1]
    o_ref[...] = c * out_scale
    carry_ref[...] = jnp.broadcast_to(c[:, -1:], carry_ref.shape)


def forget_cumsum(gate, b_f, tc=512, out_scale=1.0):
    m = gate.shape[0]
    u = jnp.triu(jnp.ones((tc, tc), F32)).astype(BF16)
    return pl.pallas_call(
        functools.partial(_forget_kernel, out_scale=out_scale),
        grid=(m // tc,),
        in_specs=[pl.BlockSpec((tc, LANES), lambda i: (i, 0)),
                  pl.BlockSpec((N_HEADS, 1), lambda i: (0, 0)),
                  pl.BlockSpec((tc, tc), lambda i: (0, 0))],
        out_specs=pl.BlockSpec((N_HEADS, tc), lambda i: (0, i)),
        out_shape=jax.ShapeDtypeStruct((N_HEADS, m), F32),
        scratch_shapes=[pltpu.VMEM((N_HEADS, LANES), F32)],
        compiler_params=_cparams(("arbitrary",)),
        name="forget_cumsum",
    )(gate, b_f.reshape(N_HEADS, 1), u)


def _flash_kernel(*refs, bq, bk, dqk, scale2, window_blocks, has_decay, hps):
    q_ref, k_ref, v_ref, tile_ref = refs[:4]
    ck_ref = refs[4] if has_decay else None
    o_ref = refs[4 + has_decay]
    dv = HEAD_DIM
    n_diag = bq // bk
    qi = pl.program_id(1)
    qs = [q_ref[:, g * dqk:(g + 1) * dqk] for g in range(hps)]

    def head_step(g, j, carry, tile):
        m, l, acc = carry
        start = pl.multiple_of(j * bk, bk)
        k = k_ref[pl.ds(start, bk), g * dqk:(g + 1) * dqk]
        v = v_ref[pl.ds(start, bk), g * dv:(g + 1) * dv]
        s = lax.dot_general(qs[g], k, (((1,), (1,)), ((), ())), preferred_element_type=F32) * scale2
        if has_decay:
            s = s - ck_ref[g, pl.ds(j, 1), :]
        if tile is not None:
            s = s + tile
        m_new = jnp.maximum(m, jnp.max(s, axis=-1, keepdims=True))
        alpha = jnp.exp2(m - m_new)
        p = jnp.exp2(s - m_new)
        l = alpha * l + jnp.sum(p, axis=-1, keepdims=True)
        acc = alpha * acc + jnp.dot(p.astype(BF16), v, preferred_element_type=F32)
        return m_new, l, acc

    def step(j, carry, tile):
        return tuple(head_step(g, j, carry[g], tile) for g in range(hps))

    carry = tuple((jnp.full((bq, 1), NEG_INF, F32), jnp.zeros((bq, 1), F32), jnp.zeros((bq, dv), F32))
                  for _ in range(hps))
    j_diag = qi * n_diag
    if window_blocks is None:
        lo = 0
        body = lambda j, c: step(j, c, None)
    else:
        lo = jnp.maximum(j_diag - window_blocks, 0)
        body = lambda j, c: step(j, c, tile_ref[n_diag - 1 + j_diag - j])
    carry = lax.fori_loop(lo, j_diag, body, carry)
    for t in range(n_diag):
        carry = step(j_diag + t, carry, tile_ref[n_diag - 1 - t])
    for g, (m, l, acc) in enumerate(carry):
        o_ref[:, g * dv:(g + 1) * dv] = (acc / l).astype(o_ref.dtype)


def flash_attention(q_arr, q_cb, k_arr, k_cb, v_arr, v_cb, *, dqk, scale, tiles, window_blocks=None,
                    decay=None, hps=1):
    s_len = q_arr.shape[0]
    _, bq, bk = tiles.shape
    nq, nk = s_len // bq, s_len // bk
    assert q_cb % hps == 0 and k_cb % hps == 0 and v_cb % hps == 0
    in_specs = [pl.BlockSpec((bq, hps * dqk), lambda h, i: (i, q_cb // hps + h)),
                pl.BlockSpec((s_len, hps * dqk), lambda h, i: (0, k_cb // hps + h)),
                pl.BlockSpec((s_len, hps * HEAD_DIM), lambda h, i: (0, v_cb // hps + h)),
                pl.BlockSpec(tiles.shape, lambda h, i: (0, 0, 0))]
    args = [q_arr, k_arr, v_arr, tiles]
    if decay is not None:
        in_specs.append(pl.BlockSpec((hps, nk, bk), lambda h, i: (h, 0, 0)))
        args.append(decay.reshape(N_HEADS, nk, bk))
    return pl.pallas_call(
        functools.partial(_flash_kernel, bq=bq, bk=bk, dqk=dqk, scale2=scale * LOG2E,
                          window_blocks=window_blocks, has_decay=decay is not None, hps=hps),
        grid=(N_HEADS // hps, nq),
        in_specs=in_specs,
        out_specs=pl.BlockSpec((bq, hps * HEAD_DIM), lambda h, i: (i, h)),
        out_shape=jax.ShapeDtypeStruct((s_len, GROUP_WIDTH), F32),
        compiler_params=_cparams(("parallel", "arbitrary")),
        name="flash_attention",
    )(*args)


def _score_tiles(bq, bk, count_fn, window):
    n_diag = bq // bk
    window_blocks = 0 if window is None else (window + bk - 1) // bk
    rel = jnp.arange(-(n_diag - 1), window_blocks + 1)
    d = rel[:, None, None] * bk + jnp.arange(bq)[None, :, None] - jnp.arange(bk)[None, None, :]
    count = jnp.where(d >= 0, count_fn(d), 0.0)
    tiles = jnp.where(count > 0, jnp.log2(jnp.maximum(count, 1.0)), NEG_INF).astype(F32)
    return tiles, (None if window is None else window_blocks)


def _causal_tiles(bq, bk):
    return _score_tiles(bq, bk, lambda d: jnp.ones(d.shape, F32), None)[0]


def _dilated_tiles(bq, bk):
    count_fn = lambda d: sum(((d <= w) & (d % dil == 0)).astype(F32) for w, dil in DILATED_PAIRS)
    return _score_tiles(bq, bk, count_fn, max(w for w, _ in DILATED_PAIRS))


def _stick_kernel(q_ref, k_ref, v_ref, t_ref, o_ref, *, bq, bk, scale, hps):
    qi = pl.program_id(1)
    d = HEAD_DIM
    n_diag = bq // bk
    qs = [q_ref[:, g * d:(g + 1) * d] for g in range(hps)]
    tri = t_ref[...]

    def head_block(g, j, r, acc, diag_t):
        start = pl.multiple_of(j * bk, bk)
        k = k_ref[pl.ds(start, bk), g * d:(g + 1) * d]
        v = v_ref[pl.ds(start, bk), g * d:(g + 1) * d]
        z = lax.dot_general(qs[g], k, (((1,), (1,)), ((), ())), preferred_element_type=F32) * scale
        sp = jnp.maximum(z, 0.0) + jnp.log(1.0 + jnp.exp(-jnp.abs(z)))
        log_1m = -sp
        if diag_t is not None:
            mask = (lax.broadcasted_iota(jnp.int32, (bq, bk), 1) + diag_t * bk
                    < lax.broadcasted_iota(jnp.int32, (bq, bk), 0))
            log_1m = jnp.where(mask, log_1m, 0.0)
        hi = log_1m.astype(BF16)
        lo = (log_1m - hi.astype(F32)).astype(BF16)
        after = (jnp.dot(hi, tri, preferred_element_type=F32) + jnp.dot(lo, tri, preferred_element_type=F32)) + r
        a = jnp.exp((z - sp) + after)
        if diag_t is not None:
            a = jnp.where(mask, a, 0.0)
        acc = acc + jnp.dot(a.astype(BF16), v, preferred_element_type=F32)
        r = r + jnp.sum(log_1m, axis=-1, keepdims=True)
        return r, acc

    def block(j, state, diag_t):
        return tuple(head_block(g, j, *state[g], diag_t) for g in range(hps))

    state = tuple((jnp.zeros((bq, 1), F32), jnp.zeros((bq, d), F32)) for _ in range(hps))
    j_diag = qi * n_diag
    for t in reversed(range(n_diag)):
        state = block(j_diag + t, state, t)

    def cond(c):
        j, state = c
        r_max = functools.reduce(jnp.maximum, [jnp.max(r) for r, _ in state])
        return jnp.logical_and(j >= 0, r_max > -F32_EXP_UNDERFLOW)

    def body(c):
        j, state = c
        return j - 1, block(j, state, None)

    _, state = lax.while_loop(cond, body, (j_diag - 1, state))
    for g, (_, acc) in enumerate(state):
        o_ref[:, g * d:(g + 1) * d] = acc.astype(o_ref.dtype)


def stick_breaking_attention(qkv, q_cb, k_cb, v_cb, *, scale, bq=512, bk=256, hps=2):
    s_len = qkv.shape[0]
    tri = jnp.tril(jnp.ones((bk, bk), F32), -1).astype(BF16)
    w = hps * HEAD_DIM
    assert q_cb % hps == 0 and k_cb % hps == 0 and v_cb % hps == 0
    return pl.pallas_call(
        functools.partial(_stick_kernel, bq=bq, bk=bk, scale=scale, hps=hps),
        grid=(N_HEADS // hps, s_len // bq),
        in_specs=[pl.BlockSpec((bq, w), lambda h, i: (i, q_cb // hps + h)),
                  pl.BlockSpec((s_len, w), lambda h, i: (0, k_cb // hps + h)),
                  pl.BlockSpec((s_len, w), lambda h, i: (0, v_cb // hps + h)),
                  pl.BlockSpec((bk, bk), lambda h, i: (0, 0))],
        out_specs=pl.BlockSpec((bq, w), lambda h, i: (i, h)),
        out_shape=jax.ShapeDtypeStruct((s_len, GROUP_WIDTH), F32),
        compiler_params=_cparams(("parallel", "arbitrary")),
        name="stick_breaking",
    )(qkv, qkv, qkv, tri)


def _prep_mla_weights(w_uq, w_uk, w_uv):
    wq = w_uq.reshape(Q_LORA_RANK, N_HEADS, HEAD_DIM + MLA_ROPE_DIM)
    wq_nope = wq[:, :, :HEAD_DIM].reshape(Q_LORA_RANK, GROUP_WIDTH)
    wq_rope = jnp.pad(wq[:, :, HEAD_DIM:], ((0, 0), (0, 0), (0, HEAD_DIM - MLA_ROPE_DIM))).reshape(Q_LORA_RANK, GROUP_WIDTH)
    w_q = jnp.concatenate([wq_nope, wq_rope], axis=1).astype(BF16)
    w_kv = jnp.concatenate([w_uk, w_uv], axis=1).astype(BF16)
    return w_q, w_kv


def kernel(x, g_attn, w_in, g_q, g_kv, w_uq, w_uk, w_uv, b_f, g_out, w_o, g_mlp, w_up, w_down, g_final):
    b, s_len, d_model = x.shape
    assert b == 1 and s_len % 2048 == 0
    depth = w_in.shape[0]
    xf = x.reshape(s_len, d_model)

    mla_tables = _rope_tables(s_len, MLA_ROPE_DIM, LANES, pass_through=False)
    dil_tables = _rope_tables(s_len, PARTIAL_ROPE_DIM, HEAD_DIM, pass_through=True)
    causal = _causal_tiles(FLASH_BQ, FLASH_BK)
    dil_tiles, dil_blocks = _dilated_tiles(DILATED_BQ, DILATED_BK)
    scale = HEAD_DIM ** -0.5
    mla_scale = (HEAD_DIM + MLA_ROPE_DIM) ** -0.5
    nh = N_HEADS
    n_a = Q_LORA_RANK + KV_LORA_RANK + MLA_ROPE_DIM
    n_a128 = -(-n_a // LANES) * LANES
    gate_col0 = w_in.shape[2] - LANES
    w_in_t = jnp.swapaxes(w_in, 1, 2)

    w_down_bf = w_down.astype(BF16)

    for l in range(depth):
        w_q, w_kv = _prep_mla_weights(w_uq[l], w_uk[l], w_uv[l])

        h = rmsnorm(xf, g_attn[l], BF16)
        seg_a = matmul_f32w(h, w_in_t, l, 0, n_a128, F32, transposed=True)
        bqk_r = matmul_f32w(h, w_in_t, l, n_a, 2 * GROUP_WIDTH, BF16, transposed=True,
                            rope=(dil_tables, PARTIAL_ROPE_DIM // 2))
        rest = matmul_f32w(h, w_in_t, l, n_a + 2 * GROUP_WIDTH, 7 * GROUP_WIDTH, BF16, transposed=True)
        gate = matmul_f32w(h, w_in_t, l, gate_col0, LANES, F32, transposed=True, tn=LANES)

        cq_n, ckv_n = mla_norms(seg_a, g_q[l], g_kv[l])
        qa = matmul(cq_n, w_q, F32)
        kv = matmul(ckv_n, w_kv, BF16)
        q_cat, k_cat = mla_cat(qa, kv, seg_a, mla_tables)
        o_a = flash_attention(q_cat, 0, k_cat, 0, kv, nh, dqk=2 * HEAD_DIM, scale=mla_scale, tiles=causal, hps=2)

        o_b = flash_attention(bqk_r, 0, bqk_r, nh, rest, 0, dqk=HEAD_DIM, scale=scale, tiles=dil_tiles,
                              window_blocks=dil_blocks, hps=2)

        o_c = stick_breaking_attention(rest, nh, 2 * nh, 3 * nh, scale=scale)

        c_f = forget_cumsum(gate, b_f[l], out_scale=LOG2E)
        o_d = flash_attention(rest, 4 * nh, rest, 5 * nh, rest, 6 * nh, dqk=HEAD_DIM, scale=scale, tiles=causal,
                              decay=c_f, hps=2)

        mix = group_norm_concat(o_a, o_b, o_c, o_d, g_out[l])
        xf = matmul_f32w(mix, w_o, l, 0, d_model, F32, res=xf)

        h2 = rmsnorm(xf, g_mlp[l], BF16)
        u = matmul_f32w(h2, w_up, l, 0, w_up.shape[2], BF16, act="relu2")
        xf = matmul(u, w_down_bf, F32, res=xf, layer=l)

    return rmsnorm(xf, g_final, F32).reshape(b, s_len, d_model)
```

```python
import functools
import math

import jax
import jax.numpy as jnp
from jax import lax
from jax.experimental import pallas as pl
from jax.experimental.pallas import tpu as pltpu

F32 = jnp.float32
BF16 = jnp.bfloat16

HEAD_DIM = 128
N_HEADS = 8
GROUP_WIDTH = N_HEADS * HEAD_DIM
Q_LORA_RANK = 896
KV_LORA_RANK = 512
MLA_ROPE_DIM = 64
ROPE_THETA = 500000.0
PARTIAL_ROPE_DIM = HEAD_DIM // 4
DILATED_PAIRS = ((128, 1), (512, 4), (2048, 16))
EPS = 1e-6
NEG_INF = -1e30

LANES = 128
VMEM_LIMIT_BYTES = 56 * 1024 * 1024

F32_EXP_UNDERFLOW = 104.0
LOG2E = math.log2(math.e)

FLASH_BQ, FLASH_BK = 1024, 1024
DILATED_BQ, DILATED_BK = 512, 512


def _cparams(semantics):
    return pltpu.CompilerParams(dimension_semantics=semantics, vmem_limit_bytes=VMEM_LIMIT_BYTES)


def _rms(x, g):
    return x * lax.rsqrt(jnp.mean(x * x, axis=-1, keepdims=True) + EPS) * g


def _rmsnorm_kernel(x_ref, g_ref, o_ref):
    o_ref[...] = _rms(x_ref[...], g_ref[...]).astype(o_ref.dtype)


def rmsnorm(x, g, out_dtype, tm=512):
    m, d = x.shape
    return pl.pallas_call(
        _rmsnorm_kernel,
        grid=(m // tm,),
        in_specs=[pl.BlockSpec((tm, d), lambda i: (i, 0)), pl.BlockSpec((1, d), lambda i: (0, 0))],
        out_specs=pl.BlockSpec((tm, d), lambda i: (i, 0)),
        out_shape=jax.ShapeDtypeStruct((m, d), out_dtype),
        compiler_params=_cparams(("parallel",)),
        name="rmsnorm",
    )(x, g.reshape(1, d))


def _mla_norms_kernel(x_ref, gq_ref, gkv_ref, oq_ref, okv_ref):
    oq_ref[...] = _rms(x_ref[:, :Q_LORA_RANK], gq_ref[...]).astype(oq_ref.dtype)
    okv_ref[...] = _rms(x_ref[:, Q_LORA_RANK:Q_LORA_RANK + KV_LORA_RANK], gkv_ref[...]).astype(okv_ref.dtype)


def mla_norms(seg_a, g_q, g_kv, tm=512):
    m, w = seg_a.shape
    return pl.pallas_call(
        _mla_norms_kernel,
        grid=(m // tm,),
        in_specs=[pl.BlockSpec((tm, w), lambda i: (i, 0)),
                  pl.BlockSpec((1, Q_LORA_RANK), lambda i: (0, 0)),
                  pl.BlockSpec((1, KV_LORA_RANK), lambda i: (0, 0))],
        out_specs=[pl.BlockSpec((tm, Q_LORA_RANK), lambda i: (i, 0)),
                   pl.BlockSpec((tm, KV_LORA_RANK), lambda i: (i, 0))],
        out_shape=[jax.ShapeDtypeStruct((m, Q_LORA_RANK), BF16),
                   jax.ShapeDtypeStruct((m, KV_LORA_RANK), BF16)],
        compiler_params=_cparams(("parallel",)),
        name="mla_norms",
    )(seg_a, g_q.reshape(1, -1), g_kv.reshape(1, -1))


def _group_norm_kernel(a_ref, b_ref, c_ref, d_ref, g_ref, o_ref):
    for gi, r in enumerate((a_ref, b_ref, c_ref, d_ref)):
        sl = slice(gi * GROUP_WIDTH, (gi + 1) * GROUP_WIDTH)
        o_ref[:, sl] = _rms(r[...], g_ref[:, sl]).astype(o_ref.dtype)


def group_norm_concat(o_a, o_b, o_c, o_d, g, tm=256):
    m = o_a.shape[0]
    spec = pl.BlockSpec((tm, GROUP_WIDTH), lambda i: (i, 0))
    return pl.pallas_call(
        _group_norm_kernel,
        grid=(m // tm,),
        in_specs=[spec, spec, spec, spec, pl.BlockSpec((1, 4 * GROUP_WIDTH), lambda i: (0, 0))],
        out_specs=pl.BlockSpec((tm, 4 * GROUP_WIDTH), lambda i: (i, 0)),
        out_shape=jax.ShapeDtypeStruct((m, 4 * GROUP_WIDTH), BF16),
        compiler_params=_cparams(("parallel",)),
        name="group_norm",
    )(o_a, o_b, o_c, o_d, g.reshape(1, -1))


def _mm_kernel(*refs, nk, act, has_res):
    a_ref, w_ref = refs[0], refs[1]
    res_ref = refs[2] if has_res else None
    o_ref = refs[2 + has_res]

    def finish(acc):
        if act == "relu2":
            acc = jnp.square(jnp.maximum(acc, 0.0))
        if has_res:
            acc = res_ref[...] + acc
        o_ref[...] = acc.astype(o_ref.dtype)

    if nk == 1:
        finish(jnp.dot(a_ref[...], w_ref[...], preferred_element_type=F32))
    elif act is None and o_ref.dtype == F32:
        k = pl.program_id(2)

        @pl.when(k == 0)
        def _():
            o_ref[...] = res_ref[...] if has_res else jnp.zeros_like(o_ref)

        o_ref[...] += jnp.dot(a_ref[...], w_ref[...], preferred_element_type=F32)
    else:
        acc_ref = refs[3 + has_res]
        k = pl.program_id(2)

        @pl.when(k == 0)
        def _():
            acc_ref[...] = jnp.zeros_like(acc_ref)

        acc_ref[...] += jnp.dot(a_ref[...], w_ref[...], preferred_element_type=F32)

        @pl.when(k == nk - 1)
        def _():
            finish(acc_ref[...])


def _pick(n, prefs):
    for p in prefs:
        if n % p == 0:
            return p
    return n


def matmul(a, w, out_dtype, act=None, res=None, layer=None, tm=None, tn=None, tk=None, name="matmul"):
    m, kdim = a.shape
    n = w.shape[-1]
    tm = tm or _pick(m, (1024, 512, 256))
    tn = tn or _pick(n, (1024, 768, 512, 256, 128))
    has_res = res is not None
    tk = tk or _pick(kdim, (4096, 2048, 1024))
    nk = kdim // tk
    w_spec = (pl.BlockSpec((tk, tn), lambda i, j, k: (k, j)) if layer is None else
              pl.BlockSpec((None, tk, tn), lambda i, j, k: (layer, k, j)))
    in_specs = [pl.BlockSpec((tm, tk), lambda i, j, k: (i, k)), w_spec]
    args = [a, w]
    if has_res:
        in_specs.append(pl.BlockSpec((tm, tn), lambda i, j, k: (i, j)))
        args.append(res)
    scratch = [pltpu.VMEM((tm, tn), F32)] if (nk > 1 and not (act is None and out_dtype == F32)) else []
    return pl.pallas_call(
        functools.partial(_mm_kernel, nk=nk, act=act, has_res=has_res),
        grid=(m // tm, n // tn, nk),
        in_specs=in_specs,
        out_specs=pl.BlockSpec((tm, tn), lambda i, j, k: (i, j)),
        out_shape=jax.ShapeDtypeStruct((m, n), out_dtype),
        scratch_shapes=scratch,
        compiler_params=_cparams(("parallel", "parallel", "arbitrary")),
        name=name,
    )(*args)


def _mmw_kernel(*refs, transposed, act, has_res, rope_half, has_side, row_chunk):
    a_ref, w_ref = refs[0], refs[1]
    idx = 2
    res_ref = tab_refs = None
    if has_res:
        res_ref = refs[idx]
        idx += 1
    if rope_half:
        tab_refs = refs[idx:idx + 3]
        idx += 3
    if has_side:
        refs[idx + 2][...] = refs[idx][...].astype(BF16)
        idx += 1
    o_ref = refs[idx]
    wbf_ref = refs[idx + 1 + has_side]

    @pl.when(pl.program_id(1) == 0)
    def _():
        def cast_rows(r, c):
            rows = pl.ds(pl.multiple_of(r * row_chunk, row_chunk), row_chunk)
            wbf_ref[rows, :] = w_ref[0, rows, :].astype(BF16)
            return c
        lax.fori_loop(0, w_ref.shape[1] // row_chunk, cast_rows, 0)

    dims = (((1,), (1,)), ((), ())) if transposed else (((1,), (0,)), ((), ()))
    acc = lax.dot_general(a_ref[...], wbf_ref[...], dims, preferred_element_type=F32)
    if act == "relu2":
        acc = jnp.square(jnp.maximum(acc, 0.0))
    if rope_half:
        acc = _rope(acc, *(t[...] for t in tab_refs), rope_half)
    if has_res:
        acc = res_ref[...] + acc
    o_ref[...] = acc.astype(o_ref.dtype)


def matmul_f32w(a, w_stack, layer, col0, n, out_dtype, act=None, res=None, rope=None, transposed=False,
                side_cast=None, tm=1024, tn=512, name="matmul_f32w"):
    m, kdim = a.shape
    has_res = res is not None
    if transposed:
        w_block, w_shape = (pl.Element(1), pl.Element(tn), pl.Element(kdim)), (tn, kdim)
        w_index = lambda j, i: (layer, (col0 // 8 + j * (tn // 8)) * 8, 0)
    else:
        assert col0 % LANES == 0
        w_block, w_shape = (pl.Element(1), pl.Element(kdim), pl.Element(tn)), (kdim, tn)
        w_index = lambda j, i: (layer, 0, (col0 // LANES + j * (tn // LANES)) * LANES)
    in_specs = [pl.BlockSpec((tm, kdim), lambda j, i: (i, 0)), pl.BlockSpec(w_block, w_index)]
    args = [a, w_stack]
    if has_res:
        in_specs.append(pl.BlockSpec((tm, tn), lambda j, i: (i, j)))
        args.append(res)
    rope_half = 0
    if rope is not None:
        tables, rope_half = rope
        in_specs += [pl.BlockSpec((tm, tables[0].shape[1]), lambda j, i: (i, 0))] * 3
        args += list(tables)
    out_specs = pl.BlockSpec((tm, tn), lambda j, i: (i, j))
    out_shape = jax.ShapeDtypeStruct((m, n), out_dtype)
    ni = m // tm
    if side_cast is not None:
        side_stack, side_layer = side_cast
        _, side_r, side_c = side_stack.shape
        slab = side_r // ((n // tn) * ni)
        assert slab * (n // tn) * ni == side_r and slab % 16 == 0
        in_specs.append(pl.BlockSpec((None, slab, side_c), lambda j, i: (side_layer, j * ni + i, 0)))
        args.append(side_stack)
        out_specs = [out_specs, pl.BlockSpec((slab, side_c), lambda j, i: (j * ni + i, 0))]
        out_shape = [out_shape, jax.ShapeDtypeStruct((side_r, side_c), BF16)]
    return pl.pallas_call(
        functools.partial(_mmw_kernel, transposed=transposed, act=act, has_res=has_res, rope_half=rope_half,
                          has_side=side_cast is not None, row_chunk=min(256, w_shape[0])),
        grid=(n // tn, ni),
        in_specs=in_specs,
        out_specs=out_specs,
        out_shape=out_shape,
        scratch_shapes=[pltpu.VMEM(w_shape, BF16)],
        compiler_params=_cparams(("parallel", "arbitrary")),
        name=name,
    )(*args)


def _rope_tables(s_len, r, period, pass_through):
    half = r // 2
    inv = ROPE_THETA ** (-jnp.arange(half, dtype=F32) * (2.0 / r))
    ang = jnp.arange(s_len, dtype=F32)[:, None] * inv[None, :]
    cos, sin = jnp.cos(ang), jnp.sin(ang)
    fill = jnp.ones if pass_through else jnp.zeros
    zeros = jnp.zeros((s_len, period - r), F32)
    zh = jnp.zeros((s_len, half), F32)
    c = jnp.concatenate([cos, cos, fill((s_len, period - r), F32)], axis=1)
    sa = jnp.concatenate([-sin, zh, zeros], axis=1)
    sb = jnp.concatenate([zh, sin, zeros], axis=1)
    return c, sa, sb


def _rope(x, c, sa, sb, half):
    w = x.shape[-1]
    reps = w // c.shape[-1]
    if reps > 1:
        c, sa, sb = (jnp.tile(t, (1, reps)) for t in (c, sa, sb))
    return x * c + pltpu.roll(x, w - half, 1) * sa + pltpu.roll(x, half, 1) * sb


def _mla_cat_kernel(qa_ref, kn_ref, kr_ref, c_ref, sa_ref, sb_ref, q_ref, k_ref):
    half = MLA_ROPE_DIM // 2
    c, sa, sb = c_ref[...], sa_ref[...], sb_ref[...]
    qr = _rope(qa_ref[:, GROUP_WIDTH:], c, sa, sb, half).astype(BF16)
    kr = _rope(kr_ref[...], c, sa, sb, half).astype(BF16)
    for h in range(N_HEADS):
        lo = h * 2 * HEAD_DIM
        q_ref[:, lo:lo + HEAD_DIM] = qa_ref[:, h * HEAD_DIM:(h + 1) * HEAD_DIM].astype(BF16)
        q_ref[:, lo + HEAD_DIM:lo + 2 * HEAD_DIM] = qr[:, h * HEAD_DIM:(h + 1) * HEAD_DIM]
        k_ref[:, lo:lo + HEAD_DIM] = kn_ref[:, h * HEAD_DIM:(h + 1) * HEAD_DIM]
        k_ref[:, lo + HEAD_DIM:lo + 2 * HEAD_DIM] = kr


def mla_cat(qa, kv, seg_a, tables, tm=512):
    m = qa.shape[0]
    kr_block = (Q_LORA_RANK + KV_LORA_RANK) // LANES
    tspec = pl.BlockSpec((tm, LANES), lambda i: (i, 0))
    out_w = N_HEADS * 2 * HEAD_DIM
    return pl.pallas_call(
        _mla_cat_kernel,
        grid=(m // tm,),
        in_specs=[pl.BlockSpec((tm, 2 * GROUP_WIDTH), lambda i: (i, 0)),
                  pl.BlockSpec((tm, GROUP_WIDTH), lambda i: (i, 0)),
                  pl.BlockSpec((tm, LANES), lambda i: (i, kr_block)),
                  tspec, tspec, tspec],
        out_specs=[pl.BlockSpec((tm, out_w), lambda i: (i, 0)), pl.BlockSpec((tm, out_w), lambda i: (i, 0))],
        out_shape=[jax.ShapeDtypeStruct((m, out_w), BF16), jax.ShapeDtypeStruct((m, out_w), BF16)],
        compiler_params=_cparams(("parallel",)),
        name="mla_cat",
    )(qa, kv, seg_a, *tables)


def _split3(x):
    hi = x.astype(BF16)
    r1 = x - hi.astype(F32)
    mid = r1.astype(BF16)
    lo = (r1 - mid.astype(F32)).astype(BF16)
    return hi, mid, lo


def _forget_kernel(x_ref, b_ref, u_ref, o_ref, carry_ref, *, out_scale):
    @pl.when(pl.program_id(0) == 0)
    def _():
        carry_ref[...] = jnp.zeros_like(carry_ref)

    xt = jnp.transpose(x_ref[...])
    y = -(xt[LANES - N_HEADS:, :] + b_ref[...])
    logf = -(jnp.maximum(y, 0.0) + jnp.log(1.0 + jnp.exp(-jnp.abs(y))))
    u = u_ref[...]
    cs = sum(jnp.dot(p, u, preferred_element_type=F32) for p in _split3(logf))
    c = cs + carry_ref[:, 0:1]
    o_ref[...] = c * out_scale
    carry_ref[...] = jnp.broadcast_to(c[:, -1:], carry_ref.shape)


def forget_cumsum(gate, b_f, tc=512, out_scale=1.0):
    m = gate.shape[0]
    u = jnp.triu(jnp.ones((tc, tc), F32)).astype(BF16)
    return pl.pallas_call(
        functools.partial(_forget_kernel, out_scale=out_scale),
        grid=(m // tc,),
        in_specs=[pl.BlockSpec((tc, LANES), lambda i: (i, 0)),
                  pl.BlockSpec((N_HEADS, 1), lambda i: (0, 0)),
                  pl.BlockSpec((tc, tc), lambda i: (0, 0))],
        out_specs=pl.BlockSpec((N_HEADS, tc), lambda i: (0, i)),
        out_shape=jax.ShapeDtypeStruct((N_HEADS, m), F32),
        scratch_shapes=[pltpu.VMEM((N_HEADS, LANES), F32)],
        compiler_params=_cparams(("arbitrary",)),
        name="forget_cumsum",
    )(gate, b_f.reshape(N_HEADS, 1), u)


def _flash_kernel(*refs, bq, bk, dqk, scale2, window_blocks, has_decay, hps):
    q_ref, k_ref, v_ref, tile_ref = refs[:4]
    ck_ref = refs[4] if has_decay else None
    o_ref = refs[4 + has_decay]
    dv = HEAD_DIM
    n_diag = bq // bk
    qi = pl.program_id(1)
    qs = [q_ref[:, g * dqk:(g + 1) * dqk] for g in range(hps)]

    def head_step(g, j, carry, tile):
        m, l, acc = carry
        start = pl.multiple_of(j * bk, bk)
        k = k_ref[pl.ds(start, bk), g * dqk:(g + 1) * dqk]
        v = v_ref[pl.ds(start, bk), g * dv:(g + 1) * dv]
        s = lax.dot_general(qs[g], k, (((1,), (1,)), ((), ())), preferred_element_type=F32) * scale2
        if has_decay:
            s = s - ck_ref[g, pl.ds(j, 1), :]
        if tile is not None:
            s = s + tile
        m_new = jnp.maximum(m, jnp.max(s, axis=-1, keepdims=True))
        alpha = jnp.exp2(m - m_new)
        p = jnp.exp2(s - m_new)
        l = alpha * l + jnp.sum(p, axis=-1, keepdims=True)
        acc = alpha * acc + jnp.dot(p.astype(BF16), v, preferred_element_type=F32)
        return m_new, l, acc

    def step(j, carry, tile):
        return tuple(head_step(g, j, carry[g], tile) for g in range(hps))

    carry = tuple((jnp.full((bq, 1), NEG_INF, F32), jnp.zeros((bq, 1), F32), jnp.zeros((bq, dv), F32))
                  for _ in range(hps))
    j_diag = qi * n_diag
    if window_blocks is None:
        lo = 0
        body = lambda j, c: step(j, c, None)
    else:
        lo = jnp.maximum(j_diag - window_blocks, 0)
        body = lambda j, c: step(j, c, tile_ref[n_diag - 1 + j_diag - j])
    carry = lax.fori_loop(lo, j_diag, body, carry)
    for t in range(n_diag):
        carry = step(j_diag + t, carry, tile_ref[n_diag - 1 - t])
    for g, (m, l, acc) in enumerate(carry):
        o_ref[:, g * dv:(g + 1) * dv] = (acc / l).astype(o_ref.dtype)


def flash_attention(q_arr, q_cb, k_arr, k_cb, v_arr, v_cb, *, dqk, scale, tiles, window_blocks=None,
                    decay=None, hps=1):
    s_len = q_arr.shape[0]
    _, bq, bk = tiles.shape
    nq, nk = s_len // bq, s_len // bk
    assert q_cb % hps == 0 and k_cb % hps == 0 and v_cb % hps == 0
    in_specs = [pl.BlockSpec((bq, hps * dqk), lambda h, i: (i, q_cb // hps + h)),
                pl.BlockSpec((s_len, hps * dqk), lambda h, i: (0, k_cb // hps + h)),
                pl.BlockSpec((s_len, hps * HEAD_DIM), lambda h, i: (0, v_cb // hps + h)),
                pl.BlockSpec(tiles.shape, lambda h, i: (0, 0, 0))]
    args = [q_arr, k_arr, v_arr, tiles]
    if decay is not None:
        in_specs.append(pl.BlockSpec((hps, nk, bk), lambda h, i: (h, 0, 0)))
        args.append(decay.reshape(N_HEADS, nk, bk))
    return pl.pallas_call(
        functools.partial(_flash_kernel, bq=bq, bk=bk, dqk=dqk, scale2=scale * LOG2E,
                          window_blocks=window_blocks, has_decay=decay is not None, hps=hps),
        grid=(N_HEADS // hps, nq),
        in_specs=in_specs,
        out_specs=pl.BlockSpec((bq, hps * HEAD_DIM), lambda h, i: (i, h)),
        out_shape=jax.ShapeDtypeStruct((s_len, GROUP_WIDTH), F32),
        compiler_params=_cparams(("parallel", "arbitrary")),
        name="flash_attention",
    )(*args)


def _score_tiles(bq, bk, count_fn, window):
    n_diag = bq // bk
    window_blocks = 0 if window is None else (window + bk - 1) // bk
    rel = jnp.arange(-(n_diag - 1), window_blocks + 1)
    d = rel[:, None, None] * bk + jnp.arange(bq)[None, :, None] - jnp.arange(bk)[None, None, :]
    count = jnp.where(d >= 0, count_fn(d), 0.0)
    tiles = jnp.where(count > 0, jnp.log2(jnp.maximum(count, 1.0)), NEG_INF).astype(F32)
    return tiles, (None if window is None else window_blocks)


def _causal_tiles(bq, bk):
    return _score_tiles(bq, bk, lambda d: jnp.ones(d.shape, F32), None)[0]


def _dilated_tiles(bq, bk):
    count_fn = lambda d: sum(((d <= w) & (d % dil == 0)).astype(F32) for w, dil in DILATED_PAIRS)
    return _score_tiles(bq, bk, count_fn, max(w for w, _ in DILATED_PAIRS))


def _stick_kernel(q_ref, k_ref, v_ref, t_ref, o_ref, *, bq, bk, scale, hps):
    qi = pl.program_id(1)
    d = HEAD_DIM
    n_diag = bq // bk
    qs = [q_ref[:, g * d:(g + 1) * d] for g in range(hps)]
    tri = t_ref[...]

    def head_block(g, j, r, acc, diag_t):
        start = pl.multiple_of(j * bk, bk)
        k = k_ref[pl.ds(start, bk), g * d:(g + 1) * d]
        v = v_ref[pl.ds(start, bk), g * d:(g + 1) * d]
        z = lax.dot_general(qs[g], k, (((1,), (1,)), ((), ())), preferred_element_type=F32) * scale
        sp = jnp.maximum(z, 0.0) + jnp.log(1.0 + jnp.exp(-jnp.abs(z)))
        log_1m = -sp
        if diag_t is not None:
            mask = (lax.broadcasted_iota(jnp.int32, (bq, bk), 1) + diag_t * bk
                    < lax.broadcasted_iota(jnp.int32, (bq, bk), 0))
            log_1m = jnp.where(mask, log_1m, 0.0)
        hi = log_1m.astype(BF16)
        lo = (log_1m - hi.astype(F32)).astype(BF16)
        after = (jnp.dot(hi, tri, preferred_element_type=F32) + jnp.dot(lo, tri, preferred_element_type=F32)) + r
        a = jnp.exp((z - sp) + after)
        if diag_t is not None:
            a = jnp.where(mask, a, 0.0)
        acc = acc + jnp.dot(a.astype(BF16), v, preferred_element_type=F32)
        r = r + jnp.sum(log_1m, axis=-1, keepdims=True)
        return r, acc

    def block(j, state, diag_t):
        return tuple(head_block(g, j, *state[g], diag_t) for g in range(hps))

    state = tuple((jnp.zeros((bq, 1), F32), jnp.zeros((bq, d), F32)) for _ in range(hps))
    j_diag = qi * n_diag
    for t in reversed(range(n_diag)):
        state = block(j_diag + t, state, t)

    def cond(c):
        j, state = c
        r_max = functools.reduce(jnp.maximum, [jnp.max(r) for r, _ in state])
        return jnp.logical_and(j >= 0, r_max > -F32_EXP_UNDERFLOW)

    def body(c):
        j, state = c
        return j - 1, block(j, state, None)

    _, state = lax.while_loop(cond, body, (j_diag - 1, state))
    for g, (_, acc) in enumerate(state):
        o_ref[:, g * d:(g + 1) * d] = acc.astype(o_ref.dtype)


def stick_breaking_attention(qkv, q_cb, k_cb, v_cb, *, scale, bq=512, bk=256, hps=2):
    s_len = qkv.shape[0]
    tri = jnp.tril(jnp.ones((bk, bk), F32), -1).astype(BF16)
    w = hps * HEAD_DIM
    assert q_cb % hps == 0 and k_cb % hps == 0 and v_cb % hps == 0
    return pl.pallas_call(
        functools.partial(_stick_kernel, bq=bq, bk=bk, scale=scale, hps=hps),
        grid=(N_HEADS // hps, s_len // bq),
        in_specs=[pl.BlockSpec((bq, w), lambda h, i: (i, q_cb // hps + h)),
                  pl.BlockSpec((s_len, w), lambda h, i: (0, k_cb // hps + h)),
                  pl.BlockSpec((s_len, w), lambda h, i: (0, v_cb // hps + h)),
                  pl.BlockSpec((bk, bk), lambda h, i: (0, 0))],
        out_specs=pl.BlockSpec((bq, w), lambda h, i: (i, h)),
        out_shape=jax.ShapeDtypeStruct((s_len, GROUP_WIDTH), F32),
        compiler_params=_cparams(("parallel", "arbitrary")),
        name="stick_breaking",
    )(qkv, qkv, qkv, tri)


def _prep_mla_weights(w_uq, w_uk, w_uv):
    wq = w_uq.reshape(Q_LORA_RANK, N_HEADS, HEAD_DIM + MLA_ROPE_DIM)
    wq_nope = wq[:, :, :HEAD_DIM].reshape(Q_LORA_RANK, GROUP_WIDTH)
    wq_rope = jnp.pad(wq[:, :, HEAD_DIM:], ((0, 0), (0, 0), (0, HEAD_DIM - MLA_ROPE_DIM))).reshape(Q_LORA_RANK, GROUP_WIDTH)
    w_q = jnp.concatenate([wq_nope, wq_rope], axis=1).astype(BF16)
    w_kv = jnp.concatenate([w_uk, w_uv], axis=1).astype(BF16)
    return w_q, w_kv


def kernel(x, g_attn, w_in, g_q, g_kv, w_uq, w_uk, w_uv, b_f, g_out, w_o, g_mlp, w_up, w_down, g_final):
    b, s_len, d_model = x.shape
    assert b == 1 and s_len % 2048 == 0
    depth = w_in.shape[0]
    xf = x.reshape(s_len, d_model)

    mla_tables = _rope_tables(s_len, MLA_ROPE_DIM, LANES, pass_through=False)
    dil_tables = _rope_tables(s_len, PARTIAL_ROPE_DIM, HEAD_DIM, pass_through=True)
    causal = _causal_tiles(FLASH_BQ, FLASH_BK)
    dil_tiles, dil_blocks = _dilated_tiles(DILATED_BQ, DILATED_BK)
    scale = HEAD_DIM ** -0.5
    mla_scale = (HEAD_DIM + MLA_ROPE_DIM) ** -0.5
    nh = N_HEADS
    n_a = Q_LORA_RANK + KV_LORA_RANK + MLA_ROPE_DIM
    n_a128 = -(-n_a // LANES) * LANES
    gate_col0 = w_in.shape[2] - LANES
    w_in_t = jnp.swapaxes(w_in, 1, 2)

    for l in range(depth):
        w_q, w_kv = _prep_mla_weights(w_uq[l], w_uk[l], w_uv[l])

        h = rmsnorm(xf, g_attn[l], BF16)
        seg_a = matmul_f32w(h, w_in_t, l, 0, n_a128, F32, transposed=True)
        bqk_r = matmul_f32w(h, w_in_t, l, n_a, 2 * GROUP_WIDTH, BF16, transposed=True,
                            rope=(dil_tables, PARTIAL_ROPE_DIM // 2))
        rest = matmul_f32w(h, w_in_t, l, n_a + 2 * GROUP_WIDTH, 7 * GROUP_WIDTH, BF16, transposed=True)
        gate = matmul_f32w(h, w_in_t, l, gate_col0, LANES, F32, transposed=True, tn=LANES)

        cq_n, ckv_n = mla_norms(seg_a, g_q[l], g_kv[l])
        qa = matmul(cq_n, w_q, F32)
        kv = matmul(ckv_n, w_kv, BF16)
        q_cat, k_cat = mla_cat(qa, kv, seg_a, mla_tables)
        o_a = flash_attention(q_cat, 0, k_cat, 0, kv, nh, dqk=2 * HEAD_DIM, scale=mla_scale, tiles=causal, hps=2)

        o_b = flash_attention(bqk_r, 0, bqk_r, nh, rest, 0, dqk=HEAD_DIM, scale=scale, tiles=dil_tiles,
                              window_blocks=dil_blocks, hps=2)

        o_c = stick_breaking_attention(rest, nh, 2 * nh, 3 * nh, scale=scale)

        c_f = forget_cumsum(gate, b_f[l], out_scale=LOG2E)
        o_d = flash_attention(rest, 4 * nh, rest, 5 * nh, rest, 6 * nh, dqk=HEAD_DIM, scale=scale, tiles=causal,
                              decay=c_f, hps=2)

        mix = group_norm_concat(o_a, o_b, o_c, o_d, g_out[l])
        xf = matmul_f32w(mix, w_o, l, 0, d_model, F32, res=xf)

        h2 = rmsnorm(xf, g_mlp[l], BF16)
        u, w_down_bf = matmul_f32w(h2, w_up, l, 0, w_up.shape[2], BF16, act="relu2", side_cast=(w_down, l))
        xf = matmul(u, w_down_bf, F32, res=xf)

    return rmsnorm(xf, g_final, F32).reshape(b, s_len, d_model)
```

```python
import functools
import math

import jax
import jax.numpy as jnp
from jax import lax
from jax.experimental import pallas as pl
from jax.experimental.pallas import tpu as pltpu

F32 = jnp.float32
BF16 = jnp.bfloat16

HEAD_DIM = 128
N_HEADS = 8
GROUP_WIDTH = N_HEADS * HEAD_DIM
Q_LORA_RANK = 896
KV_LORA_RANK = 512
MLA_ROPE_DIM = 64
ROPE_THETA = 500000.0
PARTIAL_ROPE_DIM = HEAD_DIM // 4
DILATED_PAIRS = ((128, 1), (512, 4), (2048, 16))
EPS = 1e-6
NEG_INF = -1e30

LANES = 128
VMEM_LIMIT_BYTES = 62 * 1024 * 1024

F32_EXP_UNDERFLOW = 104.0
LOG2E = math.log2(math.e)

FLASH_BQ, FLASH_BK = 1024, 1024
DILATED_BQ, DILATED_BK = 512, 512


def _cparams(semantics):
    return pltpu.CompilerParams(dimension_semantics=semantics, vmem_limit_bytes=VMEM_LIMIT_BYTES)


def _rms(x, g):
    return x * lax.rsqrt(jnp.mean(x * x, axis=-1, keepdims=True) + EPS) * g


def _rmsnorm_kernel(x_ref, g_ref, o_ref):
    o_ref[...] = _rms(x_ref[...], g_ref[...]).astype(o_ref.dtype)


def rmsnorm(x, g, out_dtype, tm=512):
    m, d = x.shape
    return pl.pallas_call(
        _rmsnorm_kernel,
        grid=(m // tm,),
        in_specs=[pl.BlockSpec((tm, d), lambda i: (i, 0)), pl.BlockSpec((1, d), lambda i: (0, 0))],
        out_specs=pl.BlockSpec((tm, d), lambda i: (i, 0)),
        out_shape=jax.ShapeDtypeStruct((m, d), out_dtype),
        compiler_params=_cparams(("parallel",)),
        name="rmsnorm",
    )(x, g.reshape(1, d))


def _mla_norms_kernel(x_ref, gq_ref, gkv_ref, oq_ref, okv_ref):
    oq_ref[...] = _rms(x_ref[:, :Q_LORA_RANK], gq_ref[...]).astype(oq_ref.dtype)
    okv_ref[...] = _rms(x_ref[:, Q_LORA_RANK:Q_LORA_RANK + KV_LORA_RANK], gkv_ref[...]).astype(okv_ref.dtype)


def mla_norms(seg_a, g_q, g_kv, tm=512):
    m, w = seg_a.shape
    return pl.pallas_call(
        _mla_norms_kernel,
        grid=(m // tm,),
        in_specs=[pl.BlockSpec((tm, w), lambda i: (i, 0)),
                  pl.BlockSpec((1, Q_LORA_RANK), lambda i: (0, 0)),
                  pl.BlockSpec((1, KV_LORA_RANK), lambda i: (0, 0))],
        out_specs=[pl.BlockSpec((tm, Q_LORA_RANK), lambda i: (i, 0)),
                   pl.BlockSpec((tm, KV_LORA_RANK), lambda i: (i, 0))],
        out_shape=[jax.ShapeDtypeStruct((m, Q_LORA_RANK), BF16),
                   jax.ShapeDtypeStruct((m, KV_LORA_RANK), BF16)],
        compiler_params=_cparams(("parallel",)),
        name="mla_norms",
    )(seg_a, g_q.reshape(1, -1), g_kv.reshape(1, -1))


def _group_norm_kernel(a_ref, b_ref, c_ref, d_ref, g_ref, o_ref):
    for gi, r in enumerate((a_ref, b_ref, c_ref, d_ref)):
        sl = slice(gi * GROUP_WIDTH, (gi + 1) * GROUP_WIDTH)
        o_ref[:, sl] = _rms(r[...], g_ref[:, sl]).astype(o_ref.dtype)


def group_norm_concat(o_a, o_b, o_c, o_d, g, tm=256):
    m = o_a.shape[0]
    spec = pl.BlockSpec((tm, GROUP_WIDTH), lambda i: (i, 0))
    return pl.pallas_call(
        _group_norm_kernel,
        grid=(m // tm,),
        in_specs=[spec, spec, spec, spec, pl.BlockSpec((1, 4 * GROUP_WIDTH), lambda i: (0, 0))],
        out_specs=pl.BlockSpec((tm, 4 * GROUP_WIDTH), lambda i: (i, 0)),
        out_shape=jax.ShapeDtypeStruct((m, 4 * GROUP_WIDTH), BF16),
        compiler_params=_cparams(("parallel",)),
        name="group_norm",
    )(o_a, o_b, o_c, o_d, g.reshape(1, -1))


def _mm_kernel(*refs, nk, act, has_res):
    a_ref, w_ref = refs[0], refs[1]
    res_ref = refs[2] if has_res else None
    o_ref = refs[2 + has_res]

    def finish(acc):
        if act == "relu2":
            acc = jnp.square(jnp.maximum(acc, 0.0))
        if has_res:
            acc = res_ref[...] + acc
        o_ref[...] = acc.astype(o_ref.dtype)

    if nk == 1:
        finish(jnp.dot(a_ref[...], w_ref[...], preferred_element_type=F32))
    elif act is None and o_ref.dtype == F32:
        k = pl.program_id(2)

        @pl.when(k == 0)
        def _():
            o_ref[...] = res_ref[...] if has_res else jnp.zeros_like(o_ref)

        o_ref[...] += jnp.dot(a_ref[...], w_ref[...], preferred_element_type=F32)
    else:
        acc_ref = refs[3 + has_res]
        k = pl.program_id(2)

        @pl.when(k == 0)
        def _():
            acc_ref[...] = jnp.zeros_like(acc_ref)

        acc_ref[...] += jnp.dot(a_ref[...], w_ref[...], preferred_element_type=F32)

        @pl.when(k == nk - 1)
        def _():
            finish(acc_ref[...])


def _pick(n, prefs):
    for p in prefs:
        if n % p == 0:
            return p
    return n


def matmul(a, w, out_dtype, act=None, res=None, layer=None, tm=None, tn=None, tk=None, name="matmul"):
    m, kdim = a.shape
    n = w.shape[-1]
    tm = tm or _pick(m, (1024, 512, 256))
    tn = tn or _pick(n, (1024, 768, 512, 256, 128))
    has_res = res is not None
    tk = tk or _pick(kdim, (4096, 2048, 1024))
    nk = kdim // tk
    w_spec = (pl.BlockSpec((tk, tn), lambda i, j, k: (k, j)) if layer is None else
              pl.BlockSpec((None, tk, tn), lambda i, j, k: (layer, k, j)))
    in_specs = [pl.BlockSpec((tm, tk), lambda i, j, k: (i, k)), w_spec]
    args = [a, w]
    if has_res:
        in_specs.append(pl.BlockSpec((tm, tn), lambda i, j, k: (i, j)))
        args.append(res)
    scratch = [pltpu.VMEM((tm, tn), F32)] if (nk > 1 and not (act is None and out_dtype == F32)) else []
    return pl.pallas_call(
        functools.partial(_mm_kernel, nk=nk, act=act, has_res=has_res),
        grid=(m // tm, n // tn, nk),
        in_specs=in_specs,
        out_specs=pl.BlockSpec((tm, tn), lambda i, j, k: (i, j)),
        out_shape=jax.ShapeDtypeStruct((m, n), out_dtype),
        scratch_shapes=scratch,
        compiler_params=_cparams(("parallel", "parallel", "arbitrary")),
        name=name,
    )(*args)


def _mmw_kernel(*refs, transposed, act, has_res, rope_half, has_side, row_chunk):
    a_ref, w_ref = refs[0], refs[1]
    idx = 2
    res_ref = tab_refs = None
    if has_res:
        res_ref = refs[idx]
        idx += 1
    if rope_half:
        tab_refs = refs[idx:idx + 3]
        idx += 3
    if has_side:
        refs[idx + 2][...] = refs[idx][...].astype(BF16)
        idx += 1
    o_ref = refs[idx]
    wbf_ref = refs[idx + 1 + has_side]

    @pl.when(pl.program_id(1) == 0)
    def _():
        def cast_rows(r, c):
            rows = pl.ds(pl.multiple_of(r * row_chunk, row_chunk), row_chunk)
            wbf_ref[rows, :] = w_ref[0, rows, :].astype(BF16)
            return c
        lax.fori_loop(0, w_ref.shape[1] // row_chunk, cast_rows, 0)

    dims = (((1,), (1,)), ((), ())) if transposed else (((1,), (0,)), ((), ()))
    acc = lax.dot_general(a_ref[...], wbf_ref[...], dims, preferred_element_type=F32)
    if act == "relu2":
        acc = jnp.square(jnp.maximum(acc, 0.0))
    if rope_half:
        acc = _rope(acc, *(t[...] for t in tab_refs), rope_half)
    if has_res:
        acc = res_ref[...] + acc
    o_ref[...] = acc.astype(o_ref.dtype)


def matmul_f32w(a, w_stack, layer, col0, n, out_dtype, act=None, res=None, rope=None, transposed=False,
                side_cast=None, tm=512, tn=1024, name="matmul_f32w"):
    m, kdim = a.shape
    has_res = res is not None
    if transposed:
        w_block, w_shape = (pl.Element(1), pl.Element(tn), pl.Element(kdim)), (tn, kdim)
        w_index = lambda j, i: (layer, (col0 // 8 + j * (tn // 8)) * 8, 0)
    else:
        assert col0 % LANES == 0
        w_block, w_shape = (pl.Element(1), pl.Element(kdim), pl.Element(tn)), (kdim, tn)
        w_index = lambda j, i: (layer, 0, (col0 // LANES + j * (tn // LANES)) * LANES)
    in_specs = [pl.BlockSpec((tm, kdim), lambda j, i: (i, 0)), pl.BlockSpec(w_block, w_index)]
    args = [a, w_stack]
    if has_res:
        in_specs.append(pl.BlockSpec((tm, tn), lambda j, i: (i, j)))
        args.append(res)
    rope_half = 0
    if rope is not None:
        tables, rope_half = rope
        in_specs += [pl.BlockSpec((tm, tables[0].shape[1]), lambda j, i: (i, 0))] * 3
        args += list(tables)
    out_specs = pl.BlockSpec((tm, tn), lambda j, i: (i, j))
    out_shape = jax.ShapeDtypeStruct((m, n), out_dtype)
    ni = m // tm
    if side_cast is not None:
        side_stack, side_layer = side_cast
        _, side_r, side_c = side_stack.shape
        slab = side_r // ((n // tn) * ni)
        assert slab * (n // tn) * ni == side_r and slab % 16 == 0
        in_specs.append(pl.BlockSpec((None, slab, side_c), lambda j, i: (side_layer, j * ni + i, 0)))
        args.append(side_stack)
        out_specs = [out_specs, pl.BlockSpec((slab, side_c), lambda j, i: (j * ni + i, 0))]
        out_shape = [out_shape, jax.ShapeDtypeStruct((side_r, side_c), BF16)]
    return pl.pallas_call(
        functools.partial(_mmw_kernel, transposed=transposed, act=act, has_res=has_res, rope_half=rope_half,
                          has_side=side_cast is not None, row_chunk=min(256, w_shape[0])),
        grid=(n // tn, ni),
        in_specs=in_specs,
        out_specs=out_specs,
        out_shape=out_shape,
        scratch_shapes=[pltpu.VMEM(w_shape, BF16)],
        compiler_params=_cparams(("parallel", "arbitrary")),
        name=name,
    )(*args)


def _rope_tables(s_len, r, period, pass_through):
    half = r // 2
    inv = ROPE_THETA ** (-jnp.arange(half, dtype=F32) * (2.0 / r))
    ang = jnp.arange(s_len, dtype=F32)[:, None] * inv[None, :]
    cos, sin = jnp.cos(ang), jnp.sin(ang)
    fill = jnp.ones if pass_through else jnp.zeros
    zeros = jnp.zeros((s_len, period - r), F32)
    zh = jnp.zeros((s_len, half), F32)
    c = jnp.concatenate([cos, cos, fill((s_len, period - r), F32)], axis=1)
    sa = jnp.concatenate([-sin, zh, zeros], axis=1)
    sb = jnp.concatenate([zh, sin, zeros], axis=1)
    return c, sa, sb


def _rope(x, c, sa, sb, half):
    w = x.shape[-1]
    reps = w // c.shape[-1]
    if reps > 1:
        c, sa, sb = (jnp.tile(t, (1, reps)) for t in (c, sa, sb))
    return x * c + pltpu.roll(x, w - half, 1) * sa + pltpu.roll(x, half, 1) * sb


def _mla_cat_kernel(qa_ref, kn_ref, kr_ref, c_ref, sa_ref, sb_ref, q_ref, k_ref):
    half = MLA_ROPE_DIM // 2
    c, sa, sb = c_ref[...], sa_ref[...], sb_ref[...]
    qr = _rope(qa_ref[:, GROUP_WIDTH:], c, sa, sb, half).astype(BF16)
    kr = _rope(kr_ref[...], c, sa, sb, half).astype(BF16)
    for h in range(N_HEADS):
        lo = h * 2 * HEAD_DIM
        q_ref[:, lo:lo + HEAD_DIM] = qa_ref[:, h * HEAD_DIM:(h + 1) * HEAD_DIM].astype(BF16)
        q_ref[:, lo + HEAD_DIM:lo + 2 * HEAD_DIM] = qr[:, h * HEAD_DIM:(h + 1) * HEAD_DIM]
        k_ref[:, lo:lo + HEAD_DIM] = kn_ref[:, h * HEAD_DIM:(h + 1) * HEAD_DIM]
        k_ref[:, lo + HEAD_DIM:lo + 2 * HEAD_DIM] = kr


def mla_cat(qa, kv, seg_a, tables, tm=512):
    m = qa.shape[0]
    kr_block = (Q_LORA_RANK + KV_LORA_RANK) // LANES
    tspec = pl.BlockSpec((tm, LANES), lambda i: (i, 0))
    out_w = N_HEADS * 2 * HEAD_DIM
    return pl.pallas_call(
        _mla_cat_kernel,
        grid=(m // tm,),
        in_specs=[pl.BlockSpec((tm, 2 * GROUP_WIDTH), lambda i: (i, 0)),
                  pl.BlockSpec((tm, GROUP_WIDTH), lambda i: (i, 0)),
                  pl.BlockSpec((tm, LANES), lambda i: (i, kr_block)),
                  tspec, tspec, tspec],
        out_specs=[pl.BlockSpec((tm, out_w), lambda i: (i, 0)), pl.BlockSpec((tm, out_w), lambda i: (i, 0))],
        out_shape=[jax.ShapeDtypeStruct((m, out_w), BF16), jax.ShapeDtypeStruct((m, out_w), BF16)],
        compiler_params=_cparams(("parallel",)),
        name="mla_cat",
    )(qa, kv, seg_a, *tables)


def _split3(x):
    hi = x.astype(BF16)
    r1 = x - hi.astype(F32)
    mid = r1.astype(BF16)
    lo = (r1 - mid.astype(F32)).astype(BF16)
    return hi, mid, lo


def _forget_kernel(x_ref, b_ref, u_ref, o_ref, carry_ref, *, out_scale):
    @pl.when(pl.program_id(0) == 0)
    def _():
        carry_ref[...] = jnp.zeros_like(carry_ref)

    xt = jnp.transpose(x_ref[...])
    y = -(xt[LANES - N_HEADS:, :] + b_ref[...])
    logf = -(jnp.maximum(y, 0.0) + jnp.log(1.0 + jnp.exp(-jnp.abs(y))))
    u = u_ref[...]
    cs = sum(jnp.dot(p, u, preferred_element_type=F32) for p in _split3(logf))
    c = cs + carry_ref[:, 0:1]
    o_ref[...] = c * out_scale
    carry_ref[...] = jnp.broadcast_to(c[:, -1:], carry_ref.shape)


def forget_cumsum(gate, b_f, tc=512, out_scale=1.0):
    m = gate.shape[0]
    u = jnp.triu(jnp.ones((tc, tc), F32)).astype(BF16)
    return pl.pallas_call(
        functools.partial(_forget_kernel, out_scale=out_scale),
        grid=(m // tc,),
        in_specs=[pl.BlockSpec((tc, LANES), lambda i: (i, 0)),
                  pl.BlockSpec((N_HEADS, 1), lambda i: (0, 0)),
                  pl.BlockSpec((tc, tc), lambda i: (0, 0))],
        out_specs=pl.BlockSpec((N_HEADS, tc), lambda i: (0, i)),
        out_shape=jax.ShapeDtypeStruct((N_HEADS, m), F32),
        scratch_shapes=[pltpu.VMEM((N_HEADS, LANES), F32)],
        compiler_params=_cparams(("arbitrary",)),
        name="forget_cumsum",
    )(gate, b_f.reshape(N_HEADS, 1), u)


def _flash_kernel(*refs, bq, bk, dqk, scale2, window_blocks, has_decay, hps):
    q_ref, k_ref, v_ref, tile_ref = refs[:4]
    ck_ref = refs[4] if has_decay else None
    o_ref = refs[4 + has_decay]
    dv = HEAD_DIM
    n_diag = bq // bk
    qi = pl.program_id(1)
    qs = [q_ref[:, g * dqk:(g + 1) * dqk] for g in range(hps)]

    def head_step(g, j, carry, tile):
        m, l, acc = carry
        start = pl.multiple_of(j * bk, bk)
        k = k_ref[pl.ds(start, bk), g * dqk:(g + 1) * dqk]
        v = v_ref[pl.ds(start, bk), g * dv:(g + 1) * dv]
        s = lax.dot_general(qs[g], k, (((1,), (1,)), ((), ())), preferred_element_type=F32) * scale2
        if has_decay:
            s = s - ck_ref[g, pl.ds(j, 1), :]
        if tile is not None:
            s = s + tile
        m_new = jnp.maximum(m, jnp.max(s, axis=-1, keepdims=True))
        alpha = jnp.exp2(m - m_new)
        p = jnp.exp2(s - m_new)
        l = alpha * l + jnp.sum(p, axis=-1, keepdims=True)
        acc = alpha * acc + jnp.dot(p.astype(BF16), v, preferred_element_type=F32)
        return m_new, l, acc

    def step(j, carry, tile):
        return tuple(head_step(g, j, carry[g], tile) for g in range(hps))

    carry = tuple((jnp.full((bq, 1), NEG_INF, F32), jnp.zeros((bq, 1), F32), jnp.zeros((bq, dv), F32))
                  for _ in range(hps))
    j_diag = qi * n_diag
    if window_blocks is None:
        lo = 0
        body = lambda j, c: step(j, c, None)
    else:
        lo = jnp.maximum(j_diag - window_blocks, 0)
        body = lambda j, c: step(j, c, tile_ref[n_diag - 1 + j_diag - j])
    carry = lax.fori_loop(lo, j_diag, body, carry)
    for t in range(n_diag):
        carry = step(j_diag + t, carry, tile_ref[n_diag - 1 - t])
    for g, (m, l, acc) in enumerate(carry):
        o_ref[:, g * dv:(g + 1) * dv] = (acc / l).astype(o_ref.dtype)


def flash_attention(q_arr, q_cb, k_arr, k_cb, v_arr, v_cb, *, dqk, scale, tiles, window_blocks=None,
                    decay=None, hps=1):
    s_len = q_arr.shape[0]
    _, bq, bk = tiles.shape
    nq, nk = s_len // bq, s_len // bk
    assert q_cb % hps == 0 and k_cb % hps == 0 and v_cb % hps == 0
    in_specs = [pl.BlockSpec((bq, hps * dqk), lambda h, i: (i, q_cb // hps + h)),
                pl.BlockSpec((s_len, hps * dqk), lambda h, i: (0, k_cb // hps + h)),
                pl.BlockSpec((s_len, hps * HEAD_DIM), lambda h, i: (0, v_cb // hps + h)),
                pl.BlockSpec(tiles.shape, lambda h, i: (0, 0, 0))]
    args = [q_arr, k_arr, v_arr, tiles]
    if decay is not None:
        in_specs.append(pl.BlockSpec((hps, nk, bk), lambda h, i: (h, 0, 0)))
        args.append(decay.reshape(N_HEADS, nk, bk))
    return pl.pallas_call(
        functools.partial(_flash_kernel, bq=bq, bk=bk, dqk=dqk, scale2=scale * LOG2E,
                          window_blocks=window_blocks, has_decay=decay is not None, hps=hps),
        grid=(N_HEADS // hps, nq),
        in_specs=in_specs,
        out_specs=pl.BlockSpec((bq, hps * HEAD_DIM), lambda h, i: (i, h)),
        out_shape=jax.ShapeDtypeStruct((s_len, GROUP_WIDTH), F32),
        compiler_params=_cparams(("parallel", "arbitrary")),
        name="flash_attention",
    )(*args)


def _score_tiles(bq, bk, count_fn, window):
    n_diag = bq // bk
    window_blocks = 0 if window is None else (window + bk - 1) // bk
    rel = jnp.arange(-(n_diag - 1), window_blocks + 1)
    d = rel[:, None, None] * bk + jnp.arange(bq)[None, :, None] - jnp.arange(bk)[None, None, :]
    count = jnp.where(d >= 0, count_fn(d), 0.0)
    tiles = jnp.where(count > 0, jnp.log2(jnp.maximum(count, 1.0)), NEG_INF).astype(F32)
    return tiles, (None if window is None else window_blocks)


def _causal_tiles(bq, bk):
    return _score_tiles(bq, bk, lambda d: jnp.ones(d.shape, F32), None)[0]


def _dilated_tiles(bq, bk):
    count_fn = lambda d: sum(((d <= w) & (d % dil == 0)).astype(F32) for w, dil in DILATED_PAIRS)
    return _score_tiles(bq, bk, count_fn, max(w for w, _ in DILATED_PAIRS))


def _stick_kernel(q_ref, k_ref, v_ref, t_ref, o_ref, *, bq, bk, scale, hps):
    qi = pl.program_id(1)
    d = HEAD_DIM
    n_diag = bq // bk
    qs = [q_ref[:, g * d:(g + 1) * d] for g in range(hps)]
    tri = t_ref[...]

    def head_block(g, j, r, acc, diag_t):
        start = pl.multiple_of(j * bk, bk)
        k = k_ref[pl.ds(start, bk), g * d:(g + 1) * d]
        v = v_ref[pl.ds(start, bk), g * d:(g + 1) * d]
        z = lax.dot_general(qs[g], k, (((1,), (1,)), ((), ())), preferred_element_type=F32) * scale
        sp = jnp.maximum(z, 0.0) + jnp.log(1.0 + jnp.exp(-jnp.abs(z)))
        log_1m = -sp
        if diag_t is not None:
            mask = (lax.broadcasted_iota(jnp.int32, (bq, bk), 1) + diag_t * bk
                    < lax.broadcasted_iota(jnp.int32, (bq, bk), 0))
            log_1m = jnp.where(mask, log_1m, 0.0)
        hi = log_1m.astype(BF16)
        lo = (log_1m - hi.astype(F32)).astype(BF16)
        after = (jnp.dot(hi, tri, preferred_element_type=F32) + jnp.dot(lo, tri, preferred_element_type=F32)) + r
        a = jnp.exp((z - sp) + after)
        if diag_t is not None:
            a = jnp.where(mask, a, 0.0)
        acc = acc + jnp.dot(a.astype(BF16), v, preferred_element_type=F32)
        r = r + jnp.sum(log_1m, axis=-1, keepdims=True)
        return r, acc

    def block(j, state, diag_t):
        return tuple(head_block(g, j, *state[g], diag_t) for g in range(hps))

    state = tuple((jnp.zeros((bq, 1), F32), jnp.zeros((bq, d), F32)) for _ in range(hps))
    j_diag = qi * n_diag
    for t in reversed(range(n_diag)):
        state = block(j_diag + t, state, t)

    def cond(c):
        j, state = c
        r_max = functools.reduce(jnp.maximum, [jnp.max(r) for r, _ in state])
        return jnp.logical_and(j >= 0, r_max > -F32_EXP_UNDERFLOW)

    def body(c):
        j, state = c
        return j - 1, block(j, state, None)

    _, state = lax.while_loop(cond, body, (j_diag - 1, state))
    for g, (_, acc) in enumerate(state):
        o_ref[:, g * d:(g + 1) * d] = acc.astype(o_ref.dtype)


def stick_breaking_attention(qkv, q_cb, k_cb, v_cb, *, scale, bq=512, bk=256, hps=2):
    s_len = qkv.shape[0]
    tri = jnp.tril(jnp.ones((bk, bk), F32), -1).astype(BF16)
    w = hps * HEAD_DIM
    assert q_cb % hps == 0 and k_cb % hps == 0 and v_cb % hps == 0
    return pl.pallas_call(
        functools.partial(_stick_kernel, bq=bq, bk=bk, scale=scale, hps=hps),
        grid=(N_HEADS // hps, s_len // bq),
        in_specs=[pl.BlockSpec((bq, w), lambda h, i: (i, q_cb // hps + h)),
                  pl.BlockSpec((s_len, w), lambda h, i: (0, k_cb // hps + h)),
                  pl.BlockSpec((s_len, w), lambda h, i: (0, v_cb // hps + h)),
                  pl.BlockSpec((bk, bk), lambda h, i: (0, 0))],
        out_specs=pl.BlockSpec((bq, w), lambda h, i: (i, h)),
        out_shape=jax.ShapeDtypeStruct((s_len, GROUP_WIDTH), F32),
        compiler_params=_cparams(("parallel", "arbitrary")),
        name="stick_breaking",
    )(qkv, qkv, qkv, tri)


def _prep_mla_weights(w_uq, w_uk, w_uv):
    wq = w_uq.reshape(Q_LORA_RANK, N_HEADS, HEAD_DIM + MLA_ROPE_DIM)
    wq_nope = wq[:, :, :HEAD_DIM].reshape(Q_LORA_RANK, GROUP_WIDTH)
    wq_rope = jnp.pad(wq[:, :, HEAD_DIM:], ((0, 0), (0, 0), (0, HEAD_DIM - MLA_ROPE_DIM))).reshape(Q_LORA_RANK, GROUP_WIDTH)
    w_q = jnp.concatenate([wq_nope, wq_rope], axis=1).astype(BF16)
    w_kv = jnp.concatenate([w_uk, w_uv], axis=1).astype(BF16)
    return w_q, w_kv


def kernel(x, g_attn, w_in, g_q, g_kv, w_uq, w_uk, w_uv, b_f, g_out, w_o, g_mlp, w_up, w_down, g_final):
    b, s_len, d_model = x.shape
    assert b == 1 and s_len % 2048 == 0
    depth = w_in.shape[0]
    xf = x.reshape(s_len, d_model)

    mla_tables = _rope_tables(s_len, MLA_ROPE_DIM, LANES, pass_through=False)
    dil_tables = _rope_tables(s_len, PARTIAL_ROPE_DIM, HEAD_DIM, pass_through=True)
    causal = _causal_tiles(FLASH_BQ, FLASH_BK)
    dil_tiles, dil_blocks = _dilated_tiles(DILATED_BQ, DILATED_BK)
    scale = HEAD_DIM ** -0.5
    mla_scale = (HEAD_DIM + MLA_ROPE_DIM) ** -0.5
    nh = N_HEADS
    n_a = Q_LORA_RANK + KV_LORA_RANK + MLA_ROPE_DIM
    n_a128 = -(-n_a // LANES) * LANES
    gate_col0 = w_in.shape[2] - LANES
    w_in_t = jnp.swapaxes(w_in, 1, 2)

    for l in range(depth):
        w_q, w_kv = _prep_mla_weights(w_uq[l], w_uk[l], w_uv[l])

        h = rmsnorm(xf, g_attn[l], BF16)
        seg_a = matmul_f32w(h, w_in_t, l, 0, n_a128, F32, transposed=True, tn=n_a128 // 2)
        bqk_r = matmul_f32w(h, w_in_t, l, n_a, 2 * GROUP_WIDTH, BF16, transposed=True,
                            rope=(dil_tables, PARTIAL_ROPE_DIM // 2))
        rest = matmul_f32w(h, w_in_t, l, n_a + 2 * GROUP_WIDTH, 7 * GROUP_WIDTH, BF16, transposed=True)
        gate = matmul_f32w(h, w_in_t, l, gate_col0, LANES, F32, transposed=True, tn=LANES)

        cq_n, ckv_n = mla_norms(seg_a, g_q[l], g_kv[l])
        qa = matmul(cq_n, w_q, F32)
        kv = matmul(ckv_n, w_kv, BF16)
        q_cat, k_cat = mla_cat(qa, kv, seg_a, mla_tables)
        o_a = flash_attention(q_cat, 0, k_cat, 0, kv, nh, dqk=2 * HEAD_DIM, scale=mla_scale, tiles=causal, hps=2)

        o_b = flash_attention(bqk_r, 0, bqk_r, nh, rest, 0, dqk=HEAD_DIM, scale=scale, tiles=dil_tiles,
                              window_blocks=dil_blocks, hps=2)

        o_c = stick_breaking_attention(rest, nh, 2 * nh, 3 * nh, scale=scale)

        c_f = forget_cumsum(gate, b_f[l], out_scale=LOG2E)
        o_d = flash_attention(rest, 4 * nh, rest, 5 * nh, rest, 6 * nh, dqk=HEAD_DIM, scale=scale, tiles=causal,
                              decay=c_f, hps=2)

        mix = group_norm_concat(o_a, o_b, o_c, o_d, g_out[l])
        xf = matmul_f32w(mix, w_o, l, 0, d_model, F32, res=xf)

        h2 = rmsnorm(xf, g_mlp[l], BF16)
        u, w_down_bf = matmul_f32w(h2, w_up, l, 0, w_up.shape[2], BF16, act="relu2", side_cast=(w_down, l))
        xf = matmul(u, w_down_bf, F32, res=xf)

    return rmsnorm(xf, g_final, F32).reshape(b, s_len, d_model)
```

```python
import functools
import math

import jax
import jax.numpy as jnp
from jax import lax
from jax.experimental import pallas as pl
from jax.experimental.pallas import tpu as pltpu

F32 = jnp.float32
BF16 = jnp.bfloat16

HEAD_DIM = 128
N_HEADS = 8
GROUP_WIDTH = N_HEADS * HEAD_DIM
Q_LORA_RANK = 896
KV_LORA_RANK = 512
MLA_ROPE_DIM = 64
ROPE_THETA = 500000.0
PARTIAL_ROPE_DIM = HEAD_DIM // 4
DILATED_PAIRS = ((128, 1), (512, 4), (2048, 16))
EPS = 1e-6
NEG_INF = -1e30

LANES = 128
VMEM_LIMIT_BYTES = 62 * 1024 * 1024

F32_EXP_UNDERFLOW = 104.0
LOG2E = math.log2(math.e)

FLASH_BQ, FLASH_BK = 1024, 1024
DILATED_BQ, DILATED_BK = 512, 512


def _cparams(semantics):
    return pltpu.CompilerParams(dimension_semantics=semantics, vmem_limit_bytes=VMEM_LIMIT_BYTES)


def _rms(x, g):
    return x * lax.rsqrt(jnp.mean(x * x, axis=-1, keepdims=True) + EPS) * g


def _rmsnorm_kernel(x_ref, g_ref, o_ref):
    o_ref[...] = _rms(x_ref[...], g_ref[...]).astype(o_ref.dtype)


def rmsnorm(x, g, out_dtype, tm=512):
    m, d = x.shape
    return pl.pallas_call(
        _rmsnorm_kernel,
        grid=(m // tm,),
        in_specs=[pl.BlockSpec((tm, d), lambda i: (i, 0)), pl.BlockSpec((1, d), lambda i: (0, 0))],
        out_specs=pl.BlockSpec((tm, d), lambda i: (i, 0)),
        out_shape=jax.ShapeDtypeStruct((m, d), out_dtype),
        compiler_params=_cparams(("parallel",)),
        name="rmsnorm",
    )(x, g.reshape(1, d))


def _mla_norms_kernel(x_ref, gq_ref, gkv_ref, oq_ref, okv_ref):
    oq_ref[...] = _rms(x_ref[:, :Q_LORA_RANK], gq_ref[...]).astype(oq_ref.dtype)
    okv_ref[...] = _rms(x_ref[:, Q_LORA_RANK:Q_LORA_RANK + KV_LORA_RANK], gkv_ref[...]).astype(okv_ref.dtype)


def mla_norms(seg_a, g_q, g_kv, tm=512):
    m, w = seg_a.shape
    return pl.pallas_call(
        _mla_norms_kernel,
        grid=(m // tm,),
        in_specs=[pl.BlockSpec((tm, w), lambda i: (i, 0)),
                  pl.BlockSpec((1, Q_LORA_RANK), lambda i: (0, 0)),
                  pl.BlockSpec((1, KV_LORA_RANK), lambda i: (0, 0))],
        out_specs=[pl.BlockSpec((tm, Q_LORA_RANK), lambda i: (i, 0)),
                   pl.BlockSpec((tm, KV_LORA_RANK), lambda i: (i, 0))],
        out_shape=[jax.ShapeDtypeStruct((m, Q_LORA_RANK), BF16),
                   jax.ShapeDtypeStruct((m, KV_LORA_RANK), BF16)],
        compiler_params=_cparams(("parallel",)),
        name="mla_norms",
    )(seg_a, g_q.reshape(1, -1), g_kv.reshape(1, -1))


def _group_norm_kernel(a_ref, b_ref, c_ref, d_ref, g_ref, o_ref):
    for gi, r in enumerate((a_ref, b_ref, c_ref, d_ref)):
        sl = slice(gi * GROUP_WIDTH, (gi + 1) * GROUP_WIDTH)
        o_ref[:, sl] = _rms(r[...], g_ref[:, sl]).astype(o_ref.dtype)


def group_norm_concat(o_a, o_b, o_c, o_d, g, tm=256):
    m = o_a.shape[0]
    spec = pl.BlockSpec((tm, GROUP_WIDTH), lambda i: (i, 0))
    return pl.pallas_call(
        _group_norm_kernel,
        grid=(m // tm,),
        in_specs=[spec, spec, spec, spec, pl.BlockSpec((1, 4 * GROUP_WIDTH), lambda i: (0, 0))],
        out_specs=pl.BlockSpec((tm, 4 * GROUP_WIDTH), lambda i: (i, 0)),
        out_shape=jax.ShapeDtypeStruct((m, 4 * GROUP_WIDTH), BF16),
        compiler_params=_cparams(("parallel",)),
        name="group_norm",
    )(o_a, o_b, o_c, o_d, g.reshape(1, -1))


def _mm_kernel(*refs, nk, act, has_res):
    a_ref, w_ref = refs[0], refs[1]
    res_ref = refs[2] if has_res else None
    o_ref = refs[2 + has_res]

    def finish(acc):
        if act == "relu2":
            acc = jnp.square(jnp.maximum(acc, 0.0))
        if has_res:
            acc = res_ref[...] + acc
        o_ref[...] = acc.astype(o_ref.dtype)

    if nk == 1:
        finish(jnp.dot(a_ref[...], w_ref[...], preferred_element_type=F32))
    elif act is None and o_ref.dtype == F32:
        k = pl.program_id(2)

        @pl.when(k == 0)
        def _():
            o_ref[...] = res_ref[...] if has_res else jnp.zeros_like(o_ref)

        o_ref[...] += jnp.dot(a_ref[...], w_ref[...], preferred_element_type=F32)
    else:
        acc_ref = refs[3 + has_res]
        k = pl.program_id(2)

        @pl.when(k == 0)
        def _():
            acc_ref[...] = jnp.zeros_like(acc_ref)

        acc_ref[...] += jnp.dot(a_ref[...], w_ref[...], preferred_element_type=F32)

        @pl.when(k == nk - 1)
        def _():
            finish(acc_ref[...])


def _pick(n, prefs):
    for p in prefs:
        if n % p == 0:
            return p
    return n


def matmul(a, w, out_dtype, act=None, res=None, layer=None, tm=None, tn=None, tk=None, name="matmul"):
    m, kdim = a.shape
    n = w.shape[-1]
    tm = tm or _pick(m, (1024, 512, 256))
    tn = tn or _pick(n, (1024, 768, 512, 256, 128))
    has_res = res is not None
    tk = tk or _pick(kdim, (4096, 2048, 1024))
    nk = kdim // tk
    w_spec = (pl.BlockSpec((tk, tn), lambda i, j, k: (k, j)) if layer is None else
              pl.BlockSpec((None, tk, tn), lambda i, j, k: (layer, k, j)))
    in_specs = [pl.BlockSpec((tm, tk), lambda i, j, k: (i, k)), w_spec]
    args = [a, w]
    if has_res:
        in_specs.append(pl.BlockSpec((tm, tn), lambda i, j, k: (i, j)))
        args.append(res)
    scratch = [pltpu.VMEM((tm, tn), F32)] if (nk > 1 and not (act is None and out_dtype == F32)) else []
    return pl.pallas_call(
        functools.partial(_mm_kernel, nk=nk, act=act, has_res=has_res),
        grid=(m // tm, n // tn, nk),
        in_specs=in_specs,
        out_specs=pl.BlockSpec((tm, tn), lambda i, j, k: (i, j)),
        out_shape=jax.ShapeDtypeStruct((m, n), out_dtype),
        scratch_shapes=scratch,
        compiler_params=_cparams(("parallel", "parallel", "arbitrary")),
        name=name,
    )(*args)


def _mmw_kernel(*refs, layer, col0, tn, nj, transposed, act, has_res, rope_half, has_side, row_chunk):
    a_ref, w_hbm = refs[0], refs[1]
    idx = 2
    res_ref = tab_refs = None
    if has_res:
        res_ref = refs[idx]
        idx += 1
    if rope_half:
        tab_refs = refs[idx:idx + 3]
        idx += 3
    if has_side:
        refs[idx + 2][...] = refs[idx][...].astype(BF16)
        idx += 1
    o_ref = refs[idx]
    wf32_ref, wbf_ref, sem = refs[idx + 1 + has_side:]
    j = pl.program_id(0)

    def w_copy(jt):
        start = pl.multiple_of(col0 + jt * tn, 8 if transposed else LANES)
        src = w_hbm.at[layer, pl.ds(start, tn), :] if transposed else w_hbm.at[layer, :, pl.ds(start, tn)]
        return pltpu.make_async_copy(src, wf32_ref, sem.at[0])

    @pl.when(pl.program_id(1) == 0)
    def _():
        @pl.when(j == 0)
        def _():
            w_copy(0).start()

        w_copy(j).wait()

        def cast_rows(r, c):
            rows = pl.ds(pl.multiple_of(r * row_chunk, row_chunk), row_chunk)
            wbf_ref[rows, :] = wf32_ref[rows, :].astype(BF16)
            return c
        lax.fori_loop(0, wf32_ref.shape[0] // row_chunk, cast_rows, 0)

        @pl.when(j + 1 < nj)
        def _():
            w_copy(j + 1).start()

    dims = (((1,), (1,)), ((), ())) if transposed else (((1,), (0,)), ((), ()))
    acc = lax.dot_general(a_ref[...], wbf_ref[...], dims, preferred_element_type=F32)
    if act == "relu2":
        acc = jnp.square(jnp.maximum(acc, 0.0))
    if rope_half:
        acc = _rope(acc, *(t[...] for t in tab_refs), rope_half)
    if has_res:
        acc = res_ref[...] + acc
    o_ref[...] = acc.astype(o_ref.dtype)


def matmul_f32w(a, w_stack, layer, col0, n, out_dtype, act=None, res=None, rope=None, transposed=False,
                side_cast=None, tm=1024, tn=1024, name="matmul_f32w"):
    m, kdim = a.shape
    has_res = res is not None
    assert col0 % (8 if transposed else LANES) == 0 and n % tn == 0 and m % tm == 0
    w_shape = (tn, kdim) if transposed else (kdim, tn)
    in_specs = [pl.BlockSpec((tm, kdim), lambda j, i: (i, 0)), pl.BlockSpec(memory_space=pl.ANY)]
    args = [a, w_stack]
    if has_res:
        in_specs.append(pl.BlockSpec((tm, tn), lambda j, i: (i, j)))
        args.append(res)
    rope_half = 0
    if rope is not None:
        tables, rope_half = rope
        in_specs += [pl.BlockSpec((tm, tables[0].shape[1]), lambda j, i: (i, 0))] * 3
        args += list(tables)
    out_specs = pl.BlockSpec((tm, tn), lambda j, i: (i, j))
    out_shape = jax.ShapeDtypeStruct((m, n), out_dtype)
    nj, ni = n // tn, m // tm
    if side_cast is not None:
        side_stack, side_layer = side_cast
        _, side_r, side_c = side_stack.shape
        slab = side_r // (nj * ni)
        assert slab * nj * ni == side_r and slab % 16 == 0
        in_specs.append(pl.BlockSpec((None, slab, side_c), lambda j, i: (side_layer, j * ni + i, 0)))
        args.append(side_stack)
        out_specs = [out_specs, pl.BlockSpec((slab, side_c), lambda j, i: (j * ni + i, 0))]
        out_shape = [out_shape, jax.ShapeDtypeStruct((side_r, side_c), BF16)]
    return pl.pallas_call(
        functools.partial(_mmw_kernel, layer=layer, col0=col0, tn=tn, nj=nj, transposed=transposed, act=act,
                          has_res=has_res, rope_half=rope_half, has_side=side_cast is not None,
                          row_chunk=min(256, w_shape[0])),
        grid=(nj, ni),
        in_specs=in_specs,
        out_specs=out_specs,
        out_shape=out_shape,
        scratch_shapes=[pltpu.VMEM(w_shape, F32), pltpu.VMEM(w_shape, BF16), pltpu.SemaphoreType.DMA((1,))],
        compiler_params=_cparams(("arbitrary", "arbitrary")),
        name=name,
    )(*args)


def _rope_tables(s_len, r, period, pass_through):
    half = r // 2
    inv = ROPE_THETA ** (-jnp.arange(half, dtype=F32) * (2.0 / r))
    ang = jnp.arange(s_len, dtype=F32)[:, None] * inv[None, :]
    cos, sin = jnp.cos(ang), jnp.sin(ang)
    fill = jnp.ones if pass_through else jnp.zeros
    zeros = jnp.zeros((s_len, period - r), F32)
    zh = jnp.zeros((s_len, half), F32)
    c = jnp.concatenate([cos, cos, fill((s_len, period - r), F32)], axis=1)
    sa = jnp.concatenate([-sin, zh, zeros], axis=1)
    sb = jnp.concatenate([zh, sin, zeros], axis=1)
    return c, sa, sb


def _rope(x, c, sa, sb, half):
    w = x.shape[-1]
    reps = w // c.shape[-1]
    if reps > 1:
        c, sa, sb = (jnp.tile(t, (1, reps)) for t in (c, sa, sb))
    return x * c + pltpu.roll(x, w - half, 1) * sa + pltpu.roll(x, half, 1) * sb


def _mla_cat_kernel(qa_ref, kn_ref, kr_ref, c_ref, sa_ref, sb_ref, q_ref, k_ref):
    half = MLA_ROPE_DIM // 2
    c, sa, sb = c_ref[...], sa_ref[...], sb_ref[...]
    qr = _rope(qa_ref[:, GROUP_WIDTH:], c, sa, sb, half).astype(BF16)
    kr = _rope(kr_ref[...], c, sa, sb, half).astype(BF16)
    for h in range(N_HEADS):
        lo = h * 2 * HEAD_DIM
        q_ref[:, lo:lo + HEAD_DIM] = qa_ref[:, h * HEAD_DIM:(h + 1) * HEAD_DIM].astype(BF16)
        q_ref[:, lo + HEAD_DIM:lo + 2 * HEAD_DIM] = qr[:, h * HEAD_DIM:(h + 1) * HEAD_DIM]
        k_ref[:, lo:lo + HEAD_DIM] = kn_ref[:, h * HEAD_DIM:(h + 1) * HEAD_DIM]
        k_ref[:, lo + HEAD_DIM:lo + 2 * HEAD_DIM] = kr


def mla_cat(qa, kv, seg_a, tables, tm=512):
    m = qa.shape[0]
    kr_block = (Q_LORA_RANK + KV_LORA_RANK) // LANES
    tspec = pl.BlockSpec((tm, LANES), lambda i: (i, 0))
    out_w = N_HEADS * 2 * HEAD_DIM
    return pl.pallas_call(
        _mla_cat_kernel,
        grid=(m // tm,),
        in_specs=[pl.BlockSpec((tm, 2 * GROUP_WIDTH), lambda i: (i, 0)),
                  pl.BlockSpec((tm, GROUP_WIDTH), lambda i: (i, 0)),
                  pl.BlockSpec((tm, LANES), lambda i: (i, kr_block)),
                  tspec, tspec, tspec],
        out_specs=[pl.BlockSpec((tm, out_w), lambda i: (i, 0)), pl.BlockSpec((tm, out_w), lambda i: (i, 0))],
        out_shape=[jax.ShapeDtypeStruct((m, out_w), BF16), jax.ShapeDtypeStruct((m, out_w), BF16)],
        compiler_params=_cparams(("parallel",)),
        name="mla_cat",
    )(qa, kv, seg_a, *tables)


def _split3(x):
    hi = x.astype(BF16)
    r1 = x - hi.astype(F32)
    mid = r1.astype(BF16)
    lo = (r1 - mid.astype(F32)).astype(BF16)
    return hi, mid, lo


def _forget_kernel(x_ref, b_ref, u_ref, o_ref, carry_ref, *, out_scale):
    @pl.when(pl.program_id(0) == 0)
    def _():
        carry_ref[...] = jnp.zeros_like(carry_ref)

    xt = jnp.transpose(x_ref[...])
    y = -(xt[LANES - N_HEADS:, :] + b_ref[...])
    logf = -(jnp.maximum(y, 0.0) + jnp.log(1.0 + jnp.exp(-jnp.abs(y))))
    u = u_ref[...]
    cs = sum(jnp.dot(p, u, preferred_element_type=F32) for p in _split3(logf))
    c = cs + carry_ref[:, 0:1]
    o_ref[...] = c * out_scale
    carry_ref[...] = jnp.broadcast_to(c[:, -1:], carry_ref.shape)


def forget_cumsum(gate, b_f, tc=512, out_scale=1.0):
    m = gate.shape[0]
    u = jnp.triu(jnp.ones((tc, tc), F32)).astype(BF16)
    return pl.pallas_call(
        functools.partial(_forget_kernel, out_scale=out_scale),
        grid=(m // tc,),
        in_specs=[pl.BlockSpec((tc, LANES), lambda i: (i, 0)),
                  pl.BlockSpec((N_HEADS, 1), lambda i: (0, 0)),
                  pl.BlockSpec((tc, tc), lambda i: (0, 0))],
        out_specs=pl.BlockSpec((N_HEADS, tc), lambda i: (0, i)),
        out_shape=jax.ShapeDtypeStruct((N_HEADS, m), F32),
        scratch_shapes=[pltpu.VMEM((N_HEADS, LANES), F32)],
        compiler_params=_cparams(("arbitrary",)),
        name="forget_cumsum",
    )(gate, b_f.reshape(N_HEADS, 1), u)


def _flash_kernel(*refs, bq, bk, dqk, scale2, window_blocks, has_decay, hps):
    q_ref, k_ref, v_ref, tile_ref = refs[:4]
    ck_ref = refs[4] if has_decay else None
    o_ref = refs[4 + has_decay]
    dv = HEAD_DIM
    n_diag = bq // bk
    qi = pl.program_id(1)
    qs = [q_ref[:, g * dqk:(g + 1) * dqk] for g in range(hps)]

    def head_step(g, j, carry, tile):
        m, l, acc = carry
        start = pl.multiple_of(j * bk, bk)
        k = k_ref[pl.ds(start, bk), g * dqk:(g + 1) * dqk]
        v = v_ref[pl.ds(start, bk), g * dv:(g + 1) * dv]
        s = lax.dot_general(qs[g], k, (((1,), (1,)), ((), ())), preferred_element_type=F32) * scale2
        if has_decay:
            s = s - ck_ref[g, pl.ds(j, 1), :]
        if tile is not None:
            s = s + tile
        m_new = jnp.maximum(m, jnp.max(s, axis=-1, keepdims=True))
        alpha = jnp.exp2(m - m_new)
        p = jnp.exp2(s - m_new)
        l = alpha * l + jnp.sum(p, axis=-1, keepdims=True)
        acc = alpha * acc + jnp.dot(p.astype(BF16), v, preferred_element_type=F32)
        return m_new, l, acc

    def step(j, carry, tile):
        return tuple(head_step(g, j, carry[g], tile) for g in range(hps))

    carry = tuple((jnp.full((bq, 1), NEG_INF, F32), jnp.zeros((bq, 1), F32), jnp.zeros((bq, dv), F32))
                  for _ in range(hps))
    j_diag = qi * n_diag
    if window_blocks is None:
        lo = 0
        body = lambda j, c: step(j, c, None)
    else:
        lo = jnp.maximum(j_diag - window_blocks, 0)
        body = lambda j, c: step(j, c, tile_ref[n_diag - 1 + j_diag - j])
    carry = lax.fori_loop(lo, j_diag, body, carry)
    for t in range(n_diag):
        carry = step(j_diag + t, carry, tile_ref[n_diag - 1 - t])
    for g, (m, l, acc) in enumerate(carry):
        o_ref[:, g * dv:(g + 1) * dv] = (acc / l).astype(o_ref.dtype)


def flash_attention(q_arr, q_cb, k_arr, k_cb, v_arr, v_cb, *, dqk, scale, tiles, window_blocks=None,
                    decay=None, hps=1):
    s_len = q_arr.shape[0]
    _, bq, bk = tiles.shape
    nq, nk = s_len // bq, s_len // bk
    assert q_cb % hps == 0 and k_cb % hps == 0 and v_cb % hps == 0
    in_specs = [pl.BlockSpec((bq, hps * dqk), lambda h, i: (i, q_cb // hps + h)),
                pl.BlockSpec((s_len, hps * dqk), lambda h, i: (0, k_cb // hps + h)),
                pl.BlockSpec((s_len, hps * HEAD_DIM), lambda h, i: (0, v_cb // hps + h)),
                pl.BlockSpec(tiles.shape, lambda h, i: (0, 0, 0))]
    args = [q_arr, k_arr, v_arr, tiles]
    if decay is not None:
        in_specs.append(pl.BlockSpec((hps, nk, bk), lambda h, i: (h, 0, 0)))
        args.append(decay.reshape(N_HEADS, nk, bk))
    return pl.pallas_call(
        functools.partial(_flash_kernel, bq=bq, bk=bk, dqk=dqk, scale2=scale * LOG2E,
                          window_blocks=window_blocks, has_decay=decay is not None, hps=hps),
        grid=(N_HEADS // hps, nq),
        in_specs=in_specs,
        out_specs=pl.BlockSpec((bq, hps * HEAD_DIM), lambda h, i: (i, h)),
        out_shape=jax.ShapeDtypeStruct((s_len, GROUP_WIDTH), F32),
        compiler_params=_cparams(("parallel", "arbitrary")),
        name="flash_attention",
    )(*args)


def _score_tiles(bq, bk, count_fn, window):
    n_diag = bq // bk
    window_blocks = 0 if window is None else (window + bk - 1) // bk
    rel = jnp.arange(-(n_diag - 1), window_blocks + 1)
    d = rel[:, None, None] * bk + jnp.arange(bq)[None, :, None] - jnp.arange(bk)[None, None, :]
    count = jnp.where(d >= 0, count_fn(d), 0.0)
    tiles = jnp.where(count > 0, jnp.log2(jnp.maximum(count, 1.0)), NEG_INF).astype(F32)
    return tiles, (None if window is None else window_blocks)


def _causal_tiles(bq, bk):
    return _score_tiles(bq, bk, lambda d: jnp.ones(d.shape, F32), None)[0]


def _dilated_tiles(bq, bk):
    count_fn = lambda d: sum(((d <= w) & (d % dil == 0)).astype(F32) for w, dil in DILATED_PAIRS)
    return _score_tiles(bq, bk, count_fn, max(w for w, _ in DILATED_PAIRS))


def _stick_kernel(q_ref, k_ref, v_ref, t_ref, o_ref, *, bq, bk, scale, hps):
    qi = pl.program_id(1)
    d = HEAD_DIM
    n_diag = bq // bk
    qs = [q_ref[:, g * d:(g + 1) * d] for g in range(hps)]
    tri = t_ref[...]

    def head_block(g, j, r, acc, diag_t):
        start = pl.multiple_of(j * bk, bk)
        k = k_ref[pl.ds(start, bk), g * d:(g + 1) * d]
        v = v_ref[pl.ds(start, bk), g * d:(g + 1) * d]
        z = lax.dot_general(qs[g], k, (((1,), (1,)), ((), ())), preferred_element_type=F32) * scale
        sp = jnp.maximum(z, 0.0) + jnp.log(1.0 + jnp.exp(-jnp.abs(z)))
        log_1m = -sp
        if diag_t is not None:
            mask = (lax.broadcasted_iota(jnp.int32, (bq, bk), 1) + diag_t * bk
                    < lax.broadcasted_iota(jnp.int32, (bq, bk), 0))
            log_1m = jnp.where(mask, log_1m, 0.0)
        hi = log_1m.astype(BF16)
        lo = (log_1m - hi.astype(F32)).astype(BF16)
        after = (jnp.dot(hi, tri, preferred_element_type=F32) + jnp.dot(lo, tri, preferred_element_type=F32)) + r
        a = jnp.exp((z - sp) + after)
        if diag_t is not None:
            a = jnp.where(mask, a, 0.0)
        acc = acc + jnp.dot(a.astype(BF16), v, preferred_element_type=F32)
        r = r + jnp.sum(log_1m, axis=-1, keepdims=True)
        return r, acc

    def block(j, state, diag_t):
        return tuple(head_block(g, j, *state[g], diag_t) for g in range(hps))

    state = tuple((jnp.zeros((bq, 1), F32), jnp.zeros((bq, d), F32)) for _ in range(hps))
    j_diag = qi * n_diag
    for t in reversed(range(n_diag)):
        state = block(j_diag + t, state, t)

    def cond(c):
        j, state = c
        r_max = functools.reduce(jnp.maximum, [jnp.max(r) for r, _ in state])
        return jnp.logical_and(j >= 0, r_max > -F32_EXP_UNDERFLOW)

    def body(c):
        j, state = c
        return j - 1, block(j, state, None)

    _, state = lax.while_loop(cond, body, (j_diag - 1, state))
    for g, (_, acc) in enumerate(state):
        o_ref[:, g * d:(g + 1) * d] = acc.astype(o_ref.dtype)


def stick_breaking_attention(qkv, q_cb, k_cb, v_cb, *, scale, bq=512, bk=256, hps=2):
    s_len = qkv.shape[0]
    tri = jnp.tril(jnp.ones((bk, bk), F32), -1).astype(BF16)
    w = hps * HEAD_DIM
    assert q_cb % hps == 0 and k_cb % hps == 0 and v_cb % hps == 0
    return pl.pallas_call(
        functools.partial(_stick_kernel, bq=bq, bk=bk, scale=scale, hps=hps),
        grid=(N_HEADS // hps, s_len // bq),
        in_specs=[pl.BlockSpec((bq, w), lambda h, i: (i, q_cb // hps + h)),
                  pl.BlockSpec((s_len, w), lambda h, i: (0, k_cb // hps + h)),
                  pl.BlockSpec((s_len, w), lambda h, i: (0, v_cb // hps + h)),
                  pl.BlockSpec((bk, bk), lambda h, i: (0, 0))],
        out_specs=pl.BlockSpec((bq, w), lambda h, i: (i, h)),
        out_shape=jax.ShapeDtypeStruct((s_len, GROUP_WIDTH), F32),
        compiler_params=_cparams(("parallel", "arbitrary")),
        name="stick_breaking",
    )(qkv, qkv, qkv, tri)


def _prep_mla_weights(w_uq, w_uk, w_uv):
    wq = w_uq.reshape(Q_LORA_RANK, N_HEADS, HEAD_DIM + MLA_ROPE_DIM)
    wq_nope = wq[:, :, :HEAD_DIM].reshape(Q_LORA_RANK, GROUP_WIDTH)
    wq_rope = jnp.pad(wq[:, :, HEAD_DIM:], ((0, 0), (0, 0), (0, HEAD_DIM - MLA_ROPE_DIM))).reshape(Q_LORA_RANK, GROUP_WIDTH)
    w_q = jnp.concatenate([wq_nope, wq_rope], axis=1).astype(BF16)
    w_kv = jnp.concatenate([w_uk, w_uv], axis=1).astype(BF16)
    return w_q, w_kv


def kernel(x, g_attn, w_in, g_q, g_kv, w_uq, w_uk, w_uv, b_f, g_out, w_o, g_mlp, w_up, w_down, g_final):
    b, s_len, d_model = x.shape
    assert b == 1 and s_len % 2048 == 0
    depth = w_in.shape[0]
    xf = x.reshape(s_len, d_model)

    mla_tables = _rope_tables(s_len, MLA_ROPE_DIM, LANES, pass_through=False)
    dil_tables = _rope_tables(s_len, PARTIAL_ROPE_DIM, HEAD_DIM, pass_through=True)
    causal = _causal_tiles(FLASH_BQ, FLASH_BK)
    dil_tiles, dil_blocks = _dilated_tiles(DILATED_BQ, DILATED_BK)
    scale = HEAD_DIM ** -0.5
    mla_scale = (HEAD_DIM + MLA_ROPE_DIM) ** -0.5
    nh = N_HEADS
    n_a = Q_LORA_RANK + KV_LORA_RANK + MLA_ROPE_DIM
    n_a128 = -(-n_a // LANES) * LANES
    gate_col0 = w_in.shape[2] - LANES
    w_in_t = jnp.swapaxes(w_in, 1, 2)

    for l in range(depth):
        w_q, w_kv = _prep_mla_weights(w_uq[l], w_uk[l], w_uv[l])

        h = rmsnorm(xf, g_attn[l], BF16)
        seg_a = matmul_f32w(h, w_in_t, l, 0, n_a128, F32, transposed=True, tn=n_a128 // 2)
        bqk_r = matmul_f32w(h, w_in_t, l, n_a, 2 * GROUP_WIDTH, BF16, transposed=True,
                            rope=(dil_tables, PARTIAL_ROPE_DIM // 2))
        rest = matmul_f32w(h, w_in_t, l, n_a + 2 * GROUP_WIDTH, 7 * GROUP_WIDTH, BF16, transposed=True)
        gate = matmul_f32w(h, w_in_t, l, gate_col0, LANES, F32, transposed=True, tn=LANES)

        cq_n, ckv_n = mla_norms(seg_a, g_q[l], g_kv[l])
        qa = matmul(cq_n, w_q, F32)
        kv = matmul(ckv_n, w_kv, BF16)
        q_cat, k_cat = mla_cat(qa, kv, seg_a, mla_tables)
        o_a = flash_attention(q_cat, 0, k_cat, 0, kv, nh, dqk=2 * HEAD_DIM, scale=mla_scale, tiles=causal, hps=2)

        o_b = flash_attention(bqk_r, 0, bqk_r, nh, rest, 0, dqk=HEAD_DIM, scale=scale, tiles=dil_tiles,
                              window_blocks=dil_blocks, hps=2)

        o_c = stick_breaking_attention(rest, nh, 2 * nh, 3 * nh, scale=scale)

        c_f = forget_cumsum(gate, b_f[l], out_scale=LOG2E)
        o_d = flash_attention(rest, 4 * nh, rest, 5 * nh, rest, 6 * nh, dqk=HEAD_DIM, scale=scale, tiles=causal,
                              decay=c_f, hps=2)

        mix = group_norm_concat(o_a, o_b, o_c, o_d, g_out[l])
        xf = matmul_f32w(mix, w_o, l, 0, d_model, F32, res=xf, tm=512)

        h2 = rmsnorm(xf, g_mlp[l], BF16)
        u, w_down_bf = matmul_f32w(h2, w_up, l, 0, w_up.shape[2], BF16, act="relu2", side_cast=(w_down, l))
        xf = matmul(u, w_down_bf, F32, res=xf)

    return rmsnorm(xf, g_final, F32).reshape(b, s_len, d_model)
```

```python
import functools
import math

import jax
import jax.numpy as jnp
from jax import lax
from jax.experimental import pallas as pl
from jax.experimental.pallas import tpu as pltpu

F32 = jnp.float32
BF16 = jnp.bfloat16

HEAD_DIM = 128
N_HEADS = 8
GROUP_WIDTH = N_HEADS * HEAD_DIM
Q_LORA_RANK = 896
KV_LORA_RANK = 512
MLA_ROPE_DIM = 64
ROPE_THETA = 500000.0
PARTIAL_ROPE_DIM = HEAD_DIM // 4
DILATED_PAIRS = ((128, 1), (512, 4), (2048, 16))
EPS = 1e-6
NEG_INF = -1e30

LANES = 128
VMEM_LIMIT_BYTES = 62 * 1024 * 1024

F32_EXP_UNDERFLOW = 104.0
LOG2E = math.log2(math.e)

FLASH_BQ, FLASH_BK = 1024, 1024
DILATED_BQ, DILATED_BK = 512, 512


def _cparams(semantics):
    return pltpu.CompilerParams(dimension_semantics=semantics, vmem_limit_bytes=VMEM_LIMIT_BYTES)


def _rms(x, g):
    return x * lax.rsqrt(jnp.mean(x * x, axis=-1, keepdims=True) + EPS) * g


def _rmsnorm_kernel(x_ref, g_ref, o_ref):
    o_ref[...] = _rms(x_ref[...], g_ref[...]).astype(o_ref.dtype)


def rmsnorm(x, g, out_dtype, tm=512):
    m, d = x.shape
    return pl.pallas_call(
        _rmsnorm_kernel,
        grid=(m // tm,),
        in_specs=[pl.BlockSpec((tm, d), lambda i: (i, 0)), pl.BlockSpec((1, d), lambda i: (0, 0))],
        out_specs=pl.BlockSpec((tm, d), lambda i: (i, 0)),
        out_shape=jax.ShapeDtypeStruct((m, d), out_dtype),
        compiler_params=_cparams(("parallel",)),
        name="rmsnorm",
    )(x, g.reshape(1, d))


def _mla_norms_kernel(x_ref, gq_ref, gkv_ref, oq_ref, okv_ref):
    oq_ref[...] = _rms(x_ref[:, :Q_LORA_RANK], gq_ref[...]).astype(oq_ref.dtype)
    okv_ref[...] = _rms(x_ref[:, Q_LORA_RANK:Q_LORA_RANK + KV_LORA_RANK], gkv_ref[...]).astype(okv_ref.dtype)


def mla_norms(seg_a, g_q, g_kv, tm=512):
    m, w = seg_a.shape
    return pl.pallas_call(
        _mla_norms_kernel,
        grid=(m // tm,),
        in_specs=[pl.BlockSpec((tm, w), lambda i: (i, 0)),
                  pl.BlockSpec((1, Q_LORA_RANK), lambda i: (0, 0)),
                  pl.BlockSpec((1, KV_LORA_RANK), lambda i: (0, 0))],
        out_specs=[pl.BlockSpec((tm, Q_LORA_RANK), lambda i: (i, 0)),
                   pl.BlockSpec((tm, KV_LORA_RANK), lambda i: (i, 0))],
        out_shape=[jax.ShapeDtypeStruct((m, Q_LORA_RANK), BF16),
                   jax.ShapeDtypeStruct((m, KV_LORA_RANK), BF16)],
        compiler_params=_cparams(("parallel",)),
        name="mla_norms",
    )(seg_a, g_q.reshape(1, -1), g_kv.reshape(1, -1))


def _group_norm_kernel(a_ref, b_ref, c_ref, d_ref, g_ref, o_ref):
    for gi, r in enumerate((a_ref, b_ref, c_ref, d_ref)):
        sl = slice(gi * GROUP_WIDTH, (gi + 1) * GROUP_WIDTH)
        o_ref[:, sl] = _rms(r[...].astype(F32), g_ref[:, sl]).astype(o_ref.dtype)


def group_norm_concat(o_a, o_b, o_c, o_d, g, tm=256):
    m = o_a.shape[0]
    spec = pl.BlockSpec((tm, GROUP_WIDTH), lambda i: (i, 0))
    return pl.pallas_call(
        _group_norm_kernel,
        grid=(m // tm,),
        in_specs=[spec, spec, spec, spec, pl.BlockSpec((1, 4 * GROUP_WIDTH), lambda i: (0, 0))],
        out_specs=pl.BlockSpec((tm, 4 * GROUP_WIDTH), lambda i: (i, 0)),
        out_shape=jax.ShapeDtypeStruct((m, 4 * GROUP_WIDTH), BF16),
        compiler_params=_cparams(("parallel",)),
        name="group_norm",
    )(o_a, o_b, o_c, o_d, g.reshape(1, -1))


def _mm_kernel(*refs, nk, act, has_res):
    a_ref, w_ref = refs[0], refs[1]
    res_ref = refs[2] if has_res else None
    o_ref = refs[2 + has_res]

    def finish(acc):
        if act == "relu2":
            acc = jnp.square(jnp.maximum(acc, 0.0))
        if has_res:
            acc = res_ref[...] + acc
        o_ref[...] = acc.astype(o_ref.dtype)

    if nk == 1:
        finish(jnp.dot(a_ref[...], w_ref[...], preferred_element_type=F32))
    elif act is None and o_ref.dtype == F32:
        k = pl.program_id(2)

        @pl.when(k == 0)
        def _():
            o_ref[...] = res_ref[...] if has_res else jnp.zeros_like(o_ref)

        o_ref[...] += jnp.dot(a_ref[...], w_ref[...], preferred_element_type=F32)
    else:
        acc_ref = refs[3 + has_res]
        k = pl.program_id(2)

        @pl.when(k == 0)
        def _():
            acc_ref[...] = jnp.zeros_like(acc_ref)

        acc_ref[...] += jnp.dot(a_ref[...], w_ref[...], preferred_element_type=F32)

        @pl.when(k == nk - 1)
        def _():
            finish(acc_ref[...])


def _pick(n, prefs):
    for p in prefs:
        if n % p == 0:
            return p
    return n


def matmul(a, w, out_dtype, act=None, res=None, layer=None, tm=None, tn=None, tk=None, name="matmul"):
    m, kdim = a.shape
    n = w.shape[-1]
    tm = tm or _pick(m, (1024, 512, 256))
    tn = tn or _pick(n, (1024, 768, 512, 256, 128))
    has_res = res is not None
    tk = tk or _pick(kdim, (4096, 2048, 1024))
    nk = kdim // tk
    w_spec = (pl.BlockSpec((tk, tn), lambda i, j, k: (k, j)) if layer is None else
              pl.BlockSpec((None, tk, tn), lambda i, j, k: (layer, k, j)))
    in_specs = [pl.BlockSpec((tm, tk), lambda i, j, k: (i, k)), w_spec]
    args = [a, w]
    if has_res:
        in_specs.append(pl.BlockSpec((tm, tn), lambda i, j, k: (i, j)))
        args.append(res)
    scratch = [pltpu.VMEM((tm, tn), F32)] if (nk > 1 and not (act is None and out_dtype == F32)) else []
    return pl.pallas_call(
        functools.partial(_mm_kernel, nk=nk, act=act, has_res=has_res),
        grid=(m // tm, n // tn, nk),
        in_specs=in_specs,
        out_specs=pl.BlockSpec((tm, tn), lambda i, j, k: (i, j)),
        out_shape=jax.ShapeDtypeStruct((m, n), out_dtype),
        scratch_shapes=scratch,
        compiler_params=_cparams(("parallel", "parallel", "arbitrary")),
        name=name,
    )(*args)


def _mmw_kernel(*refs, layer, col0, tn, nj, transposed, act, has_res, rope_half, has_side, row_chunk):
    a_ref, w_hbm = refs[0], refs[1]
    idx = 2
    res_ref = tab_refs = None
    if has_res:
        res_ref = refs[idx]
        idx += 1
    if rope_half:
        tab_refs = refs[idx:idx + 3]
        idx += 3
    if has_side:
        refs[idx + 2][...] = refs[idx][...].astype(BF16)
        idx += 1
    o_ref = refs[idx]
    wf32_ref, wbf_ref, sem = refs[idx + 1 + has_side:]
    j = pl.program_id(0)

    def w_copy(jt):
        start = pl.multiple_of(col0 + jt * tn, 8 if transposed else LANES)
        src = w_hbm.at[layer, pl.ds(start, tn), :] if transposed else w_hbm.at[layer, :, pl.ds(start, tn)]
        return pltpu.make_async_copy(src, wf32_ref, sem.at[0])

    @pl.when(pl.program_id(1) == 0)
    def _():
        @pl.when(j == 0)
        def _():
            w_copy(0).start()

        w_copy(j).wait()

        def cast_rows(r, c):
            rows = pl.ds(pl.multiple_of(r * row_chunk, row_chunk), row_chunk)
            wbf_ref[rows, :] = wf32_ref[rows, :].astype(BF16)
            return c
        lax.fori_loop(0, wf32_ref.shape[0] // row_chunk, cast_rows, 0)

        @pl.when(j + 1 < nj)
        def _():
            w_copy(j + 1).start()

    dims = (((1,), (1,)), ((), ())) if transposed else (((1,), (0,)), ((), ()))
    acc = lax.dot_general(a_ref[...], wbf_ref[...], dims, preferred_element_type=F32)
    if act == "relu2":
        acc = jnp.square(jnp.maximum(acc, 0.0))
    if rope_half:
        acc = _rope(acc, *(t[...] for t in tab_refs), rope_half)
    if has_res:
        acc = res_ref[...] + acc
    o_ref[...] = acc.astype(o_ref.dtype)


def matmul_f32w(a, w_stack, layer, col0, n, out_dtype, act=None, res=None, rope=None, transposed=False,
                side_cast=None, tm=1024, tn=1024, name="matmul_f32w"):
    m, kdim = a.shape
    has_res = res is not None
    assert col0 % (8 if transposed else LANES) == 0 and n % tn == 0 and m % tm == 0
    w_shape = (tn, kdim) if transposed else (kdim, tn)
    in_specs = [pl.BlockSpec((tm, kdim), lambda j, i: (i, 0)), pl.BlockSpec(memory_space=pl.ANY)]
    args = [a, w_stack]
    if has_res:
        in_specs.append(pl.BlockSpec((tm, tn), lambda j, i: (i, j)))
        args.append(res)
    rope_half = 0
    if rope is not None:
        tables, rope_half = rope
        in_specs += [pl.BlockSpec((tm, tables[0].shape[1]), lambda j, i: (i, 0))] * 3
        args += list(tables)
    out_specs = pl.BlockSpec((tm, tn), lambda j, i: (i, j))
    out_shape = jax.ShapeDtypeStruct((m, n), out_dtype)
    nj, ni = n // tn, m // tm
    if side_cast is not None:
        side_stack, side_layer = side_cast
        _, side_r, side_c = side_stack.shape
        slab = side_r // (nj * ni)
        assert slab * nj * ni == side_r and slab % 16 == 0
        in_specs.append(pl.BlockSpec((None, slab, side_c), lambda j, i: (side_layer, j * ni + i, 0)))
        args.append(side_stack)
        out_specs = [out_specs, pl.BlockSpec((slab, side_c), lambda j, i: (j * ni + i, 0))]
        out_shape = [out_shape, jax.ShapeDtypeStruct((side_r, side_c), BF16)]
    return pl.pallas_call(
        functools.partial(_mmw_kernel, layer=layer, col0=col0, tn=tn, nj=nj, transposed=transposed, act=act,
                          has_res=has_res, rope_half=rope_half, has_side=side_cast is not None,
                          row_chunk=min(256, w_shape[0])),
        grid=(nj, ni),
        in_specs=in_specs,
        out_specs=out_specs,
        out_shape=out_shape,
        scratch_shapes=[pltpu.VMEM(w_shape, F32), pltpu.VMEM(w_shape, BF16), pltpu.SemaphoreType.DMA((1,))],
        compiler_params=_cparams(("arbitrary", "arbitrary")),
        name=name,
    )(*args)


def _rope_tables(s_len, r, period, pass_through):
    half = r // 2
    inv = ROPE_THETA ** (-jnp.arange(half, dtype=F32) * (2.0 / r))
    ang = jnp.arange(s_len, dtype=F32)[:, None] * inv[None, :]
    cos, sin = jnp.cos(ang), jnp.sin(ang)
    fill = jnp.ones if pass_through else jnp.zeros
    zeros = jnp.zeros((s_len, period - r), F32)
    zh = jnp.zeros((s_len, half), F32)
    c = jnp.concatenate([cos, cos, fill((s_len, period - r), F32)], axis=1)
    sa = jnp.concatenate([-sin, zh, zeros], axis=1)
    sb = jnp.concatenate([zh, sin, zeros], axis=1)
    return c, sa, sb


def _rope(x, c, sa, sb, half):
    w = x.shape[-1]
    reps = w // c.shape[-1]
    if reps > 1:
        c, sa, sb = (jnp.tile(t, (1, reps)) for t in (c, sa, sb))
    return x * c + pltpu.roll(x, w - half, 1) * sa + pltpu.roll(x, half, 1) * sb


def _mla_cat_kernel(qa_ref, kn_ref, kr_ref, c_ref, sa_ref, sb_ref, q_ref, k_ref):
    half = MLA_ROPE_DIM // 2
    c, sa, sb = c_ref[...], sa_ref[...], sb_ref[...]
    qr = _rope(qa_ref[:, GROUP_WIDTH:], c, sa, sb, half).astype(BF16)
    kr = _rope(kr_ref[...], c, sa, sb, half).astype(BF16)
    for h in range(N_HEADS):
        lo = h * 2 * HEAD_DIM
        q_ref[:, lo:lo + HEAD_DIM] = qa_ref[:, h * HEAD_DIM:(h + 1) * HEAD_DIM].astype(BF16)
        q_ref[:, lo + HEAD_DIM:lo + 2 * HEAD_DIM] = qr[:, h * HEAD_DIM:(h + 1) * HEAD_DIM]
        k_ref[:, lo:lo + HEAD_DIM] = kn_ref[:, h * HEAD_DIM:(h + 1) * HEAD_DIM]
        k_ref[:, lo + HEAD_DIM:lo + 2 * HEAD_DIM] = kr


def mla_cat(qa, kv, seg_a, tables, tm=512):
    m = qa.shape[0]
    kr_block = (Q_LORA_RANK + KV_LORA_RANK) // LANES
    tspec = pl.BlockSpec((tm, LANES), lambda i: (i, 0))
    out_w = N_HEADS * 2 * HEAD_DIM
    return pl.pallas_call(
        _mla_cat_kernel,
        grid=(m // tm,),
        in_specs=[pl.BlockSpec((tm, 2 * GROUP_WIDTH), lambda i: (i, 0)),
                  pl.BlockSpec((tm, GROUP_WIDTH), lambda i: (i, 0)),
                  pl.BlockSpec((tm, LANES), lambda i: (i, kr_block)),
                  tspec, tspec, tspec],
        out_specs=[pl.BlockSpec((tm, out_w), lambda i: (i, 0)), pl.BlockSpec((tm, out_w), lambda i: (i, 0))],
        out_shape=[jax.ShapeDtypeStruct((m, out_w), BF16), jax.ShapeDtypeStruct((m, out_w), BF16)],
        compiler_params=_cparams(("parallel",)),
        name="mla_cat",
    )(qa, kv, seg_a, *tables)


def _split3(x):
    hi = x.astype(BF16)
    r1 = x - hi.astype(F32)
    mid = r1.astype(BF16)
    lo = (r1 - mid.astype(F32)).astype(BF16)
    return hi, mid, lo


def _forget_kernel(x_ref, b_ref, u_ref, o_ref, carry_ref, *, out_scale):
    @pl.when(pl.program_id(0) == 0)
    def _():
        carry_ref[...] = jnp.zeros_like(carry_ref)

    xt = jnp.transpose(x_ref[...])
    y = -(xt[LANES - N_HEADS:, :] + b_ref[...])
    logf = -(jnp.maximum(y, 0.0) + jnp.log(1.0 + jnp.exp(-jnp.abs(y))))
    u = u_ref[...]
    cs = sum(jnp.dot(p, u, preferred_element_type=F32) for p in _split3(logf))
    c = cs + carry_ref[:, 0:1]
    o_ref[...] = c * out_scale
    carry_ref[...] = jnp.broadcast_to(c[:, -1:], carry_ref.shape)


def forget_cumsum(gate, b_f, tc=512, out_scale=1.0):
    m = gate.shape[0]
    u = jnp.triu(jnp.ones((tc, tc), F32)).astype(BF16)
    return pl.pallas_call(
        functools.partial(_forget_kernel, out_scale=out_scale),
        grid=(m // tc,),
        in_specs=[pl.BlockSpec((tc, LANES), lambda i: (i, 0)),
                  pl.BlockSpec((N_HEADS, 1), lambda i: (0, 0)),
                  pl.BlockSpec((tc, tc), lambda i: (0, 0))],
        out_specs=pl.BlockSpec((N_HEADS, tc), lambda i: (0, i)),
        out_shape=jax.ShapeDtypeStruct((N_HEADS, m), F32),
        scratch_shapes=[pltpu.VMEM((N_HEADS, LANES), F32)],
        compiler_params=_cparams(("arbitrary",)),
        name="forget_cumsum",
    )(gate, b_f.reshape(N_HEADS, 1), u)


def _flash_kernel(*refs, bq, bk, dqk, scale2, window_blocks, has_decay, hps):
    q_ref, k_ref, v_ref, tile_ref = refs[:4]
    ck_ref = refs[4] if has_decay else None
    o_ref = refs[4 + has_decay]
    dv = HEAD_DIM
    n_diag = bq // bk
    qi = pl.program_id(1)
    qs = [q_ref[:, g * dqk:(g + 1) * dqk] for g in range(hps)]

    def head_step(g, j, carry, tile):
        m, l, acc = carry
        start = pl.multiple_of(j * bk, bk)
        k = k_ref[pl.ds(start, bk), g * dqk:(g + 1) * dqk]
        v = v_ref[pl.ds(start, bk), g * dv:(g + 1) * dv]
        s = lax.dot_general(qs[g], k, (((1,), (1,)), ((), ())), preferred_element_type=F32) * scale2
        if has_decay:
            s = s - ck_ref[g, pl.ds(j, 1), :]
        if tile is not None:
            s = s + tile
        m_new = jnp.maximum(m, jnp.max(s, axis=-1, keepdims=True))
        alpha = jnp.exp2(m - m_new)
        p = jnp.exp2(s - m_new)
        l = alpha * l + jnp.sum(p, axis=-1, keepdims=True)
        acc = alpha * acc + jnp.dot(p.astype(BF16), v, preferred_element_type=F32)
        return m_new, l, acc

    def step(j, carry, tile):
        return tuple(head_step(g, j, carry[g], tile) for g in range(hps))

    carry = tuple((jnp.full((bq, 1), NEG_INF, F32), jnp.zeros((bq, 1), F32), jnp.zeros((bq, dv), F32))
                  for _ in range(hps))
    j_diag = qi * n_diag
    if window_blocks is None:
        lo = 0
        body = lambda j, c: step(j, c, None)
    else:
        lo = jnp.maximum(j_diag - window_blocks, 0)
        body = lambda j, c: step(j, c, tile_ref[n_diag - 1 + j_diag - j])
    carry = lax.fori_loop(lo, j_diag, body, carry)
    for t in range(n_diag):
        carry = step(j_diag + t, carry, tile_ref[n_diag - 1 - t])
    for g, (m, l, acc) in enumerate(carry):
        o_ref[:, g * dv:(g + 1) * dv] = (acc / l).astype(o_ref.dtype)


def flash_attention(q_arr, q_cb, k_arr, k_cb, v_arr, v_cb, *, dqk, scale, tiles, window_blocks=None,
                    decay=None, hps=1):
    s_len = q_arr.shape[0]
    _, bq, bk = tiles.shape
    nq, nk = s_len // bq, s_len // bk
    assert q_cb % hps == 0 and k_cb % hps == 0 and v_cb % hps == 0
    in_specs = [pl.BlockSpec((bq, hps * dqk), lambda h, i: (i, q_cb // hps + h)),
                pl.BlockSpec((s_len, hps * dqk), lambda h, i: (0, k_cb // hps + h)),
                pl.BlockSpec((s_len, hps * HEAD_DIM), lambda h, i: (0, v_cb // hps + h)),
                pl.BlockSpec(tiles.shape, lambda h, i: (0, 0, 0))]
    args = [q_arr, k_arr, v_arr, tiles]
    if decay is not None:
        in_specs.append(pl.BlockSpec((hps, nk, bk), lambda h, i: (h, 0, 0)))
        args.append(decay.reshape(N_HEADS, nk, bk))
    return pl.pallas_call(
        functools.partial(_flash_kernel, bq=bq, bk=bk, dqk=dqk, scale2=scale * LOG2E,
                          window_blocks=window_blocks, has_decay=decay is not None, hps=hps),
        grid=(N_HEADS // hps, nq),
        in_specs=in_specs,
        out_specs=pl.BlockSpec((bq, hps * HEAD_DIM), lambda h, i: (i, h)),
        out_shape=jax.ShapeDtypeStruct((s_len, GROUP_WIDTH), BF16),
        compiler_params=_cparams(("parallel", "arbitrary")),
        name="flash_attention",
    )(*args)


def _score_tiles(bq, bk, count_fn, window):
    n_diag = bq // bk
    window_blocks = 0 if window is None else (window + bk - 1) // bk
    rel = jnp.arange(-(n_diag - 1), window_blocks + 1)
    d = rel[:, None, None] * bk + jnp.arange(bq)[None, :, None] - jnp.arange(bk)[None, None, :]
    count = jnp.where(d >= 0, count_fn(d), 0.0)
    tiles = jnp.where(count > 0, jnp.log2(jnp.maximum(count, 1.0)), NEG_INF).astype(F32)
    return tiles, (None if window is None else window_blocks)


def _causal_tiles(bq, bk):
    return _score_tiles(bq, bk, lambda d: jnp.ones(d.shape, F32), None)[0]


def _dilated_tiles(bq, bk):
    count_fn = lambda d: sum(((d <= w) & (d % dil == 0)).astype(F32) for w, dil in DILATED_PAIRS)
    return _score_tiles(bq, bk, count_fn, max(w for w, _ in DILATED_PAIRS))


def _stick_kernel(q_ref, k_ref, v_ref, t_ref, o_ref, *, bq, bk, scale, hps):
    qi = pl.program_id(1)
    d = HEAD_DIM
    n_diag = bq // bk
    qs = [q_ref[:, g * d:(g + 1) * d] for g in range(hps)]
    tri = t_ref[...]

    def head_block(g, j, r, acc, diag_t):
        start = pl.multiple_of(j * bk, bk)
        k = k_ref[pl.ds(start, bk), g * d:(g + 1) * d]
        v = v_ref[pl.ds(start, bk), g * d:(g + 1) * d]
        z = lax.dot_general(qs[g], k, (((1,), (1,)), ((), ())), preferred_element_type=F32) * scale
        sp = jnp.maximum(z, 0.0) + jnp.log(1.0 + jnp.exp(-jnp.abs(z)))
        log_1m = -sp
        if diag_t is not None:
            mask = (lax.broadcasted_iota(jnp.int32, (bq, bk), 1) + diag_t * bk
                    < lax.broadcasted_iota(jnp.int32, (bq, bk), 0))
            log_1m = jnp.where(mask, log_1m, 0.0)
        hi = log_1m.astype(BF16)
        lo = (log_1m - hi.astype(F32)).astype(BF16)
        after = (jnp.dot(hi, tri, preferred_element_type=F32) + jnp.dot(lo, tri, preferred_element_type=F32)) + r
        a = jnp.exp((z - sp) + after)
        if diag_t is not None:
            a = jnp.where(mask, a, 0.0)
        acc = acc + jnp.dot(a.astype(BF16), v, preferred_element_type=F32)
        r = r + jnp.sum(log_1m, axis=-1, keepdims=True)
        return r, acc

    def block(j, state, diag_t):
        return tuple(head_block(g, j, *state[g], diag_t) for g in range(hps))

    state = tuple((jnp.zeros((bq, 1), F32), jnp.zeros((bq, d), F32)) for _ in range(hps))
    j_diag = qi * n_diag
    for t in reversed(range(n_diag)):
        state = block(j_diag + t, state, t)

    def cond(c):
        j, state = c
        r_max = functools.reduce(jnp.maximum, [jnp.max(r) for r, _ in state])
        return jnp.logical_and(j >= 0, r_max > -F32_EXP_UNDERFLOW)

    def body(c):
        j, state = c
        return j - 1, block(j, state, None)

    _, state = lax.while_loop(cond, body, (j_diag - 1, state))
    for g, (_, acc) in enumerate(state):
        o_ref[:, g * d:(g + 1) * d] = acc.astype(o_ref.dtype)


def stick_breaking_attention(qkv, q_cb, k_cb, v_cb, *, scale, bq=512, bk=256, hps=4):
    s_len = qkv.shape[0]
    tri = jnp.tril(jnp.ones((bk, bk), F32), -1).astype(BF16)
    w = hps * HEAD_DIM
    assert q_cb % hps == 0 and k_cb % hps == 0 and v_cb % hps == 0
    return pl.pallas_call(
        functools.partial(_stick_kernel, bq=bq, bk=bk, scale=scale, hps=hps),
        grid=(N_HEADS // hps, s_len // bq),
        in_specs=[pl.BlockSpec((bq, w), lambda h, i: (i, q_cb // hps + h)),
                  pl.BlockSpec((s_len, w), lambda h, i: (0, k_cb // hps + h)),
                  pl.BlockSpec((s_len, w), lambda h, i: (0, v_cb // hps + h)),
                  pl.BlockSpec((bk, bk), lambda h, i: (0, 0))],
        out_specs=pl.BlockSpec((bq, w), lambda h, i: (i, h)),
        out_shape=jax.ShapeDtypeStruct((s_len, GROUP_WIDTH), BF16),
        compiler_params=_cparams(("parallel", "arbitrary")),
        name="stick_breaking",
    )(qkv, qkv, qkv, tri)


def _prep_mla_weights(w_uq, w_uk, w_uv):
    wq = w_uq.reshape(Q_LORA_RANK, N_HEADS, HEAD_DIM + MLA_ROPE_DIM)
    wq_nope = wq[:, :, :HEAD_DIM].reshape(Q_LORA_RANK, GROUP_WIDTH)
    wq_rope = jnp.pad(wq[:, :, HEAD_DIM:], ((0, 0), (0, 0), (0, HEAD_DIM - MLA_ROPE_DIM))).reshape(Q_LORA_RANK, GROUP_WIDTH)
    w_q = jnp.concatenate([wq_nope, wq_rope], axis=1).astype(BF16)
    w_kv = jnp.concatenate([w_uk, w_uv], axis=1).astype(BF16)
    return w_q, w_kv


def kernel(x, g_attn, w_in, g_q, g_kv, w_uq, w_uk, w_uv, b_f, g_out, w_o, g_mlp, w_up, w_down, g_final):
    b, s_len, d_model = x.shape
    assert b == 1 and s_len % 2048 == 0
    depth = w_in.shape[0]
    xf = x.reshape(s_len, d_model)

    mla_tables = _rope_tables(s_len, MLA_ROPE_DIM, LANES, pass_through=False)
    dil_tables = _rope_tables(s_len, PARTIAL_ROPE_DIM, HEAD_DIM, pass_through=True)
    causal = _causal_tiles(FLASH_BQ, FLASH_BK)
    dil_tiles, dil_blocks = _dilated_tiles(DILATED_BQ, DILATED_BK)
    scale = HEAD_DIM ** -0.5
    mla_scale = (HEAD_DIM + MLA_ROPE_DIM) ** -0.5
    nh = N_HEADS
    n_a = Q_LORA_RANK + KV_LORA_RANK + MLA_ROPE_DIM
    n_a128 = -(-n_a // LANES) * LANES
    gate_col0 = w_in.shape[2] - LANES
    w_in_t = jnp.swapaxes(w_in, 1, 2)

    for l in range(depth):
        w_q, w_kv = _prep_mla_weights(w_uq[l], w_uk[l], w_uv[l])

        h = rmsnorm(xf, g_attn[l], BF16)
        seg_a = matmul_f32w(h, w_in_t, l, 0, n_a128, F32, transposed=True, tn=n_a128 // 2)
        bqk_r = matmul_f32w(h, w_in_t, l, n_a, 2 * GROUP_WIDTH, BF16, transposed=True,
                            rope=(dil_tables, PARTIAL_ROPE_DIM // 2))
        rest = matmul_f32w(h, w_in_t, l, n_a + 2 * GROUP_WIDTH, 7 * GROUP_WIDTH, BF16, transposed=True)
        gate = matmul_f32w(h, w_in_t, l, gate_col0, LANES, F32, transposed=True, tn=LANES)

        cq_n, ckv_n = mla_norms(seg_a, g_q[l], g_kv[l])
        qa = matmul(cq_n, w_q, F32)
        kv = matmul(ckv_n, w_kv, BF16)
        q_cat, k_cat = mla_cat(qa, kv, seg_a, mla_tables)
        o_a = flash_attention(q_cat, 0, k_cat, 0, kv, nh, dqk=2 * HEAD_DIM, scale=mla_scale, tiles=causal, hps=2)

        o_b = flash_attention(bqk_r, 0, bqk_r, nh, rest, 0, dqk=HEAD_DIM, scale=scale, tiles=dil_tiles,
                              window_blocks=dil_blocks, hps=2)

        o_c = stick_breaking_attention(rest, nh, 2 * nh, 3 * nh, scale=scale)

        c_f = forget_cumsum(gate, b_f[l], out_scale=LOG2E)
        o_d = flash_attention(rest, 4 * nh, rest, 5 * nh, rest, 6 * nh, dqk=HEAD_DIM, scale=scale, tiles=causal,
                              decay=c_f, hps=2)

        mix = group_norm_concat(o_a, o_b, o_c, o_d, g_out[l])
        xf = matmul_f32w(mix, w_o, l, 0, d_model, F32, res=xf, tm=512)

        h2 = rmsnorm(xf, g_mlp[l], BF16)
        u, w_down_bf = matmul_f32w(h2, w_up, l, 0, w_up.shape[2], BF16, act="relu2", side_cast=(w_down, l))
        xf = matmul(u, w_down_bf, F32, res=xf)

    return rmsnorm(xf, g_final, F32).reshape(b, s_len, d_model)
```

```python
import functools
import math

import jax
import jax.numpy as jnp
from jax import lax
from jax.experimental import pallas as pl
from jax.experimental.pallas import tpu as pltpu

F32 = jnp.float32
BF16 = jnp.bfloat16

HEAD_DIM = 128
N_HEADS = 8
GROUP_WIDTH = N_HEADS * HEAD_DIM
Q_LORA_RANK = 896
KV_LORA_RANK = 512
MLA_ROPE_DIM = 64
ROPE_THETA = 500000.0
PARTIAL_ROPE_DIM = HEAD_DIM // 4
DILATED_PAIRS = ((128, 1), (512, 4), (2048, 16))
EPS = 1e-6
NEG_INF = -1e30

LANES = 128
VMEM_LIMIT_BYTES = 62 * 1024 * 1024

F32_EXP_UNDERFLOW = 104.0
LOG2E = math.log2(math.e)

FLASH_BQ, FLASH_BK = 1024, 1024
DILATED_BQ, DILATED_BK = 512, 512


def _cparams(semantics):
    return pltpu.CompilerParams(dimension_semantics=semantics, vmem_limit_bytes=VMEM_LIMIT_BYTES)


def _rms(x, g):
    return x * lax.rsqrt(jnp.mean(x * x, axis=-1, keepdims=True) + EPS) * g


def _rmsnorm_kernel(x_ref, g_ref, o_ref):
    o_ref[...] = _rms(x_ref[...], g_ref[...]).astype(o_ref.dtype)


def rmsnorm(x, g, out_dtype, tm=512):
    m, d = x.shape
    return pl.pallas_call(
        _rmsnorm_kernel,
        grid=(m // tm,),
        in_specs=[pl.BlockSpec((tm, d), lambda i: (i, 0)), pl.BlockSpec((1, d), lambda i: (0, 0))],
        out_specs=pl.BlockSpec((tm, d), lambda i: (i, 0)),
        out_shape=jax.ShapeDtypeStruct((m, d), out_dtype),
        compiler_params=_cparams(("parallel",)),
        name="rmsnorm",
    )(x, g.reshape(1, d))


def _group_norm_kernel(a_ref, b_ref, c_ref, d_ref, g_ref, o_ref):
    for gi, r in enumerate((a_ref, b_ref, c_ref, d_ref)):
        sl = slice(gi * GROUP_WIDTH, (gi + 1) * GROUP_WIDTH)
        o_ref[:, sl] = _rms(r[...].astype(F32), g_ref[:, sl]).astype(o_ref.dtype)


def group_norm_concat(o_a, o_b, o_c, o_d, g, tm=256):
    m = o_a.shape[0]
    spec = pl.BlockSpec((tm, GROUP_WIDTH), lambda i: (i, 0))
    return pl.pallas_call(
        _group_norm_kernel,
        grid=(m // tm,),
        in_specs=[spec, spec, spec, spec, pl.BlockSpec((1, 4 * GROUP_WIDTH), lambda i: (0, 0))],
        out_specs=pl.BlockSpec((tm, 4 * GROUP_WIDTH), lambda i: (i, 0)),
        out_shape=jax.ShapeDtypeStruct((m, 4 * GROUP_WIDTH), BF16),
        compiler_params=_cparams(("parallel",)),
        name="group_norm",
    )(o_a, o_b, o_c, o_d, g.reshape(1, -1))


def _mm_kernel(*refs, nk, act, has_res):
    a_ref, w_ref = refs[0], refs[1]
    res_ref = refs[2] if has_res else None
    o_ref = refs[2 + has_res]

    def finish(acc):
        if act == "relu2":
            acc = jnp.square(jnp.maximum(acc, 0.0))
        if has_res:
            acc = res_ref[...] + acc
        o_ref[...] = acc.astype(o_ref.dtype)

    if nk == 1:
        finish(jnp.dot(a_ref[...], w_ref[...], preferred_element_type=F32))
    elif act is None and o_ref.dtype == F32:
        k = pl.program_id(2)

        @pl.when(k == 0)
        def _():
            o_ref[...] = res_ref[...] if has_res else jnp.zeros_like(o_ref)

        o_ref[...] += jnp.dot(a_ref[...], w_ref[...], preferred_element_type=F32)
    else:
        acc_ref = refs[3 + has_res]
        k = pl.program_id(2)

        @pl.when(k == 0)
        def _():
            acc_ref[...] = jnp.zeros_like(acc_ref)

        acc_ref[...] += jnp.dot(a_ref[...], w_ref[...], preferred_element_type=F32)

        @pl.when(k == nk - 1)
        def _():
            finish(acc_ref[...])


def _pick(n, prefs):
    for p in prefs:
        if n % p == 0:
            return p
    return n


def matmul(a, w, out_dtype, act=None, res=None, layer=None, tm=None, tn=None, tk=None, name="matmul"):
    m, kdim = a.shape
    n = w.shape[-1]
    tm = tm or _pick(m, (1024, 512, 256))
    tn = tn or _pick(n, (1024, 768, 512, 256, 128))
    has_res = res is not None
    tk = tk or _pick(kdim, (4096, 2048, 1024))
    nk = kdim // tk
    w_spec = (pl.BlockSpec((tk, tn), lambda i, j, k: (k, j)) if layer is None else
              pl.BlockSpec((None, tk, tn), lambda i, j, k: (layer, k, j)))
    in_specs = [pl.BlockSpec((tm, tk), lambda i, j, k: (i, k)), w_spec]
    args = [a, w]
    if has_res:
        in_specs.append(pl.BlockSpec((tm, tn), lambda i, j, k: (i, j)))
        args.append(res)
    scratch = [pltpu.VMEM((tm, tn), F32)] if (nk > 1 and not (act is None and out_dtype == F32)) else []
    return pl.pallas_call(
        functools.partial(_mm_kernel, nk=nk, act=act, has_res=has_res),
        grid=(m // tm, n // tn, nk),
        in_specs=in_specs,
        out_specs=pl.BlockSpec((tm, tn), lambda i, j, k: (i, j)),
        out_shape=jax.ShapeDtypeStruct((m, n), out_dtype),
        scratch_shapes=scratch,
        compiler_params=_cparams(("parallel", "parallel", "arbitrary")),
        name=name,
    )(*args)


def _mmw_kernel(*refs, layer, col0, tn, nj, transposed, act, has_res, rope_half, has_side, row_chunk):
    a_ref, w_hbm = refs[0], refs[1]
    idx = 2
    res_ref = tab_refs = None
    if has_res:
        res_ref = refs[idx]
        idx += 1
    if rope_half:
        tab_refs = refs[idx:idx + 3]
        idx += 3
    if has_side:
        refs[idx + 2][...] = refs[idx][...].astype(BF16)
        idx += 1
    o_ref = refs[idx]
    wf32_ref, wbf_ref, sem = refs[idx + 1 + has_side:]
    j = pl.program_id(0)

    def w_copy(jt):
        start = pl.multiple_of(col0 + jt * tn, 8 if transposed else LANES)
        src = w_hbm.at[layer, pl.ds(start, tn), :] if transposed else w_hbm.at[layer, :, pl.ds(start, tn)]
        return pltpu.make_async_copy(src, wf32_ref, sem.at[0])

    @pl.when(pl.program_id(1) == 0)
    def _():
        @pl.when(j == 0)
        def _():
            w_copy(0).start()

        w_copy(j).wait()

        def cast_rows(r, c):
            rows = pl.ds(pl.multiple_of(r * row_chunk, row_chunk), row_chunk)
            wbf_ref[rows, :] = wf32_ref[rows, :].astype(BF16)
            return c
        lax.fori_loop(0, wf32_ref.shape[0] // row_chunk, cast_rows, 0)

        @pl.when(j + 1 < nj)
        def _():
            w_copy(j + 1).start()

    dims = (((1,), (1,)), ((), ())) if transposed else (((1,), (0,)), ((), ()))
    acc = lax.dot_general(a_ref[...], wbf_ref[...], dims, preferred_element_type=F32)
    if act == "relu2":
        acc = jnp.square(jnp.maximum(acc, 0.0))
    if rope_half:
        acc = _rope(acc, *(t[...] for t in tab_refs), rope_half)
    if has_res:
        acc = res_ref[...] + acc
    o_ref[...] = acc.astype(o_ref.dtype)


def matmul_f32w(a, w_stack, layer, col0, n, out_dtype, act=None, res=None, rope=None, transposed=False,
                side_cast=None, tm=1024, tn=1024, name="matmul_f32w"):
    m, kdim = a.shape
    has_res = res is not None
    assert col0 % (8 if transposed else LANES) == 0 and n % tn == 0 and m % tm == 0
    w_shape = (tn, kdim) if transposed else (kdim, tn)
    in_specs = [pl.BlockSpec((tm, kdim), lambda j, i: (i, 0)), pl.BlockSpec(memory_space=pl.ANY)]
    args = [a, w_stack]
    if has_res:
        in_specs.append(pl.BlockSpec((tm, tn), lambda j, i: (i, j)))
        args.append(res)
    rope_half = 0
    if rope is not None:
        tables, rope_half = rope
        in_specs += [pl.BlockSpec((tm, tables[0].shape[1]), lambda j, i: (i, 0))] * 3
        args += list(tables)
    out_specs = pl.BlockSpec((tm, tn), lambda j, i: (i, j))
    out_shape = jax.ShapeDtypeStruct((m, n), out_dtype)
    nj, ni = n // tn, m // tm
    if side_cast is not None:
        side_stack, side_layer = side_cast
        _, side_r, side_c = side_stack.shape
        slab = side_r // (nj * ni)
        assert slab * nj * ni == side_r and slab % 16 == 0
        in_specs.append(pl.BlockSpec((None, slab, side_c), lambda j, i: (side_layer, j * ni + i, 0)))
        args.append(side_stack)
        out_specs = [out_specs, pl.BlockSpec((slab, side_c), lambda j, i: (j * ni + i, 0))]
        out_shape = [out_shape, jax.ShapeDtypeStruct((side_r, side_c), BF16)]
    return pl.pallas_call(
        functools.partial(_mmw_kernel, layer=layer, col0=col0, tn=tn, nj=nj, transposed=transposed, act=act,
                          has_res=has_res, rope_half=rope_half, has_side=side_cast is not None,
                          row_chunk=min(256, w_shape[0])),
        grid=(nj, ni),
        in_specs=in_specs,
        out_specs=out_specs,
        out_shape=out_shape,
        scratch_shapes=[pltpu.VMEM(w_shape, F32), pltpu.VMEM(w_shape, BF16), pltpu.SemaphoreType.DMA((1,))],
        compiler_params=_cparams(("arbitrary", "arbitrary")),
        name=name,
    )(*args)


def _rope_tables(s_len, r, period, pass_through):
    half = r // 2
    inv = ROPE_THETA ** (-jnp.arange(half, dtype=F32) * (2.0 / r))
    ang = jnp.arange(s_len, dtype=F32)[:, None] * inv[None, :]
    cos, sin = jnp.cos(ang), jnp.sin(ang)
    fill = jnp.ones if pass_through else jnp.zeros
    zeros = jnp.zeros((s_len, period - r), F32)
    zh = jnp.zeros((s_len, half), F32)
    c = jnp.concatenate([cos, cos, fill((s_len, period - r), F32)], axis=1)
    sa = jnp.concatenate([-sin, zh, zeros], axis=1)
    sb = jnp.concatenate([zh, sin, zeros], axis=1)
    return c, sa, sb


def _rope(x, c, sa, sb, half):
    w = x.shape[-1]
    reps = w // c.shape[-1]
    if reps > 1:
        c, sa, sb = (jnp.tile(t, (1, reps)) for t in (c, sa, sb))
    return x * c + pltpu.roll(x, w - half, 1) * sa + pltpu.roll(x, half, 1) * sb


def _mla_proj_kernel(seg_ref, g_ref, w_ref, c_ref, sa_ref, sb_ref, o_ref, a_ref, *, col0, width, kr_col0,
                     n_kr_tiles):
    j = pl.program_id(1)
    half = MLA_ROPE_DIM // 2
    group = 2 * HEAD_DIM

    @pl.when(j == 0)
    def _():
        a_ref[...] = _rms(seg_ref[:, col0:col0 + width], g_ref[...]).astype(BF16)

    acc = jnp.dot(a_ref[...], w_ref[...], preferred_element_type=F32)
    c, sa, sb = c_ref[...], sa_ref[...], sb_ref[...]
    if n_kr_tiles is None:
        for lo in range(0, acc.shape[1], group):
            o_ref[:, lo:lo + HEAD_DIM] = acc[:, lo:lo + HEAD_DIM].astype(o_ref.dtype)
            o_ref[:, lo + HEAD_DIM:lo + group] = _rope(acc[:, lo + HEAD_DIM:lo + group], c, sa, sb, half).astype(o_ref.dtype)
    else:
        kr = _rope(seg_ref[:, kr_col0:kr_col0 + LANES], c, sa, sb, half)
        kr = jnp.concatenate([jnp.zeros_like(kr), kr], axis=1)
        keep = (j < n_kr_tiles).astype(F32)
        acc = acc + jnp.tile(kr, (1, acc.shape[1] // group)) * keep
        o_ref[...] = acc.astype(o_ref.dtype)


def mla_proj(seg_a, col0, g, w, tables, n_kr_tiles=None, tm=1024, tn=1024):
    m, seg_w = seg_a.shape
    width, n = w.shape
    tspec = pl.BlockSpec((tm, tables[0].shape[1]), lambda i, j: (i, 0))
    return pl.pallas_call(
        functools.partial(_mla_proj_kernel, col0=col0, width=width, kr_col0=Q_LORA_RANK + KV_LORA_RANK,
                          n_kr_tiles=n_kr_tiles),
        grid=(m // tm, n // tn),
        in_specs=[pl.BlockSpec((tm, seg_w), lambda i, j: (i, 0)),
                  pl.BlockSpec((1, width), lambda i, j: (0, 0)),
                  pl.BlockSpec((width, tn), lambda i, j: (0, j)),
                  tspec, tspec, tspec],
        out_specs=pl.BlockSpec((tm, tn), lambda i, j: (i, j)),
        out_shape=jax.ShapeDtypeStruct((m, n), BF16),
        scratch_shapes=[pltpu.VMEM((tm, width), BF16)],
        compiler_params=_cparams(("parallel", "arbitrary")),
        name="mla_proj",
    )(seg_a, g.reshape(1, -1), w, *tables)


def _split3(x):
    hi = x.astype(BF16)
    r1 = x - hi.astype(F32)
    mid = r1.astype(BF16)
    lo = (r1 - mid.astype(F32)).astype(BF16)
    return hi, mid, lo


def _forget_kernel(x_ref, b_ref, u_ref, o_ref, carry_ref, *, out_scale):
    @pl.when(pl.program_id(0) == 0)
    def _():
        carry_ref[...] = jnp.zeros_like(carry_ref)

    xt = jnp.transpose(x_ref[...])
    y = -(xt[LANES - N_HEADS:, :] + b_ref[...])
    logf = -(jnp.maximum(y, 0.0) + jnp.log(1.0 + jnp.exp(-jnp.abs(y))))
    u = u_ref[...]
    cs = sum(jnp.dot(p, u, preferred_element_type=F32) for p in _split3(logf))
    c = cs + carry_ref[:, 0:1]
    o_ref[...] = c * out_scale
    carry_ref[...] = jnp.broadcast_to(c[:, -1:], carry_ref.shape)


def forget_cumsum(gate, b_f, tc=512, out_scale=1.0):
    m = gate.shape[0]
    u = jnp.triu(jnp.ones((tc, tc), F32)).astype(BF16)
    return pl.pallas_call(
        functools.partial(_forget_kernel, out_scale=out_scale),
        grid=(m // tc,),
        in_specs=[pl.BlockSpec((tc, LANES), lambda i: (i, 0)),
                  pl.BlockSpec((N_HEADS, 1), lambda i: (0, 0)),
                  pl.BlockSpec((tc, tc), lambda i: (0, 0))],
        out_specs=pl.BlockSpec((N_HEADS, tc), lambda i: (0, i)),
        out_shape=jax.ShapeDtypeStruct((N_HEADS, m), F32),
        scratch_shapes=[pltpu.VMEM((N_HEADS, LANES), F32)],
        compiler_params=_cparams(("arbitrary",)),
        name="forget_cumsum",
    )(gate, b_f.reshape(N_HEADS, 1), u)


def _flash_kernel(*refs, bq, bk, dqk, scale2, window_blocks, has_decay, hps):
    q_ref, k_ref, v_ref, tile_ref = refs[:4]
    ck_ref = refs[4] if has_decay else None
    o_ref = refs[4 + has_decay]
    dv = HEAD_DIM
    n_diag = bq // bk
    qi = pl.program_id(1)
    qs = [q_ref[:, g * dqk:(g + 1) * dqk] for g in range(hps)]

    def head_step(g, j, carry, tile):
        m, l, acc = carry
        start = pl.multiple_of(j * bk, bk)
        k = k_ref[pl.ds(start, bk), g * dqk:(g + 1) * dqk]
        v = v_ref[pl.ds(start, bk), g * dv:(g + 1) * dv]
        s = lax.dot_general(qs[g], k, (((1,), (1,)), ((), ())), preferred_element_type=F32) * scale2
        if has_decay:
            s = s - ck_ref[g, pl.ds(j, 1), :]
        if tile is not None:
            s = s + tile
        m_new = jnp.maximum(m, jnp.max(s, axis=-1, keepdims=True))
        alpha = jnp.exp2(m - m_new)
        p = jnp.exp2(s - m_new)
        l = alpha * l + jnp.sum(p, axis=-1, keepdims=True)
        acc = alpha * acc + jnp.dot(p.astype(BF16), v, preferred_element_type=F32)
        return m_new, l, acc

    def step(j, carry, tile):
        return tuple(head_step(g, j, carry[g], tile) for g in range(hps))

    carry = tuple((jnp.full((bq, 1), NEG_INF, F32), jnp.zeros((bq, 1), F32), jnp.zeros((bq, dv), F32))
                  for _ in range(hps))
    j_diag = qi * n_diag
    if window_blocks is None:
        lo = 0
        body = lambda j, c: step(j, c, None)
    else:
        lo = jnp.maximum(j_diag - window_blocks, 0)
        body = lambda j, c: step(j, c, tile_ref[n_diag - 1 + j_diag - j])
    carry = lax.fori_loop(lo, j_diag, body, carry)
    for t in range(n_diag):
        carry = step(j_diag + t, carry, tile_ref[n_diag - 1 - t])
    for g, (m, l, acc) in enumerate(carry):
        o_ref[:, g * dv:(g + 1) * dv] = (acc / l).astype(o_ref.dtype)


def flash_attention(q_arr, q_cb, k_arr, k_cb, v_arr, v_cb, *, dqk, scale, tiles, window_blocks=None,
                    decay=None, hps=1):
    s_len = q_arr.shape[0]
    _, bq, bk = tiles.shape
    nq, nk = s_len // bq, s_len // bk
    assert q_cb % hps == 0 and k_cb % hps == 0 and v_cb % hps == 0
    in_specs = [pl.BlockSpec((bq, hps * dqk), lambda h, i: (i, q_cb // hps + h)),
                pl.BlockSpec((s_len, hps * dqk), lambda h, i: (0, k_cb // hps + h)),
                pl.BlockSpec((s_len, hps * HEAD_DIM), lambda h, i: (0, v_cb // hps + h)),
                pl.BlockSpec(tiles.shape, lambda h, i: (0, 0, 0))]
    args = [q_arr, k_arr, v_arr, tiles]
    if decay is not None:
        in_specs.append(pl.BlockSpec((hps, nk, bk), lambda h, i: (h, 0, 0)))
        args.append(decay.reshape(N_HEADS, nk, bk))
    return pl.pallas_call(
        functools.partial(_flash_kernel, bq=bq, bk=bk, dqk=dqk, scale2=scale * LOG2E,
                          window_blocks=window_blocks, has_decay=decay is not None, hps=hps),
        grid=(N_HEADS // hps, nq),
        in_specs=in_specs,
        out_specs=pl.BlockSpec((bq, hps * HEAD_DIM), lambda h, i: (i, h)),
        out_shape=jax.ShapeDtypeStruct((s_len, GROUP_WIDTH), BF16),
        compiler_params=_cparams(("parallel", "arbitrary")),
        name="flash_attention",
    )(*args)


def _score_tiles(bq, bk, count_fn, window):
    n_diag = bq // bk
    window_blocks = 0 if window is None else (window + bk - 1) // bk
    rel = jnp.arange(-(n_diag - 1), window_blocks + 1)
    d = rel[:, None, None] * bk + jnp.arange(bq)[None, :, None] - jnp.arange(bk)[None, None, :]
    count = jnp.where(d >= 0, count_fn(d), 0.0)
    tiles = jnp.where(count > 0, jnp.log2(jnp.maximum(count, 1.0)), NEG_INF).astype(F32)
    return tiles, (None if window is None else window_blocks)


def _causal_tiles(bq, bk):
    return _score_tiles(bq, bk, lambda d: jnp.ones(d.shape, F32), None)[0]


def _dilated_tiles(bq, bk):
    count_fn = lambda d: sum(((d <= w) & (d % dil == 0)).astype(F32) for w, dil in DILATED_PAIRS)
    return _score_tiles(bq, bk, count_fn, max(w for w, _ in DILATED_PAIRS))


def _stick_kernel(q_ref, k_ref, v_ref, t_ref, o_ref, *, bq, bk, scale, hps):
    qi = pl.program_id(1)
    d = HEAD_DIM
    n_diag = bq // bk
    qs = [q_ref[:, g * d:(g + 1) * d] for g in range(hps)]
    tri = t_ref[...]

    def head_block(g, j, r, acc, diag_t):
        start = pl.multiple_of(j * bk, bk)
        k = k_ref[pl.ds(start, bk), g * d:(g + 1) * d]
        v = v_ref[pl.ds(start, bk), g * d:(g + 1) * d]
        z = lax.dot_general(qs[g], k, (((1,), (1,)), ((), ())), preferred_element_type=F32) * scale
        sp = jnp.maximum(z, 0.0) + jnp.log(1.0 + jnp.exp(-jnp.abs(z)))
        log_1m = -sp
        if diag_t is not None:
            mask = (lax.broadcasted_iota(jnp.int32, (bq, bk), 1) + diag_t * bk
                    < lax.broadcasted_iota(jnp.int32, (bq, bk), 0))
            log_1m = jnp.where(mask, log_1m, 0.0)
        hi = log_1m.astype(BF16)
        lo = (log_1m - hi.astype(F32)).astype(BF16)
        after = (jnp.dot(hi, tri, preferred_element_type=F32) + jnp.dot(lo, tri, preferred_element_type=F32)) + r
        a = jnp.exp((z - sp) + after)
        if diag_t is not None:
            a = jnp.where(mask, a, 0.0)
        acc = acc + jnp.dot(a.astype(BF16), v, preferred_element_type=F32)
        r = r + jnp.sum(log_1m, axis=-1, keepdims=True)
        return r, acc

    def block(j, state, diag_t):
        return tuple(head_block(g, j, *state[g], diag_t) for g in range(hps))

    state = tuple((jnp.zeros((bq, 1), F32), jnp.zeros((bq, d), F32)) for _ in range(hps))
    j_diag = qi * n_diag
    for t in reversed(range(n_diag)):
        state = block(j_diag + t, state, t)

    def cond(c):
        j, state = c
        r_max = functools.reduce(jnp.maximum, [jnp.max(r) for r, _ in state])
        return jnp.logical_and(j >= 0, r_max > -F32_EXP_UNDERFLOW)

    def body(c):
        j, state = c
        return j - 1, block(j, state, None)

    _, state = lax.while_loop(cond, body, (j_diag - 1, state))
    for g, (_, acc) in enumerate(state):
        o_ref[:, g * d:(g + 1) * d] = acc.astype(o_ref.dtype)


def stick_breaking_attention(qkv, q_cb, k_cb, v_cb, *, scale, bq=512, bk=256, hps=4):
    s_len = qkv.shape[0]
    tri = jnp.tril(jnp.ones((bk, bk), F32), -1).astype(BF16)
    w = hps * HEAD_DIM
    assert q_cb % hps == 0 and k_cb % hps == 0 and v_cb % hps == 0
    return pl.pallas_call(
        functools.partial(_stick_kernel, bq=bq, bk=bk, scale=scale, hps=hps),
        grid=(N_HEADS // hps, s_len // bq),
        in_specs=[pl.BlockSpec((bq, w), lambda h, i: (i, q_cb // hps + h)),
                  pl.BlockSpec((s_len, w), lambda h, i: (0, k_cb // hps + h)),
                  pl.BlockSpec((s_len, w), lambda h, i: (0, v_cb // hps + h)),
                  pl.BlockSpec((bk, bk), lambda h, i: (0, 0))],
        out_specs=pl.BlockSpec((bq, w), lambda h, i: (i, h)),
        out_shape=jax.ShapeDtypeStruct((s_len, GROUP_WIDTH), BF16),
        compiler_params=_cparams(("parallel", "arbitrary")),
        name="stick_breaking",
    )(qkv, qkv, qkv, tri)


def _prep_mla_weights(w_uq, w_uk, w_uv):
    pad = 2 * HEAD_DIM
    wq = w_uq.reshape(Q_LORA_RANK, N_HEADS, HEAD_DIM + MLA_ROPE_DIM)
    w_q = jnp.pad(wq, ((0, 0), (0, 0), (0, pad - wq.shape[2]))).reshape(Q_LORA_RANK, N_HEADS * pad)
    wk = w_uk.reshape(KV_LORA_RANK, N_HEADS, HEAD_DIM)
    w_k = jnp.pad(wk, ((0, 0), (0, 0), (0, pad - HEAD_DIM))).reshape(KV_LORA_RANK, N_HEADS * pad)
    return w_q.astype(BF16), jnp.concatenate([w_k, w_uv], axis=1).astype(BF16)


def kernel(x, g_attn, w_in, g_q, g_kv, w_uq, w_uk, w_uv, b_f, g_out, w_o, g_mlp, w_up, w_down, g_final):
    b, s_len, d_model = x.shape
    assert b == 1 and s_len % 2048 == 0
    depth = w_in.shape[0]
    xf = x.reshape(s_len, d_model)

    mla_tables = _rope_tables(s_len, MLA_ROPE_DIM, LANES, pass_through=False)
    dil_tables = _rope_tables(s_len, PARTIAL_ROPE_DIM, HEAD_DIM, pass_through=True)
    causal = _causal_tiles(FLASH_BQ, FLASH_BK)
    dil_tiles, dil_blocks = _dilated_tiles(DILATED_BQ, DILATED_BK)
    scale = HEAD_DIM ** -0.5
    mla_scale = (HEAD_DIM + MLA_ROPE_DIM) ** -0.5
    nh = N_HEADS
    n_a = Q_LORA_RANK + KV_LORA_RANK + MLA_ROPE_DIM
    n_a128 = -(-n_a // LANES) * LANES
    gate_col0 = w_in.shape[2] - LANES
    w_in_t = jnp.swapaxes(w_in, 1, 2)

    for l in range(depth):
        w_q, w_kv = _prep_mla_weights(w_uq[l], w_uk[l], w_uv[l])

        h = rmsnorm(xf, g_attn[l], BF16)
        seg_a = matmul_f32w(h, w_in_t, l, 0, n_a128, F32, transposed=True, tn=n_a128 // 2)
        bqk_r = matmul_f32w(h, w_in_t, l, n_a, 2 * GROUP_WIDTH, BF16, transposed=True,
                            rope=(dil_tables, PARTIAL_ROPE_DIM // 2))
        rest = matmul_f32w(h, w_in_t, l, n_a + 2 * GROUP_WIDTH, 7 * GROUP_WIDTH, BF16, transposed=True)
        gate = matmul_f32w(h, w_in_t, l, gate_col0, LANES, F32, transposed=True, tn=LANES)

        q_cat = mla_proj(seg_a, 0, g_q[l], w_q, mla_tables)
        kv_cat = mla_proj(seg_a, Q_LORA_RANK, g_kv[l], w_kv, mla_tables,
                          n_kr_tiles=N_HEADS * 2 * HEAD_DIM // 1024)
        o_a = flash_attention(q_cat, 0, kv_cat, 0, kv_cat, 2 * nh, dqk=2 * HEAD_DIM, scale=mla_scale, tiles=causal,
                              hps=2)

        o_b = flash_attention(bqk_r, 0, bqk_r, nh, rest, 0, dqk=HEAD_DIM, scale=scale, tiles=dil_tiles,
                              window_blocks=dil_blocks, hps=2)

        o_c = stick_breaking_attention(rest, nh, 2 * nh, 3 * nh, scale=scale)

        c_f = forget_cumsum(gate, b_f[l], out_scale=LOG2E)
        o_d = flash_attention(rest, 4 * nh, rest, 5 * nh, rest, 6 * nh, dqk=HEAD_DIM, scale=scale, tiles=causal,
                              decay=c_f, hps=2)

        mix = group_norm_concat(o_a, o_b, o_c, o_d, g_out[l])
        xf = matmul_f32w(mix, w_o, l, 0, d_model, F32, res=xf, tm=512)

        h2 = rmsnorm(xf, g_mlp[l], BF16)
        u, w_down_bf = matmul_f32w(h2, w_up, l, 0, w_up.shape[2], BF16, act="relu2", side_cast=(w_down, l))
        xf = matmul(u, w_down_bf, F32, res=xf)

    return rmsnorm(xf, g_final, F32).reshape(b, s_len, d_model)
```

```python
import functools
import math

import jax
import jax.numpy as jnp
from jax import lax
from jax.experimental import pallas as pl
from jax.experimental.pallas import tpu as pltpu

F32 = jnp.float32
BF16 = jnp.bfloat16

HEAD_DIM = 128
N_HEADS = 8
GROUP_WIDTH = N_HEADS * HEAD_DIM
Q_LORA_RANK = 896
KV_LORA_RANK = 512
MLA_ROPE_DIM = 64
ROPE_THETA = 500000.0
PARTIAL_ROPE_DIM = HEAD_DIM // 4
DILATED_PAIRS = ((128, 1), (512, 4), (2048, 16))
EPS = 1e-6
NEG_INF = -1e30

LANES = 128
VMEM_LIMIT_BYTES = 62 * 1024 * 1024

F32_EXP_UNDERFLOW = 104.0
LOG2E = math.log2(math.e)

FLASH_BQ, FLASH_BK = 1024, 1024
DILATED_BQ, DILATED_BK = 512, 512


def _cparams(semantics):
    return pltpu.CompilerParams(dimension_semantics=semantics, vmem_limit_bytes=VMEM_LIMIT_BYTES)


def _rms(x, g):
    return x * lax.rsqrt(jnp.mean(x * x, axis=-1, keepdims=True) + EPS) * g


def _rmsnorm_kernel(x_ref, g_ref, o_ref):
    o_ref[...] = _rms(x_ref[...], g_ref[...]).astype(o_ref.dtype)


def rmsnorm(x, g, out_dtype, tm=512):
    m, d = x.shape
    return pl.pallas_call(
        _rmsnorm_kernel,
        grid=(m // tm,),
        in_specs=[pl.BlockSpec((tm, d), lambda i: (i, 0)), pl.BlockSpec((1, d), lambda i: (0, 0))],
        out_specs=pl.BlockSpec((tm, d), lambda i: (i, 0)),
        out_shape=jax.ShapeDtypeStruct((m, d), out_dtype),
        compiler_params=_cparams(("parallel",)),
        name="rmsnorm",
    )(x, g.reshape(1, d))


def _group_norm_kernel(a_ref, b_ref, c_ref, d_ref, g_ref, o_ref):
    for gi, r in enumerate((a_ref, b_ref, c_ref, d_ref)):
        sl = slice(gi * GROUP_WIDTH, (gi + 1) * GROUP_WIDTH)
        o_ref[:, sl] = _rms(r[...].astype(F32), g_ref[:, sl]).astype(o_ref.dtype)


def group_norm_concat(o_a, o_b, o_c, o_d, g, tm=512):
    m = o_a.shape[0]
    spec = pl.BlockSpec((tm, GROUP_WIDTH), lambda i: (i, 0))
    return pl.pallas_call(
        _group_norm_kernel,
        grid=(m // tm,),
        in_specs=[spec, spec, spec, spec, pl.BlockSpec((1, 4 * GROUP_WIDTH), lambda i: (0, 0))],
        out_specs=pl.BlockSpec((tm, 4 * GROUP_WIDTH), lambda i: (i, 0)),
        out_shape=jax.ShapeDtypeStruct((m, 4 * GROUP_WIDTH), BF16),
        compiler_params=_cparams(("parallel",)),
        name="group_norm",
    )(o_a, o_b, o_c, o_d, g.reshape(1, -1))


def _mm_kernel(*refs, nk, act, has_res):
    a_ref, w_ref = refs[0], refs[1]
    res_ref = refs[2] if has_res else None
    o_ref = refs[2 + has_res]

    def finish(acc):
        if act == "relu2":
            acc = jnp.square(jnp.maximum(acc, 0.0))
        if has_res:
            acc = res_ref[...] + acc
        o_ref[...] = acc.astype(o_ref.dtype)

    if nk == 1:
        finish(jnp.dot(a_ref[...], w_ref[...], preferred_element_type=F32))
    elif act is None and o_ref.dtype == F32:
        k = pl.program_id(2)

        @pl.when(k == 0)
        def _():
            o_ref[...] = res_ref[...] if has_res else jnp.zeros_like(o_ref)

        o_ref[...] += jnp.dot(a_ref[...], w_ref[...], preferred_element_type=F32)
    else:
        acc_ref = refs[3 + has_res]
        k = pl.program_id(2)

        @pl.when(k == 0)
        def _():
            acc_ref[...] = jnp.zeros_like(acc_ref)

        acc_ref[...] += jnp.dot(a_ref[...], w_ref[...], preferred_element_type=F32)

        @pl.when(k == nk - 1)
        def _():
            finish(acc_ref[...])


def _pick(n, prefs):
    for p in prefs:
        if n % p == 0:
            return p
    return n


def matmul(a, w, out_dtype, act=None, res=None, layer=None, tm=None, tn=None, tk=None, name="matmul"):
    m, kdim = a.shape
    n = w.shape[-1]
    tm = tm or _pick(m, (1024, 512, 256))
    tn = tn or _pick(n, (1024, 768, 512, 256, 128))
    has_res = res is not None
    tk = tk or _pick(kdim, (4096, 2048, 1024))
    nk = kdim // tk
    w_spec = (pl.BlockSpec((tk, tn), lambda i, j, k: (k, j)) if layer is None else
              pl.BlockSpec((None, tk, tn), lambda i, j, k: (layer, k, j)))
    in_specs = [pl.BlockSpec((tm, tk), lambda i, j, k: (i, k)), w_spec]
    args = [a, w]
    if has_res:
        in_specs.append(pl.BlockSpec((tm, tn), lambda i, j, k: (i, j)))
        args.append(res)
    scratch = [pltpu.VMEM((tm, tn), F32)] if (nk > 1 and not (act is None and out_dtype == F32)) else []
    return pl.pallas_call(
        functools.partial(_mm_kernel, nk=nk, act=act, has_res=has_res),
        grid=(m // tm, n // tn, nk),
        in_specs=in_specs,
        out_specs=pl.BlockSpec((tm, tn), lambda i, j, k: (i, j)),
        out_shape=jax.ShapeDtypeStruct((m, n), out_dtype),
        scratch_shapes=scratch,
        compiler_params=_cparams(("parallel", "parallel", "arbitrary")),
        name=name,
    )(*args)


def _mmw_kernel(*refs, layer, col0, tn, nj, transposed, act, has_res, rope_half, has_side, row_chunk):
    a_ref, w_hbm = refs[0], refs[1]
    idx = 2
    res_ref = tab_refs = None
    if has_res:
        res_ref = refs[idx]
        idx += 1
    if rope_half:
        tab_refs = refs[idx:idx + 3]
        idx += 3
    if has_side:
        refs[idx + 2][...] = refs[idx][...].astype(BF16)
        idx += 1
    o_ref = refs[idx]
    wf32_ref, wbf_ref, sem = refs[idx + 1 + has_side:]
    j = pl.program_id(0)

    def w_copy(jt):
        start = pl.multiple_of(col0 + jt * tn, 8 if transposed else LANES)
        src = w_hbm.at[layer, pl.ds(start, tn), :] if transposed else w_hbm.at[layer, :, pl.ds(start, tn)]
        return pltpu.make_async_copy(src, wf32_ref, sem.at[0])

    @pl.when(pl.program_id(1) == 0)
    def _():
        @pl.when(j == 0)
        def _():
            w_copy(0).start()

        w_copy(j).wait()

        def cast_rows(r, c):
            rows = pl.ds(pl.multiple_of(r * row_chunk, row_chunk), row_chunk)
            wbf_ref[rows, :] = wf32_ref[rows, :].astype(BF16)
            return c
        lax.fori_loop(0, wf32_ref.shape[0] // row_chunk, cast_rows, 0)

        @pl.when(j + 1 < nj)
        def _():
            w_copy(j + 1).start()

    dims = (((1,), (1,)), ((), ())) if transposed else (((1,), (0,)), ((), ()))
    acc = lax.dot_general(a_ref[...], wbf_ref[...], dims, preferred_element_type=F32)
    if act == "relu2":
        acc = jnp.square(jnp.maximum(acc, 0.0))
    if rope_half:
        acc = _rope(acc, *(t[...] for t in tab_refs), rope_half)
    if has_res:
        acc = res_ref[...] + acc
    o_ref[...] = acc.astype(o_ref.dtype)


def matmul_f32w(a, w_stack, layer, col0, n, out_dtype, act=None, res=None, rope=None, transposed=False,
                side_cast=None, tm=1024, tn=1024, name="matmul_f32w"):
    m, kdim = a.shape
    has_res = res is not None
    assert col0 % (8 if transposed else LANES) == 0 and n % tn == 0 and m % tm == 0
    w_shape = (tn, kdim) if transposed else (kdim, tn)
    in_specs = [pl.BlockSpec((tm, kdim), lambda j, i: (i, 0)), pl.BlockSpec(memory_space=pl.ANY)]
    args = [a, w_stack]
    if has_res:
        in_specs.append(pl.BlockSpec((tm, tn), lambda j, i: (i, j)))
        args.append(res)
    rope_half = 0
    if rope is not None:
        tables, rope_half = rope
        in_specs += [pl.BlockSpec((tm, tables[0].shape[1]), lambda j, i: (i, 0))] * 3
        args += list(tables)
    out_specs = pl.BlockSpec((tm, tn), lambda j, i: (i, j))
    out_shape = jax.ShapeDtypeStruct((m, n), out_dtype)
    nj, ni = n // tn, m // tm
    if side_cast is not None:
        side_stack, side_layer = side_cast
        _, side_r, side_c = side_stack.shape
        slab = side_r // (nj * ni)
        assert slab * nj * ni == side_r and slab % 16 == 0
        in_specs.append(pl.BlockSpec((None, slab, side_c), lambda j, i: (side_layer, j * ni + i, 0)))
        args.append(side_stack)
        out_specs = [out_specs, pl.BlockSpec((slab, side_c), lambda j, i: (j * ni + i, 0))]
        out_shape = [out_shape, jax.ShapeDtypeStruct((side_r, side_c), BF16)]
    return pl.pallas_call(
        functools.partial(_mmw_kernel, layer=layer, col0=col0, tn=tn, nj=nj, transposed=transposed, act=act,
                          has_res=has_res, rope_half=rope_half, has_side=side_cast is not None,
                          row_chunk=min(256, w_shape[0])),
        grid=(nj, ni),
        in_specs=in_specs,
        out_specs=out_specs,
        out_shape=out_shape,
        scratch_shapes=[pltpu.VMEM(w_shape, F32), pltpu.VMEM(w_shape, BF16), pltpu.SemaphoreType.DMA((1,))],
        compiler_params=_cparams(("arbitrary", "arbitrary")),
        name=name,
    )(*args)


def _rope_tables(s_len, r, period, pass_through):
    half = r // 2
    inv = ROPE_THETA ** (-jnp.arange(half, dtype=F32) * (2.0 / r))
    ang = jnp.arange(s_len, dtype=F32)[:, None] * inv[None, :]
    cos, sin = jnp.cos(ang), jnp.sin(ang)
    fill = jnp.ones if pass_through else jnp.zeros
    zeros = jnp.zeros((s_len, period - r), F32)
    zh = jnp.zeros((s_len, half), F32)
    c = jnp.concatenate([cos, cos, fill((s_len, period - r), F32)], axis=1)
    sa = jnp.concatenate([-sin, zh, zeros], axis=1)
    sb = jnp.concatenate([zh, sin, zeros], axis=1)
    return c, sa, sb


def _rope(x, c, sa, sb, half):
    w = x.shape[-1]
    reps = w // c.shape[-1]
    if reps > 1:
        c, sa, sb = (jnp.tile(t, (1, reps)) for t in (c, sa, sb))
    return x * c + pltpu.roll(x, w - half, 1) * sa + pltpu.roll(x, half, 1) * sb


def _mla_proj_kernel(seg_ref, g_ref, w_ref, c_ref, sa_ref, sb_ref, o_ref, *, col0, width, kr_col0, kr_cols):
    half = MLA_ROPE_DIM // 2
    group = 2 * HEAD_DIM
    a = _rms(seg_ref[:, col0:col0 + width], g_ref[...]).astype(BF16)
    acc = jnp.dot(a, w_ref[...], preferred_element_type=F32)
    c, sa, sb = c_ref[...], sa_ref[...], sb_ref[...]
    if kr_cols is None:
        for lo in range(0, acc.shape[1], group):
            o_ref[:, lo:lo + HEAD_DIM] = acc[:, lo:lo + HEAD_DIM].astype(o_ref.dtype)
            o_ref[:, lo + HEAD_DIM:lo + group] = _rope(acc[:, lo + HEAD_DIM:lo + group], c, sa, sb, half).astype(o_ref.dtype)
    else:
        kr = _rope(seg_ref[:, kr_col0:kr_col0 + LANES], c, sa, sb, half)
        for lo in range(0, kr_cols, group):
            o_ref[:, lo:lo + HEAD_DIM] = acc[:, lo:lo + HEAD_DIM].astype(o_ref.dtype)
            o_ref[:, lo + HEAD_DIM:lo + group] = (acc[:, lo + HEAD_DIM:lo + group] + kr).astype(o_ref.dtype)
        o_ref[:, kr_cols:] = acc[:, kr_cols:].astype(o_ref.dtype)


def mla_proj(seg_a, col0, g, w, tables, kr_cols=None, tm=1024):
    m, seg_w = seg_a.shape
    width, n = w.shape
    tspec = pl.BlockSpec((tm, tables[0].shape[1]), lambda i: (i, 0))
    return pl.pallas_call(
        functools.partial(_mla_proj_kernel, col0=col0, width=width, kr_col0=Q_LORA_RANK + KV_LORA_RANK,
                          kr_cols=kr_cols),
        grid=(m // tm,),
        in_specs=[pl.BlockSpec((tm, seg_w), lambda i: (i, 0)),
                  pl.BlockSpec((1, width), lambda i: (0, 0)),
                  pl.BlockSpec((width, n), lambda i: (0, 0)),
                  tspec, tspec, tspec],
        out_specs=pl.BlockSpec((tm, n), lambda i: (i, 0)),
        out_shape=jax.ShapeDtypeStruct((m, n), BF16),
        compiler_params=_cparams(("parallel",)),
        name="mla_proj",
    )(seg_a, g.reshape(1, -1), w, *tables)


def _split3(x):
    hi = x.astype(BF16)
    r1 = x - hi.astype(F32)
    mid = r1.astype(BF16)
    lo = (r1 - mid.astype(F32)).astype(BF16)
    return hi, mid, lo


def _forget_kernel(x_ref, b_ref, u_ref, o_ref, carry_ref, *, out_scale):
    @pl.when(pl.program_id(0) == 0)
    def _():
        carry_ref[...] = jnp.zeros_like(carry_ref)

    xt = jnp.transpose(x_ref[...])
    y = -(xt[LANES - N_HEADS:, :] + b_ref[...])
    logf = -(jnp.maximum(y, 0.0) + jnp.log(1.0 + jnp.exp(-jnp.abs(y))))
    u = u_ref[...]
    cs = sum(jnp.dot(p, u, preferred_element_type=F32) for p in _split3(logf))
    c = cs + carry_ref[:, 0:1]
    o_ref[...] = c * out_scale
    carry_ref[...] = jnp.broadcast_to(c[:, -1:], carry_ref.shape)


def forget_cumsum(gate, b_f, tc=512, out_scale=1.0):
    m = gate.shape[0]
    u = jnp.triu(jnp.ones((tc, tc), F32)).astype(BF16)
    return pl.pallas_call(
        functools.partial(_forget_kernel, out_scale=out_scale),
        grid=(m // tc,),
        in_specs=[pl.BlockSpec((tc, LANES), lambda i: (i, 0)),
                  pl.BlockSpec((N_HEADS, 1), lambda i: (0, 0)),
                  pl.BlockSpec((tc, tc), lambda i: (0, 0))],
        out_specs=pl.BlockSpec((N_HEADS, tc), lambda i: (0, i)),
        out_shape=jax.ShapeDtypeStruct((N_HEADS, m), F32),
        scratch_shapes=[pltpu.VMEM((N_HEADS, LANES), F32)],
        compiler_params=_cparams(("arbitrary",)),
        name="forget_cumsum",
    )(gate, b_f.reshape(N_HEADS, 1), u)


def _flash_kernel(*refs, bq, bk, dqk, scale2, window_blocks, has_decay, hps):
    q_ref, k_ref, v_ref, tile_ref = refs[:4]
    ck_ref = refs[4] if has_decay else None
    o_ref = refs[4 + has_decay]
    dv = HEAD_DIM
    n_diag = bq // bk
    qi = pl.program_id(1)
    qs = [q_ref[:, g * dqk:(g + 1) * dqk] for g in range(hps)]

    def head_step(g, j, carry, tile):
        m, l, acc = carry
        start = pl.multiple_of(j * bk, bk)
        k = k_ref[pl.ds(start, bk), g * dqk:(g + 1) * dqk]
        v = v_ref[pl.ds(start, bk), g * dv:(g + 1) * dv]
        s = lax.dot_general(qs[g], k, (((1,), (1,)), ((), ())), preferred_element_type=F32) * scale2
        if has_decay:
            s = s - ck_ref[g, pl.ds(j, 1), :]
        if tile is not None:
            s = s + tile
        m_new = jnp.maximum(m, jnp.max(s, axis=-1, keepdims=True))
        alpha = jnp.exp2(m - m_new)
        p = jnp.exp2(s - m_new)
        l = alpha * l + jnp.sum(p, axis=-1, keepdims=True)
        acc = alpha * acc + jnp.dot(p.astype(BF16), v, preferred_element_type=F32)
        return m_new, l, acc

    def step(j, carry, tile):
        return tuple(head_step(g, j, carry[g], tile) for g in range(hps))

    carry = tuple((jnp.full((bq, 1), NEG_INF, F32), jnp.zeros((bq, 1), F32), jnp.zeros((bq, dv), F32))
                  for _ in range(hps))
    j_diag = qi * n_diag
    if window_blocks is None:
        lo = 0
        body = lambda j, c: step(j, c, None)
    else:
        lo = jnp.maximum(j_diag - window_blocks, 0)
        body = lambda j, c: step(j, c, tile_ref[n_diag - 1 + j_diag - j])
    carry = lax.fori_loop(lo, j_diag, body, carry)
    for t in range(n_diag):
        carry = step(j_diag + t, carry, tile_ref[n_diag - 1 - t])
    for g, (m, l, acc) in enumerate(carry):
        o_ref[:, g * dv:(g + 1) * dv] = (acc / l).astype(o_ref.dtype)


def flash_attention(q_arr, q_cb, k_arr, k_cb, v_arr, v_cb, *, dqk, scale, tiles, window_blocks=None,
                    decay=None, hps=1):
    s_len = q_arr.shape[0]
    _, bq, bk = tiles.shape
    nq, nk = s_len // bq, s_len // bk
    assert q_cb % hps == 0 and k_cb % hps == 0 and v_cb % hps == 0
    in_specs = [pl.BlockSpec((bq, hps * dqk), lambda h, i: (i, q_cb // hps + h)),
                pl.BlockSpec((s_len, hps * dqk), lambda h, i: (0, k_cb // hps + h)),
                pl.BlockSpec((s_len, hps * HEAD_DIM), lambda h, i: (0, v_cb // hps + h)),
                pl.BlockSpec(tiles.shape, lambda h, i: (0, 0, 0))]
    args = [q_arr, k_arr, v_arr, tiles]
    if decay is not None:
        in_specs.append(pl.BlockSpec((hps, nk, bk), lambda h, i: (h, 0, 0)))
        args.append(decay.reshape(N_HEADS, nk, bk))
    return pl.pallas_call(
        functools.partial(_flash_kernel, bq=bq, bk=bk, dqk=dqk, scale2=scale * LOG2E,
                          window_blocks=window_blocks, has_decay=decay is not None, hps=hps),
        grid=(N_HEADS // hps, nq),
        in_specs=in_specs,
        out_specs=pl.BlockSpec((bq, hps * HEAD_DIM), lambda h, i: (i, h)),
        out_shape=jax.ShapeDtypeStruct((s_len, GROUP_WIDTH), BF16),
        compiler_params=_cparams(("parallel", "arbitrary")),
        name="flash_attention",
    )(*args)


def _score_tiles(bq, bk, count_fn, window):
    n_diag = bq // bk
    window_blocks = 0 if window is None else (window + bk - 1) // bk
    rel = jnp.arange(-(n_diag - 1), window_blocks + 1)
    d = rel[:, None, None] * bk + jnp.arange(bq)[None, :, None] - jnp.arange(bk)[None, None, :]
    count = jnp.where(d >= 0, count_fn(d), 0.0)
    tiles = jnp.where(count > 0, jnp.log2(jnp.maximum(count, 1.0)), NEG_INF).astype(F32)
    return tiles, (None if window is None else window_blocks)


def _causal_tiles(bq, bk):
    return _score_tiles(bq, bk, lambda d: jnp.ones(d.shape, F32), None)[0]


def _dilated_tiles(bq, bk):
    count_fn = lambda d: sum(((d <= w) & (d % dil == 0)).astype(F32) for w, dil in DILATED_PAIRS)
    return _score_tiles(bq, bk, count_fn, max(w for w, _ in DILATED_PAIRS))


def _stick_kernel(q_ref, k_ref, v_ref, t_ref, o_ref, *, bq, bk, scale, hps):
    qi = pl.program_id(1)
    d = HEAD_DIM
    n_diag = bq // bk
    qs = [q_ref[:, g * d:(g + 1) * d] for g in range(hps)]
    tri = t_ref[...]

    def head_block(g, j, r, acc, diag_t):
        start = pl.multiple_of(j * bk, bk)
        k = k_ref[pl.ds(start, bk), g * d:(g + 1) * d]
        v = v_ref[pl.ds(start, bk), g * d:(g + 1) * d]
        z = lax.dot_general(qs[g], k, (((1,), (1,)), ((), ())), preferred_element_type=F32) * scale
        sp = jnp.maximum(z, 0.0) + jnp.log(1.0 + jnp.exp(-jnp.abs(z)))
        log_1m = -sp
        if diag_t is not None:
            mask = (lax.broadcasted_iota(jnp.int32, (bq, bk), 1) + diag_t * bk
                    < lax.broadcasted_iota(jnp.int32, (bq, bk), 0))
            log_1m = jnp.where(mask, log_1m, 0.0)
        hi = log_1m.astype(BF16)
        lo = (log_1m - hi.astype(F32)).astype(BF16)
        after = (jnp.dot(hi, tri, preferred_element_type=F32) + jnp.dot(lo, tri, preferred_element_type=F32)) + r
        a = jnp.exp((z - sp) + after)
        if diag_t is not None:
            a = jnp.where(mask, a, 0.0)
        acc = acc + jnp.dot(a.astype(BF16), v, preferred_element_type=F32)
        r = r + jnp.sum(log_1m, axis=-1, keepdims=True)
        return r, acc

    def block(j, state, diag_t):
        return tuple(head_block(g, j, *state[g], diag_t) for g in range(hps))

    state = tuple((jnp.zeros((bq, 1), F32), jnp.zeros((bq, d), F32)) for _ in range(hps))
    j_diag = qi * n_diag
    for t in reversed(range(n_diag)):
        state = block(j_diag + t, state, t)

    def cond(c):
        j, state = c
        r_max = functools.reduce(jnp.maximum, [jnp.max(r) for r, _ in state])
        return jnp.logical_and(j >= 0, r_max > -F32_EXP_UNDERFLOW)

    def body(c):
        j, state = c
        return j - 1, block(j, state, None)

    _, state = lax.while_loop(cond, body, (j_diag - 1, state))
    for g, (_, acc) in enumerate(state):
        o_ref[:, g * d:(g + 1) * d] = acc.astype(o_ref.dtype)


def stick_breaking_attention(qkv, q_cb, k_cb, v_cb, *, scale, bq=512, bk=256, hps=4):
    s_len = qkv.shape[0]
    tri = jnp.tril(jnp.ones((bk, bk), F32), -1).astype(BF16)
    w = hps * HEAD_DIM
    assert q_cb % hps == 0 and k_cb % hps == 0 and v_cb % hps == 0
    return pl.pallas_call(
        functools.partial(_stick_kernel, bq=bq, bk=bk, scale=scale, hps=hps),
        grid=(N_HEADS // hps, s_len // bq),
        in_specs=[pl.BlockSpec((bq, w), lambda h, i: (i, q_cb // hps + h)),
                  pl.BlockSpec((s_len, w), lambda h, i: (0, k_cb // hps + h)),
                  pl.BlockSpec((s_len, w), lambda h, i: (0, v_cb // hps + h)),
                  pl.BlockSpec((bk, bk), lambda h, i: (0, 0))],
        out_specs=pl.BlockSpec((bq, w), lambda h, i: (i, h)),
        out_shape=jax.ShapeDtypeStruct((s_len, GROUP_WIDTH), BF16),
        compiler_params=_cparams(("parallel", "arbitrary")),
        name="stick_breaking",
    )(qkv, qkv, qkv, tri)


def _prep_mla_weights(w_uq, w_uk, w_uv):
    pad = 2 * HEAD_DIM
    wq = w_uq.reshape(Q_LORA_RANK, N_HEADS, HEAD_DIM + MLA_ROPE_DIM)
    w_q = jnp.pad(wq, ((0, 0), (0, 0), (0, pad - wq.shape[2]))).reshape(Q_LORA_RANK, N_HEADS * pad)
    wk = w_uk.reshape(KV_LORA_RANK, N_HEADS, HEAD_DIM)
    w_k = jnp.pad(wk, ((0, 0), (0, 0), (0, pad - HEAD_DIM))).reshape(KV_LORA_RANK, N_HEADS * pad)
    return w_q.astype(BF16), jnp.concatenate([w_k, w_uv], axis=1).astype(BF16)


def kernel(x, g_attn, w_in, g_q, g_kv, w_uq, w_uk, w_uv, b_f, g_out, w_o, g_mlp, w_up, w_down, g_final):
    b, s_len, d_model = x.shape
    assert b == 1 and s_len % 2048 == 0
    depth = w_in.shape[0]
    xf = x.reshape(s_len, d_model)

    mla_tables = _rope_tables(s_len, MLA_ROPE_DIM, LANES, pass_through=False)
    dil_tables = _rope_tables(s_len, PARTIAL_ROPE_DIM, HEAD_DIM, pass_through=True)
    causal = _causal_tiles(FLASH_BQ, FLASH_BK)
    dil_tiles, dil_blocks = _dilated_tiles(DILATED_BQ, DILATED_BK)
    scale = HEAD_DIM ** -0.5
    mla_scale = (HEAD_DIM + MLA_ROPE_DIM) ** -0.5
    nh = N_HEADS
    n_a = Q_LORA_RANK + KV_LORA_RANK + MLA_ROPE_DIM
    n_a128 = -(-n_a // LANES) * LANES
    gate_col0 = w_in.shape[2] - LANES
    w_in_t = jnp.swapaxes(w_in, 1, 2)

    for l in range(depth):
        w_q, w_kv = _prep_mla_weights(w_uq[l], w_uk[l], w_uv[l])

        h = rmsnorm(xf, g_attn[l], BF16)
        seg_a = matmul_f32w(h, w_in_t, l, 0, n_a128, F32, transposed=True, tn=n_a128 // 2)
        bqk_r = matmul_f32w(h, w_in_t, l, n_a, 2 * GROUP_WIDTH, BF16, transposed=True,
                            rope=(dil_tables, PARTIAL_ROPE_DIM // 2))
        rest = matmul_f32w(h, w_in_t, l, n_a + 2 * GROUP_WIDTH, 7 * GROUP_WIDTH, BF16, transposed=True)
        gate = matmul_f32w(h, w_in_t, l, gate_col0, LANES, F32, transposed=True, tn=LANES)

        q_cat = mla_proj(seg_a, 0, g_q[l], w_q, mla_tables)
        kv_cat = mla_proj(seg_a, Q_LORA_RANK, g_kv[l], w_kv, mla_tables,
                          kr_cols=N_HEADS * 2 * HEAD_DIM)
        o_a = flash_attention(q_cat, 0, kv_cat, 0, kv_cat, 2 * nh, dqk=2 * HEAD_DIM, scale=mla_scale, tiles=causal,
                              hps=2)

        o_b = flash_attention(bqk_r, 0, bqk_r, nh, rest, 0, dqk=HEAD_DIM, scale=scale, tiles=dil_tiles,
                              window_blocks=dil_blocks, hps=2)

        o_c = stick_breaking_attention(rest, nh, 2 * nh, 3 * nh, scale=scale)

        c_f = forget_cumsum(gate, b_f[l], out_scale=LOG2E)
        o_d = flash_attention(rest, 4 * nh, rest, 5 * nh, rest, 6 * nh, dqk=HEAD_DIM, scale=scale, tiles=causal,
                              decay=c_f, hps=2)

        mix = group_norm_concat(o_a, o_b, o_c, o_d, g_out[l])
        xf = matmul_f32w(mix, w_o, l, 0, d_model, F32, res=xf, tm=512)

        h2 = rmsnorm(xf, g_mlp[l], BF16)
        u, w_down_bf = matmul_f32w(h2, w_up, l, 0, w_up.shape[2], BF16, act="relu2", side_cast=(w_down, l))
        xf = matmul(u, w_down_bf, F32, res=xf)

    return rmsnorm(xf, g_final, F32).reshape(b, s_len, d_model)
```

```python
import functools
import math

import jax
import jax.numpy as jnp
from jax import lax
from jax.experimental import pallas as pl
from jax.experimental.pallas import tpu as pltpu

F32 = jnp.float32
BF16 = jnp.bfloat16

HEAD_DIM = 128
N_HEADS = 8
GROUP_WIDTH = N_HEADS * HEAD_DIM
Q_LORA_RANK = 896
KV_LORA_RANK = 512
MLA_ROPE_DIM = 64
ROPE_THETA = 500000.0
PARTIAL_ROPE_DIM = HEAD_DIM // 4
DILATED_PAIRS = ((128, 1), (512, 4), (2048, 16))
EPS = 1e-6
NEG_INF = -1e30

LANES = 128
VMEM_LIMIT_BYTES = 62 * 1024 * 1024

F32_EXP_UNDERFLOW = 104.0
LOG2E = math.log2(math.e)

FLASH_BQ, FLASH_BK = 1024, 1024
DILATED_BQ, DILATED_BK = 512, 512


def _cparams(semantics):
    return pltpu.CompilerParams(dimension_semantics=semantics, vmem_limit_bytes=VMEM_LIMIT_BYTES)


def _rms(x, g):
    return x * lax.rsqrt(jnp.mean(x * x, axis=-1, keepdims=True) + EPS) * g


def _rmsnorm_kernel(x_ref, g_ref, o_ref):
    o_ref[...] = _rms(x_ref[...], g_ref[...]).astype(o_ref.dtype)


def rmsnorm(x, g, out_dtype, tm=512):
    m, d = x.shape
    return pl.pallas_call(
        _rmsnorm_kernel,
        grid=(m // tm,),
        in_specs=[pl.BlockSpec((tm, d), lambda i: (i, 0)), pl.BlockSpec((1, d), lambda i: (0, 0))],
        out_specs=pl.BlockSpec((tm, d), lambda i: (i, 0)),
        out_shape=jax.ShapeDtypeStruct((m, d), out_dtype),
        compiler_params=_cparams(("parallel",)),
        name="rmsnorm",
    )(x, g.reshape(1, d))


def _group_norm_kernel(a_ref, b_ref, c_ref, d_ref, g_ref, o_ref):
    for gi, r in enumerate((a_ref, b_ref, c_ref, d_ref)):
        sl = slice(gi * GROUP_WIDTH, (gi + 1) * GROUP_WIDTH)
        o_ref[:, sl] = _rms(r[...].astype(F32), g_ref[:, sl]).astype(o_ref.dtype)


def group_norm_concat(o_a, o_b, o_c, o_d, g, tm=512):
    m = o_a.shape[0]
    spec = pl.BlockSpec((tm, GROUP_WIDTH), lambda i: (i, 0))
    return pl.pallas_call(
        _group_norm_kernel,
        grid=(m // tm,),
        in_specs=[spec, spec, spec, spec, pl.BlockSpec((1, 4 * GROUP_WIDTH), lambda i: (0, 0))],
        out_specs=pl.BlockSpec((tm, 4 * GROUP_WIDTH), lambda i: (i, 0)),
        out_shape=jax.ShapeDtypeStruct((m, 4 * GROUP_WIDTH), BF16),
        compiler_params=_cparams(("parallel",)),
        name="group_norm",
    )(o_a, o_b, o_c, o_d, g.reshape(1, -1))


def _mm_kernel(*refs, nk, act, has_res):
    a_ref, w_ref = refs[0], refs[1]
    res_ref = refs[2] if has_res else None
    o_ref = refs[2 + has_res]

    def finish(acc):
        if act == "relu2":
            acc = jnp.square(jnp.maximum(acc, 0.0))
        if has_res:
            acc = res_ref[...] + acc
        o_ref[...] = acc.astype(o_ref.dtype)

    if nk == 1:
        finish(jnp.dot(a_ref[...], w_ref[...], preferred_element_type=F32))
    elif act is None and o_ref.dtype == F32:
        k = pl.program_id(2)

        @pl.when(k == 0)
        def _():
            o_ref[...] = res_ref[...] if has_res else jnp.zeros_like(o_ref)

        o_ref[...] += jnp.dot(a_ref[...], w_ref[...], preferred_element_type=F32)
    else:
        acc_ref = refs[3 + has_res]
        k = pl.program_id(2)

        @pl.when(k == 0)
        def _():
            acc_ref[...] = jnp.zeros_like(acc_ref)

        acc_ref[...] += jnp.dot(a_ref[...], w_ref[...], preferred_element_type=F32)

        @pl.when(k == nk - 1)
        def _():
            finish(acc_ref[...])


def _pick(n, prefs):
    for p in prefs:
        if n % p == 0:
            return p
    return n


def matmul(a, w, out_dtype, act=None, res=None, layer=None, tm=None, tn=None, tk=None, name="matmul"):
    m, kdim = a.shape
    n = w.shape[-1]
    tm = tm or _pick(m, (1024, 512, 256))
    tn = tn or _pick(n, (1024, 768, 512, 256, 128))
    has_res = res is not None
    tk = tk or _pick(kdim, (4096, 2048, 1024))
    nk = kdim // tk
    w_spec = (pl.BlockSpec((tk, tn), lambda i, j, k: (k, j)) if layer is None else
              pl.BlockSpec((None, tk, tn), lambda i, j, k: (layer, k, j)))
    in_specs = [pl.BlockSpec((tm, tk), lambda i, j, k: (i, k)), w_spec]
    args = [a, w]
    if has_res:
        in_specs.append(pl.BlockSpec((tm, tn), lambda i, j, k: (i, j)))
        args.append(res)
    scratch = [pltpu.VMEM((tm, tn), F32)] if (nk > 1 and not (act is None and out_dtype == F32)) else []
    return pl.pallas_call(
        functools.partial(_mm_kernel, nk=nk, act=act, has_res=has_res),
        grid=(m // tm, n // tn, nk),
        in_specs=in_specs,
        out_specs=pl.BlockSpec((tm, tn), lambda i, j, k: (i, j)),
        out_shape=jax.ShapeDtypeStruct((m, n), out_dtype),
        scratch_shapes=scratch,
        compiler_params=_cparams(("parallel", "parallel", "arbitrary")),
        name=name,
    )(*args)


def _mmw_kernel(*refs, layer, col0, tn, nj, transposed, act, has_res, rope_half, norm_dim, emit_norm, has_side,
                row_chunk):
    refs = list(refs)
    a_ref, w_hbm = refs.pop(0), refs.pop(0)
    res_ref = refs.pop(0) if has_res else None
    tab_refs = [refs.pop(0) for _ in range(3)] if rope_half else None
    ssq_in_ref = refs.pop(0) if norm_dim else None
    gain_ref = refs.pop(0) if emit_norm else None
    side_in_ref = refs.pop(0) if has_side else None
    o_ref = refs.pop(0)
    xg_ref, ssq_out_ref = (refs.pop(0), refs.pop(0)) if emit_norm else (None, None)
    side_out_ref = refs.pop(0) if has_side else None
    wf32_ref, wbf_ref, sem = refs
    j = pl.program_id(0)

    if has_side:
        side_out_ref[...] = side_in_ref[...].astype(BF16)

    def w_copy(jt):
        start = pl.multiple_of(col0 + jt * tn, 8 if transposed else LANES)
        src = w_hbm.at[layer, pl.ds(start, tn), :] if transposed else w_hbm.at[layer, :, pl.ds(start, tn)]
        return pltpu.make_async_copy(src, wf32_ref, sem.at[0])

    @pl.when(pl.program_id(1) == 0)
    def _():
        @pl.when(j == 0)
        def _():
            w_copy(0).start()

        w_copy(j).wait()

        def cast_rows(r, c):
            rows = pl.ds(pl.multiple_of(r * row_chunk, row_chunk), row_chunk)
            wbf_ref[rows, :] = wf32_ref[rows, :].astype(BF16)
            return c
        lax.fori_loop(0, wf32_ref.shape[0] // row_chunk, cast_rows, 0)

        @pl.when(j + 1 < nj)
        def _():
            w_copy(j + 1).start()

    dims = (((1,), (1,)), ((), ())) if transposed else (((1,), (0,)), ((), ()))
    acc = lax.dot_general(a_ref[...], wbf_ref[...], dims, preferred_element_type=F32)
    if norm_dim:
        acc = acc * lax.rsqrt(jnp.sum(ssq_in_ref[...], axis=0) * (1.0 / norm_dim) + EPS)
    if act == "relu2":
        acc = jnp.square(jnp.maximum(acc, 0.0))
    if rope_half:
        acc = _rope(acc, *(t[...] for t in tab_refs), rope_half)
    if has_res:
        acc = res_ref[...] + acc
    o_ref[...] = acc.astype(o_ref.dtype)
    if emit_norm:
        xg_ref[...] = (acc * gain_ref[...]).astype(BF16)
        ssq_out_ref[...] = jnp.sum(acc * acc, axis=1, keepdims=True)


def matmul_f32w(a, w_stack, layer, col0, n, out_dtype, act=None, res=None, rope=None, transposed=False,
                side_cast=None, row_ssq=None, emit_norm_gain=None, tm=1024, tn=1024, name="matmul_f32w"):
    m, kdim = a.shape
    has_res = res is not None
    assert col0 % (8 if transposed else LANES) == 0 and n % tn == 0 and m % tm == 0
    w_shape = (tn, kdim) if transposed else (kdim, tn)
    in_specs = [pl.BlockSpec((tm, kdim), lambda j, i: (i, 0)), pl.BlockSpec(memory_space=pl.ANY)]
    args = [a, w_stack]
    if has_res:
        in_specs.append(pl.BlockSpec((tm, tn), lambda j, i: (i, j)))
        args.append(res)
    rope_half = 0
    if rope is not None:
        tables, rope_half = rope
        in_specs += [pl.BlockSpec((tm, tables[0].shape[1]), lambda j, i: (i, 0))] * 3
        args += list(tables)
    if row_ssq is not None:
        in_specs.append(pl.BlockSpec((row_ssq.shape[0], tm, 1), lambda j, i: (0, i, 0)))
        args.append(row_ssq)
    out_specs = [pl.BlockSpec((tm, tn), lambda j, i: (i, j))]
    out_shape = [jax.ShapeDtypeStruct((m, n), out_dtype)]
    nj, ni = n // tn, m // tm
    if emit_norm_gain is not None:
        in_specs.append(pl.BlockSpec((1, tn), lambda j, i: (0, j)))
        args.append(emit_norm_gain.reshape(1, n))
        out_specs += [pl.BlockSpec((tm, tn), lambda j, i: (i, j)), pl.BlockSpec((None, tm, 1), lambda j, i: (j, i, 0))]
        out_shape += [jax.ShapeDtypeStruct((m, n), BF16), jax.ShapeDtypeStruct((nj, m, 1), F32)]
    if side_cast is not None:
        side_stack, side_layer = side_cast
        _, side_r, side_c = side_stack.shape
        slab = side_r // (nj * ni)
        assert slab * nj * ni == side_r and slab % 16 == 0
        in_specs.append(pl.BlockSpec((None, slab, side_c), lambda j, i: (side_layer, j * ni + i, 0)))
        args.append(side_stack)
        out_specs.append(pl.BlockSpec((slab, side_c), lambda j, i: (j * ni + i, 0)))
        out_shape.append(jax.ShapeDtypeStruct((side_r, side_c), BF16))
    out = pl.pallas_call(
        functools.partial(_mmw_kernel, layer=layer, col0=col0, tn=tn, nj=nj, transposed=transposed, act=act,
                          has_res=has_res, rope_half=rope_half, norm_dim=kdim if row_ssq is not None else 0,
                          emit_norm=emit_norm_gain is not None, has_side=side_cast is not None,
                          row_chunk=min(256, w_shape[0])),
        grid=(nj, ni),
        in_specs=in_specs,
        out_specs=out_specs,
        out_shape=out_shape,
        scratch_shapes=[pltpu.VMEM(w_shape, F32), pltpu.VMEM(w_shape, BF16), pltpu.SemaphoreType.DMA((1,))],
        compiler_params=_cparams(("arbitrary", "arbitrary")),
        name=name,
    )(*args)
    return out[0] if len(out) == 1 else tuple(out)


def _rope_tables(s_len, r, period, pass_through):
    half = r // 2
    inv = ROPE_THETA ** (-jnp.arange(half, dtype=F32) * (2.0 / r))
    ang = jnp.arange(s_len, dtype=F32)[:, None] * inv[None, :]
    cos, sin = jnp.cos(ang), jnp.sin(ang)
    fill = jnp.ones if pass_through else jnp.zeros
    zeros = jnp.zeros((s_len, period - r), F32)
    zh = jnp.zeros((s_len, half), F32)
    c = jnp.concatenate([cos, cos, fill((s_len, period - r), F32)], axis=1)
    sa = jnp.concatenate([-sin, zh, zeros], axis=1)
    sb = jnp.concatenate([zh, sin, zeros], axis=1)
    return c, sa, sb


def _rope(x, c, sa, sb, half):
    w = x.shape[-1]
    reps = w // c.shape[-1]
    if reps > 1:
        c, sa, sb = (jnp.tile(t, (1, reps)) for t in (c, sa, sb))
    return x * c + pltpu.roll(x, w - half, 1) * sa + pltpu.roll(x, half, 1) * sb


def _mla_proj_kernel(seg_ref, g_ref, w_ref, c_ref, sa_ref, sb_ref, o_ref, *, col0, width, kr_col0, kr_cols):
    half = MLA_ROPE_DIM // 2
    group = 2 * HEAD_DIM
    a = _rms(seg_ref[:, col0:col0 + width], g_ref[...]).astype(BF16)
    acc = jnp.dot(a, w_ref[...], preferred_element_type=F32)
    c, sa, sb = c_ref[...], sa_ref[...], sb_ref[...]
    if kr_cols is None:
        for lo in range(0, acc.shape[1], group):
            o_ref[:, lo:lo + HEAD_DIM] = acc[:, lo:lo + HEAD_DIM].astype(o_ref.dtype)
            o_ref[:, lo + HEAD_DIM:lo + group] = _rope(acc[:, lo + HEAD_DIM:lo + group], c, sa, sb, half).astype(o_ref.dtype)
    else:
        kr = _rope(seg_ref[:, kr_col0:kr_col0 + LANES], c, sa, sb, half)
        for lo in range(0, kr_cols, group):
            o_ref[:, lo:lo + HEAD_DIM] = acc[:, lo:lo + HEAD_DIM].astype(o_ref.dtype)
            o_ref[:, lo + HEAD_DIM:lo + group] = (acc[:, lo + HEAD_DIM:lo + group] + kr).astype(o_ref.dtype)
        o_ref[:, kr_cols:] = acc[:, kr_cols:].astype(o_ref.dtype)


def mla_proj(seg_a, col0, g, w, tables, kr_cols=None, tm=1024):
    m, seg_w = seg_a.shape
    width, n = w.shape
    tspec = pl.BlockSpec((tm, tables[0].shape[1]), lambda i: (i, 0))
    return pl.pallas_call(
        functools.partial(_mla_proj_kernel, col0=col0, width=width, kr_col0=Q_LORA_RANK + KV_LORA_RANK,
                          kr_cols=kr_cols),
        grid=(m // tm,),
        in_specs=[pl.BlockSpec((tm, seg_w), lambda i: (i, 0)),
                  pl.BlockSpec((1, width), lambda i: (0, 0)),
                  pl.BlockSpec((width, n), lambda i: (0, 0)),
                  tspec, tspec, tspec],
        out_specs=pl.BlockSpec((tm, n), lambda i: (i, 0)),
        out_shape=jax.ShapeDtypeStruct((m, n), BF16),
        compiler_params=_cparams(("parallel",)),
        name="mla_proj",
    )(seg_a, g.reshape(1, -1), w, *tables)


def _split3(x):
    hi = x.astype(BF16)
    r1 = x - hi.astype(F32)
    mid = r1.astype(BF16)
    lo = (r1 - mid.astype(F32)).astype(BF16)
    return hi, mid, lo


def _forget_kernel(x_ref, b_ref, u_ref, o_ref, carry_ref, *, out_scale):
    @pl.when(pl.program_id(0) == 0)
    def _():
        carry_ref[...] = jnp.zeros_like(carry_ref)

    xt = jnp.transpose(x_ref[...])
    y = -(xt[LANES - N_HEADS:, :] + b_ref[...])
    logf = -(jnp.maximum(y, 0.0) + jnp.log(1.0 + jnp.exp(-jnp.abs(y))))
    u = u_ref[...]
    cs = sum(jnp.dot(p, u, preferred_element_type=F32) for p in _split3(logf))
    c = cs + carry_ref[:, 0:1]
    o_ref[...] = c * out_scale
    carry_ref[...] = jnp.broadcast_to(c[:, -1:], carry_ref.shape)


def forget_cumsum(gate, b_f, tc=512, out_scale=1.0):
    m = gate.shape[0]
    u = jnp.triu(jnp.ones((tc, tc), F32)).astype(BF16)
    return pl.pallas_call(
        functools.partial(_forget_kernel, out_scale=out_scale),
        grid=(m // tc,),
        in_specs=[pl.BlockSpec((tc, LANES), lambda i: (i, 0)),
                  pl.BlockSpec((N_HEADS, 1), lambda i: (0, 0)),
                  pl.BlockSpec((tc, tc), lambda i: (0, 0))],
        out_specs=pl.BlockSpec((N_HEADS, tc), lambda i: (0, i)),
        out_shape=jax.ShapeDtypeStruct((N_HEADS, m), F32),
        scratch_shapes=[pltpu.VMEM((N_HEADS, LANES), F32)],
        compiler_params=_cparams(("arbitrary",)),
        name="forget_cumsum",
    )(gate, b_f.reshape(N_HEADS, 1), u)


def _flash_kernel(*refs, bq, bk, dqk, scale2, window_blocks, has_decay, hps):
    q_ref, k_ref, v_ref, tile_ref = refs[:4]
    ck_ref = refs[4] if has_decay else None
    o_ref = refs[4 + has_decay]
    dv = HEAD_DIM
    n_diag = bq // bk
    qi = pl.program_id(1)
    qs = [q_ref[:, g * dqk:(g + 1) * dqk] for g in range(hps)]

    def head_step(g, j, carry, tile):
        m, l, acc = carry
        start = pl.multiple_of(j * bk, bk)
        k = k_ref[pl.ds(start, bk), g * dqk:(g + 1) * dqk]
        v = v_ref[pl.ds(start, bk), g * dv:(g + 1) * dv]
        s = lax.dot_general(qs[g], k, (((1,), (1,)), ((), ())), preferred_element_type=F32) * scale2
        if has_decay:
            s = s - ck_ref[g, pl.ds(j, 1), :]
        if tile is not None:
            s = s + tile
        m_new = jnp.maximum(m, jnp.max(s, axis=-1, keepdims=True))
        alpha = jnp.exp2(m - m_new)
        p = jnp.exp2(s - m_new)
        l = alpha * l + jnp.sum(p, axis=-1, keepdims=True)
        acc = alpha * acc + jnp.dot(p.astype(BF16), v, preferred_element_type=F32)
        return m_new, l, acc

    def step(j, carry, tile):
        return tuple(head_step(g, j, carry[g], tile) for g in range(hps))

    carry = tuple((jnp.full((bq, 1), NEG_INF, F32), jnp.zeros((bq, 1), F32), jnp.zeros((bq, dv), F32))
                  for _ in range(hps))
    j_diag = qi * n_diag
    if window_blocks is None:
        lo = 0
        body = lambda j, c: step(j, c, None)
    else:
        lo = jnp.maximum(j_diag - window_blocks, 0)
        body = lambda j, c: step(j, c, tile_ref[n_diag - 1 + j_diag - j])
    carry = lax.fori_loop(lo, j_diag, body, carry)
    for t in range(n_diag):
        carry = step(j_diag + t, carry, tile_ref[n_diag - 1 - t])
    for g, (m, l, acc) in enumerate(carry):
        o_ref[:, g * dv:(g + 1) * dv] = (acc / l).astype(o_ref.dtype)


def flash_attention(q_arr, q_cb, k_arr, k_cb, v_arr, v_cb, *, dqk, scale, tiles, window_blocks=None,
                    decay=None, hps=1):
    s_len = q_arr.shape[0]
    _, bq, bk = tiles.shape
    nq, nk = s_len // bq, s_len // bk
    assert q_cb % hps == 0 and k_cb % hps == 0 and v_cb % hps == 0
    in_specs = [pl.BlockSpec((bq, hps * dqk), lambda h, i: (i, q_cb // hps + h)),
                pl.BlockSpec((s_len, hps * dqk), lambda h, i: (0, k_cb // hps + h)),
                pl.BlockSpec((s_len, hps * HEAD_DIM), lambda h, i: (0, v_cb // hps + h)),
                pl.BlockSpec(tiles.shape, lambda h, i: (0, 0, 0))]
    args = [q_arr, k_arr, v_arr, tiles]
    if decay is not None:
        in_specs.append(pl.BlockSpec((hps, nk, bk), lambda h, i: (h, 0, 0)))
        args.append(decay.reshape(N_HEADS, nk, bk))
    return pl.pallas_call(
        functools.partial(_flash_kernel, bq=bq, bk=bk, dqk=dqk, scale2=scale * LOG2E,
                          window_blocks=window_blocks, has_decay=decay is not None, hps=hps),
        grid=(N_HEADS // hps, nq),
        in_specs=in_specs,
        out_specs=pl.BlockSpec((bq, hps * HEAD_DIM), lambda h, i: (i, h)),
        out_shape=jax.ShapeDtypeStruct((s_len, GROUP_WIDTH), BF16),
        compiler_params=_cparams(("parallel", "arbitrary")),
        name="flash_attention",
    )(*args)


def _score_tiles(bq, bk, count_fn, window):
    n_diag = bq // bk
    window_blocks = 0 if window is None else (window + bk - 1) // bk
    rel = jnp.arange(-(n_diag - 1), window_blocks + 1)
    d = rel[:, None, None] * bk + jnp.arange(bq)[None, :, None] - jnp.arange(bk)[None, None, :]
    count = jnp.where(d >= 0, count_fn(d), 0.0)
    tiles = jnp.where(count > 0, jnp.log2(jnp.maximum(count, 1.0)), NEG_INF).astype(F32)
    return tiles, (None if window is None else window_blocks)


def _causal_tiles(bq, bk):
    return _score_tiles(bq, bk, lambda d: jnp.ones(d.shape, F32), None)[0]


def _dilated_tiles(bq, bk):
    count_fn = lambda d: sum(((d <= w) & (d % dil == 0)).astype(F32) for w, dil in DILATED_PAIRS)
    return _score_tiles(bq, bk, count_fn, max(w for w, _ in DILATED_PAIRS))


def _stick_kernel(q_ref, k_ref, v_ref, t_ref, o_ref, *, bq, bk, scale, hps):
    qi = pl.program_id(1)
    d = HEAD_DIM
    n_diag = bq // bk
    qs = [q_ref[:, g * d:(g + 1) * d] for g in range(hps)]
    tri = t_ref[...]

    def head_block(g, j, r, acc, diag_t):
        start = pl.multiple_of(j * bk, bk)
        k = k_ref[pl.ds(start, bk), g * d:(g + 1) * d]
        v = v_ref[pl.ds(start, bk), g * d:(g + 1) * d]
        z = lax.dot_general(qs[g], k, (((1,), (1,)), ((), ())), preferred_element_type=F32) * scale
        sp = jnp.maximum(z, 0.0) + jnp.log(1.0 + jnp.exp(-jnp.abs(z)))
        log_1m = -sp
        if diag_t is not None:
            mask = (lax.broadcasted_iota(jnp.int32, (bq, bk), 1) + diag_t * bk
                    < lax.broadcasted_iota(jnp.int32, (bq, bk), 0))
            log_1m = jnp.where(mask, log_1m, 0.0)
        hi = log_1m.astype(BF16)
        lo = (log_1m - hi.astype(F32)).astype(BF16)
        after = (jnp.dot(hi, tri, preferred_element_type=F32) + jnp.dot(lo, tri, preferred_element_type=F32)) + r
        a = jnp.exp((z - sp) + after)
        if diag_t is not None:
            a = jnp.where(mask, a, 0.0)
        acc = acc + jnp.dot(a.astype(BF16), v, preferred_element_type=F32)
        r = r + jnp.sum(log_1m, axis=-1, keepdims=True)
        return r, acc

    def block(j, state, diag_t):
        return tuple(head_block(g, j, *state[g], diag_t) for g in range(hps))

    state = tuple((jnp.zeros((bq, 1), F32), jnp.zeros((bq, d), F32)) for _ in range(hps))
    j_diag = qi * n_diag
    for t in reversed(range(n_diag)):
        state = block(j_diag + t, state, t)

    def cond(c):
        j, state = c
        r_max = functools.reduce(jnp.maximum, [jnp.max(r) for r, _ in state])
        return jnp.logical_and(j >= 0, r_max > -F32_EXP_UNDERFLOW)

    def body(c):
        j, state = c
        return j - 1, block(j, state, None)

    _, state = lax.while_loop(cond, body, (j_diag - 1, state))
    for g, (_, acc) in enumerate(state):
        o_ref[:, g * d:(g + 1) * d] = acc.astype(o_ref.dtype)


def stick_breaking_attention(qkv, q_cb, k_cb, v_cb, *, scale, bq=512, bk=256, hps=4):
    s_len = qkv.shape[0]
    tri = jnp.tril(jnp.ones((bk, bk), F32), -1).astype(BF16)
    w = hps * HEAD_DIM
    assert q_cb % hps == 0 and k_cb % hps == 0 and v_cb % hps == 0
    return pl.pallas_call(
        functools.partial(_stick_kernel, bq=bq, bk=bk, scale=scale, hps=hps),
        grid=(N_HEADS // hps, s_len // bq),
        in_specs=[pl.BlockSpec((bq, w), lambda h, i: (i, q_cb // hps + h)),
                  pl.BlockSpec((s_len, w), lambda h, i: (0, k_cb // hps + h)),
                  pl.BlockSpec((s_len, w), lambda h, i: (0, v_cb // hps + h)),
                  pl.BlockSpec((bk, bk), lambda h, i: (0, 0))],
        out_specs=pl.BlockSpec((bq, w), lambda h, i: (i, h)),
        out_shape=jax.ShapeDtypeStruct((s_len, GROUP_WIDTH), BF16),
        compiler_params=_cparams(("parallel", "arbitrary")),
        name="stick_breaking",
    )(qkv, qkv, qkv, tri)


def _prep_mla_weights(w_uq, w_uk, w_uv):
    pad = 2 * HEAD_DIM
    wq = w_uq.reshape(Q_LORA_RANK, N_HEADS, HEAD_DIM + MLA_ROPE_DIM)
    w_q = jnp.pad(wq, ((0, 0), (0, 0), (0, pad - wq.shape[2]))).reshape(Q_LORA_RANK, N_HEADS * pad)
    wk = w_uk.reshape(KV_LORA_RANK, N_HEADS, HEAD_DIM)
    w_k = jnp.pad(wk, ((0, 0), (0, 0), (0, pad - HEAD_DIM))).reshape(KV_LORA_RANK, N_HEADS * pad)
    return w_q.astype(BF16), jnp.concatenate([w_k, w_uv], axis=1).astype(BF16)


def kernel(x, g_attn, w_in, g_q, g_kv, w_uq, w_uk, w_uv, b_f, g_out, w_o, g_mlp, w_up, w_down, g_final):
    b, s_len, d_model = x.shape
    assert b == 1 and s_len % 2048 == 0
    depth = w_in.shape[0]
    xf = x.reshape(s_len, d_model)

    mla_tables = _rope_tables(s_len, MLA_ROPE_DIM, LANES, pass_through=False)
    dil_tables = _rope_tables(s_len, PARTIAL_ROPE_DIM, HEAD_DIM, pass_through=True)
    causal = _causal_tiles(FLASH_BQ, FLASH_BK)
    dil_tiles, dil_blocks = _dilated_tiles(DILATED_BQ, DILATED_BK)
    scale = HEAD_DIM ** -0.5
    mla_scale = (HEAD_DIM + MLA_ROPE_DIM) ** -0.5
    nh = N_HEADS
    n_a = Q_LORA_RANK + KV_LORA_RANK + MLA_ROPE_DIM
    n_a128 = -(-n_a // LANES) * LANES
    gate_col0 = w_in.shape[2] - LANES
    w_in_t = jnp.swapaxes(w_in, 1, 2)

    for l in range(depth):
        w_q, w_kv = _prep_mla_weights(w_uq[l], w_uk[l], w_uv[l])

        h = rmsnorm(xf, g_attn[l], BF16)
        seg_a = matmul_f32w(h, w_in_t, l, 0, n_a128, F32, transposed=True, tn=n_a128 // 2)
        bqk_r = matmul_f32w(h, w_in_t, l, n_a, 2 * GROUP_WIDTH, BF16, transposed=True,
                            rope=(dil_tables, PARTIAL_ROPE_DIM // 2))
        rest = matmul_f32w(h, w_in_t, l, n_a + 2 * GROUP_WIDTH, 7 * GROUP_WIDTH, BF16, transposed=True)
        gate = matmul_f32w(h, w_in_t, l, gate_col0, LANES, F32, transposed=True, tn=LANES)

        q_cat = mla_proj(seg_a, 0, g_q[l], w_q, mla_tables)
        kv_cat = mla_proj(seg_a, Q_LORA_RANK, g_kv[l], w_kv, mla_tables,
                          kr_cols=N_HEADS * 2 * HEAD_DIM)
        o_a = flash_attention(q_cat, 0, kv_cat, 0, kv_cat, 2 * nh, dqk=2 * HEAD_DIM, scale=mla_scale, tiles=causal,
                              hps=2)

        o_b = flash_attention(bqk_r, 0, bqk_r, nh, rest, 0, dqk=HEAD_DIM, scale=scale, tiles=dil_tiles,
                              window_blocks=dil_blocks, hps=2)

        o_c = stick_breaking_attention(rest, nh, 2 * nh, 3 * nh, scale=scale)

        c_f = forget_cumsum(gate, b_f[l], out_scale=LOG2E)
        o_d = flash_attention(rest, 4 * nh, rest, 5 * nh, rest, 6 * nh, dqk=HEAD_DIM, scale=scale, tiles=causal,
                              decay=c_f, hps=2)

        mix = group_norm_concat(o_a, o_b, o_c, o_d, g_out[l])
        xf, xg, ssq = matmul_f32w(mix, w_o, l, 0, d_model, F32, res=xf, emit_norm_gain=g_mlp[l], tm=512)
        ssq = jnp.sum(ssq, axis=0, keepdims=True)
        u, w_down_bf = matmul_f32w(xg, w_up, l, 0, w_up.shape[2], BF16, act="relu2", row_ssq=ssq,
                                   side_cast=(w_down, l))
        xf = matmul(u, w_down_bf, F32, res=xf)

    return rmsnorm(xf, g_final, F32).reshape(b, s_len, d_model)
```

```python
import functools
import math

import jax
import jax.numpy as jnp
from jax import lax
from jax.experimental import pallas as pl
from jax.experimental.pallas import tpu as pltpu

F32 = jnp.float32
BF16 = jnp.bfloat16

HEAD_DIM = 128
N_HEADS = 8
GROUP_WIDTH = N_HEADS * HEAD_DIM
Q_LORA_RANK = 896
KV_LORA_RANK = 512
MLA_ROPE_DIM = 64
ROPE_THETA = 500000.0
PARTIAL_ROPE_DIM = HEAD_DIM // 4
DILATED_PAIRS = ((128, 1), (512, 4), (2048, 16))
EPS = 1e-6
NEG_INF = -1e30

LANES = 128
VMEM_LIMIT_BYTES = 62 * 1024 * 1024

F32_EXP_UNDERFLOW = 104.0
LOG2E = math.log2(math.e)

FLASH_BQ, FLASH_BK = 1024, 1024
DILATED_BQ, DILATED_BK = 512, 512


def _cparams(semantics):
    return pltpu.CompilerParams(dimension_semantics=semantics, vmem_limit_bytes=VMEM_LIMIT_BYTES)


def _rms(x, g):
    return x * lax.rsqrt(jnp.mean(x * x, axis=-1, keepdims=True) + EPS) * g


def _rmsnorm_kernel(x_ref, g_ref, o_ref):
    o_ref[...] = _rms(x_ref[...], g_ref[...]).astype(o_ref.dtype)


def rmsnorm(x, g, out_dtype, tm=512):
    m, d = x.shape
    return pl.pallas_call(
        _rmsnorm_kernel,
        grid=(m // tm,),
        in_specs=[pl.BlockSpec((tm, d), lambda i: (i, 0)), pl.BlockSpec((1, d), lambda i: (0, 0))],
        out_specs=pl.BlockSpec((tm, d), lambda i: (i, 0)),
        out_shape=jax.ShapeDtypeStruct((m, d), out_dtype),
        compiler_params=_cparams(("parallel",)),
        name="rmsnorm",
    )(x, g.reshape(1, d))


def _group_norm_kernel(a_ref, b_ref, c_ref, d_ref, g_ref, o_ref):
    for gi, r in enumerate((a_ref, b_ref, c_ref, d_ref)):
        sl = slice(gi * GROUP_WIDTH, (gi + 1) * GROUP_WIDTH)
        o_ref[:, sl] = _rms(r[...].astype(F32), g_ref[:, sl]).astype(o_ref.dtype)


def group_norm_concat(o_a, o_b, o_c, o_d, g, tm=512):
    m = o_a.shape[0]
    spec = pl.BlockSpec((tm, GROUP_WIDTH), lambda i: (i, 0))
    return pl.pallas_call(
        _group_norm_kernel,
        grid=(m // tm,),
        in_specs=[spec, spec, spec, spec, pl.BlockSpec((1, 4 * GROUP_WIDTH), lambda i: (0, 0))],
        out_specs=pl.BlockSpec((tm, 4 * GROUP_WIDTH), lambda i: (i, 0)),
        out_shape=jax.ShapeDtypeStruct((m, 4 * GROUP_WIDTH), BF16),
        compiler_params=_cparams(("parallel",)),
        name="group_norm",
    )(o_a, o_b, o_c, o_d, g.reshape(1, -1))


def _mm_kernel(*refs, nk, act, has_res, emit_norm):
    refs = list(refs)
    a_ref, w_ref = refs.pop(0), refs.pop(0)
    res_ref = refs.pop(0) if has_res else None
    gain_ref = refs.pop(0) if emit_norm else None
    o_ref = refs.pop(0)
    xg_ref, ssq_ref = (refs.pop(0), refs.pop(0)) if emit_norm else (None, None)

    def finish(acc):
        if act == "relu2":
            acc = jnp.square(jnp.maximum(acc, 0.0))
        if has_res:
            acc = res_ref[...] + acc
        o_ref[...] = acc.astype(o_ref.dtype)

    if nk == 1:
        finish(jnp.dot(a_ref[...], w_ref[...], preferred_element_type=F32))
    elif act is None and o_ref.dtype == F32:
        k = pl.program_id(2)

        @pl.when(k == 0)
        def _():
            o_ref[...] = res_ref[...] if has_res else jnp.zeros_like(o_ref)

        o_ref[...] += jnp.dot(a_ref[...], w_ref[...], preferred_element_type=F32)

        if emit_norm:
            @pl.when(k == nk - 1)
            def _():
                x = o_ref[...]
                xg_ref[...] = (x * gain_ref[...]).astype(BF16)
                ssq_ref[...] = jnp.sum(x * x, axis=1, keepdims=True)
    else:
        acc_ref = refs[0]
        k = pl.program_id(2)

        @pl.when(k == 0)
        def _():
            acc_ref[...] = jnp.zeros_like(acc_ref)

        acc_ref[...] += jnp.dot(a_ref[...], w_ref[...], preferred_element_type=F32)

        @pl.when(k == nk - 1)
        def _():
            finish(acc_ref[...])


def _pick(n, prefs):
    for p in prefs:
        if n % p == 0:
            return p
    return n


def matmul(a, w, out_dtype, act=None, res=None, layer=None, emit_norm_gain=None, tm=None, tn=None, tk=None,
           name="matmul"):
    m, kdim = a.shape
    n = w.shape[-1]
    tm = tm or _pick(m, (1024, 512, 256))
    tn = tn or _pick(n, (1024, 768, 512, 256, 128))
    has_res = res is not None
    tk = tk or _pick(kdim, (4096, 2048, 1024))
    nk = kdim // tk
    out_is_acc = nk > 1 and act is None and out_dtype == F32
    emit_norm = emit_norm_gain is not None
    assert out_is_acc or not emit_norm
    w_spec = (pl.BlockSpec((tk, tn), lambda i, j, k: (k, j)) if layer is None else
              pl.BlockSpec((None, tk, tn), lambda i, j, k: (layer, k, j)))
    in_specs = [pl.BlockSpec((tm, tk), lambda i, j, k: (i, k)), w_spec]
    args = [a, w]
    if has_res:
        in_specs.append(pl.BlockSpec((tm, tn), lambda i, j, k: (i, j)))
        args.append(res)
    out_specs = [pl.BlockSpec((tm, tn), lambda i, j, k: (i, j))]
    out_shape = [jax.ShapeDtypeStruct((m, n), out_dtype)]
    if emit_norm:
        in_specs.append(pl.BlockSpec((1, tn), lambda i, j, k: (0, j)))
        args.append(emit_norm_gain.reshape(1, n))
        out_specs += [pl.BlockSpec((tm, tn), lambda i, j, k: (i, j)),
                      pl.BlockSpec((None, tm, 1), lambda i, j, k: (j, i, 0))]
        out_shape += [jax.ShapeDtypeStruct((m, n), BF16), jax.ShapeDtypeStruct((n // tn, m, 1), F32)]
    scratch = [pltpu.VMEM((tm, tn), F32)] if (nk > 1 and not out_is_acc) else []
    out = pl.pallas_call(
        functools.partial(_mm_kernel, nk=nk, act=act, has_res=has_res, emit_norm=emit_norm),
        grid=(m // tm, n // tn, nk),
        in_specs=in_specs,
        out_specs=out_specs,
        out_shape=out_shape,
        scratch_shapes=scratch,
        compiler_params=_cparams(("parallel", "parallel", "arbitrary")),
        name=name,
    )(*args)
    return out[0] if len(out) == 1 else tuple(out)


def _mmw_kernel(*refs, layer, col0, tn, nj, transposed, act, has_res, rope_half, norm_dim, emit_norm, has_side,
                row_chunk):
    refs = list(refs)
    a_ref, w_hbm = refs.pop(0), refs.pop(0)
    res_ref = refs.pop(0) if has_res else None
    tab_refs = [refs.pop(0) for _ in range(3)] if rope_half else None
    ssq_in_ref = refs.pop(0) if norm_dim else None
    gain_ref = refs.pop(0) if emit_norm else None
    side_in_ref = refs.pop(0) if has_side else None
    o_ref = refs.pop(0)
    xg_ref, ssq_out_ref = (refs.pop(0), refs.pop(0)) if emit_norm else (None, None)
    side_out_ref = refs.pop(0) if has_side else None
    wf32_ref, wbf_ref, sem = refs
    j = pl.program_id(0)

    if has_side:
        side_out_ref[...] = side_in_ref[...].astype(BF16)

    def w_copy(jt):
        start = pl.multiple_of(col0 + jt * tn, 8 if transposed else LANES)
        src = w_hbm.at[layer, pl.ds(start, tn), :] if transposed else w_hbm.at[layer, :, pl.ds(start, tn)]
        return pltpu.make_async_copy(src, wf32_ref, sem.at[0])

    @pl.when(pl.program_id(1) == 0)
    def _():
        @pl.when(j == 0)
        def _():
            w_copy(0).start()

        w_copy(j).wait()

        def cast_rows(r, c):
            rows = pl.ds(pl.multiple_of(r * row_chunk, row_chunk), row_chunk)
            wbf_ref[rows, :] = wf32_ref[rows, :].astype(BF16)
            return c
        lax.fori_loop(0, wf32_ref.shape[0] // row_chunk, cast_rows, 0)

        @pl.when(j + 1 < nj)
        def _():
            w_copy(j + 1).start()

    dims = (((1,), (1,)), ((), ())) if transposed else (((1,), (0,)), ((), ()))
    acc = lax.dot_general(a_ref[...], wbf_ref[...], dims, preferred_element_type=F32)
    if norm_dim:
        acc = acc * lax.rsqrt(jnp.sum(ssq_in_ref[...], axis=0) * (1.0 / norm_dim) + EPS)
    if act == "relu2":
        acc = jnp.square(jnp.maximum(acc, 0.0))
    if rope_half:
        acc = _rope(acc, *(t[...] for t in tab_refs), rope_half)
    if has_res:
        acc = res_ref[...] + acc
    o_ref[...] = acc.astype(o_ref.dtype)
    if emit_norm:
        xg_ref[...] = (acc * gain_ref[...]).astype(BF16)
        ssq_out_ref[...] = jnp.sum(acc * acc, axis=1, keepdims=True)


def matmul_f32w(a, w_stack, layer, col0, n, out_dtype, act=None, res=None, rope=None, transposed=False,
                side_cast=None, row_ssq=None, emit_norm_gain=None, tm=1024, tn=1024, name="matmul_f32w"):
    m, kdim = a.shape
    has_res = res is not None
    assert col0 % (8 if transposed else LANES) == 0 and n % tn == 0 and m % tm == 0
    w_shape = (tn, kdim) if transposed else (kdim, tn)
    in_specs = [pl.BlockSpec((tm, kdim), lambda j, i: (i, 0)), pl.BlockSpec(memory_space=pl.ANY)]
    args = [a, w_stack]
    if has_res:
        in_specs.append(pl.BlockSpec((tm, tn), lambda j, i: (i, j)))
        args.append(res)
    rope_half = 0
    if rope is not None:
        tables, rope_half = rope
        in_specs += [pl.BlockSpec((tm, tables[0].shape[1]), lambda j, i: (i, 0))] * 3
        args += list(tables)
    if row_ssq is not None:
        in_specs.append(pl.BlockSpec((row_ssq.shape[0], tm, 1), lambda j, i: (0, i, 0)))
        args.append(row_ssq)
    out_specs = [pl.BlockSpec((tm, tn), lambda j, i: (i, j))]
    out_shape = [jax.ShapeDtypeStruct((m, n), out_dtype)]
    nj, ni = n // tn, m // tm
    if emit_norm_gain is not None:
        in_specs.append(pl.BlockSpec((1, tn), lambda j, i: (0, j)))
        args.append(emit_norm_gain.reshape(1, n))
        out_specs += [pl.BlockSpec((tm, tn), lambda j, i: (i, j)), pl.BlockSpec((None, tm, 1), lambda j, i: (j, i, 0))]
        out_shape += [jax.ShapeDtypeStruct((m, n), BF16), jax.ShapeDtypeStruct((nj, m, 1), F32)]
    if side_cast is not None:
        side_stack, side_layer = side_cast
        _, side_r, side_c = side_stack.shape
        slab = side_r // (nj * ni)
        assert slab * nj * ni == side_r and slab % 16 == 0
        in_specs.append(pl.BlockSpec((None, slab, side_c), lambda j, i: (side_layer, j * ni + i, 0)))
        args.append(side_stack)
        out_specs.append(pl.BlockSpec((slab, side_c), lambda j, i: (j * ni + i, 0)))
        out_shape.append(jax.ShapeDtypeStruct((side_r, side_c), BF16))
    out = pl.pallas_call(
        functools.partial(_mmw_kernel, layer=layer, col0=col0, tn=tn, nj=nj, transposed=transposed, act=act,
                          has_res=has_res, rope_half=rope_half, norm_dim=kdim if row_ssq is not None else 0,
                          emit_norm=emit_norm_gain is not None, has_side=side_cast is not None,
                          row_chunk=min(256, w_shape[0])),
        grid=(nj, ni),
        in_specs=in_specs,
        out_specs=out_specs,
        out_shape=out_shape,
        scratch_shapes=[pltpu.VMEM(w_shape, F32), pltpu.VMEM(w_shape, BF16), pltpu.SemaphoreType.DMA((1,))],
        compiler_params=_cparams(("arbitrary", "arbitrary")),
        name=name,
    )(*args)
    return out[0] if len(out) == 1 else tuple(out)


def _rope_tables(s_len, r, period, pass_through):
    half = r // 2
    inv = ROPE_THETA ** (-jnp.arange(half, dtype=F32) * (2.0 / r))
    ang = jnp.arange(s_len, dtype=F32)[:, None] * inv[None, :]
    cos, sin = jnp.cos(ang), jnp.sin(ang)
    fill = jnp.ones if pass_through else jnp.zeros
    zeros = jnp.zeros((s_len, period - r), F32)
    zh = jnp.zeros((s_len, half), F32)
    c = jnp.concatenate([cos, cos, fill((s_len, period - r), F32)], axis=1)
    sa = jnp.concatenate([-sin, zh, zeros], axis=1)
    sb = jnp.concatenate([zh, sin, zeros], axis=1)
    return c, sa, sb


def _rope(x, c, sa, sb, half):
    w = x.shape[-1]
    reps = w // c.shape[-1]
    if reps > 1:
        c, sa, sb = (jnp.tile(t, (1, reps)) for t in (c, sa, sb))
    return x * c + pltpu.roll(x, w - half, 1) * sa + pltpu.roll(x, half, 1) * sb


def _mla_proj_kernel(seg_ref, g_ref, w_ref, c_ref, sa_ref, sb_ref, o_ref, *, col0, width, kr_col0, kr_cols):
    half = MLA_ROPE_DIM // 2
    group = 2 * HEAD_DIM
    a = _rms(seg_ref[:, col0:col0 + width], g_ref[...]).astype(BF16)
    acc = jnp.dot(a, w_ref[...], preferred_element_type=F32)
    c, sa, sb = c_ref[...], sa_ref[...], sb_ref[...]
    if kr_cols is None:
        for lo in range(0, acc.shape[1], group):
            o_ref[:, lo:lo + HEAD_DIM] = acc[:, lo:lo + HEAD_DIM].astype(o_ref.dtype)
            o_ref[:, lo + HEAD_DIM:lo + group] = _rope(acc[:, lo + HEAD_DIM:lo + group], c, sa, sb, half).astype(o_ref.dtype)
    else:
        kr = _rope(seg_ref[:, kr_col0:kr_col0 + LANES], c, sa, sb, half)
        for lo in range(0, kr_cols, group):
            o_ref[:, lo:lo + HEAD_DIM] = acc[:, lo:lo + HEAD_DIM].astype(o_ref.dtype)
            o_ref[:, lo + HEAD_DIM:lo + group] = (acc[:, lo + HEAD_DIM:lo + group] + kr).astype(o_ref.dtype)
        o_ref[:, kr_cols:] = acc[:, kr_cols:].astype(o_ref.dtype)


def mla_proj(seg_a, col0, g, w, tables, kr_cols=None, tm=1024):
    m, seg_w = seg_a.shape
    width, n = w.shape
    tspec = pl.BlockSpec((tm, tables[0].shape[1]), lambda i: (i, 0))
    return pl.pallas_call(
        functools.partial(_mla_proj_kernel, col0=col0, width=width, kr_col0=Q_LORA_RANK + KV_LORA_RANK,
                          kr_cols=kr_cols),
        grid=(m // tm,),
        in_specs=[pl.BlockSpec((tm, seg_w), lambda i: (i, 0)),
                  pl.BlockSpec((1, width), lambda i: (0, 0)),
                  pl.BlockSpec((width, n), lambda i: (0, 0)),
                  tspec, tspec, tspec],
        out_specs=pl.BlockSpec((tm, n), lambda i: (i, 0)),
        out_shape=jax.ShapeDtypeStruct((m, n), BF16),
        compiler_params=_cparams(("parallel",)),
        name="mla_proj",
    )(seg_a, g.reshape(1, -1), w, *tables)


def _split3(x):
    hi = x.astype(BF16)
    r1 = x - hi.astype(F32)
    mid = r1.astype(BF16)
    lo = (r1 - mid.astype(F32)).astype(BF16)
    return hi, mid, lo


def _forget_kernel(x_ref, b_ref, u_ref, o_ref, carry_ref, *, out_scale):
    @pl.when(pl.program_id(0) == 0)
    def _():
        carry_ref[...] = jnp.zeros_like(carry_ref)

    xt = jnp.transpose(x_ref[...])
    y = -(xt[LANES - N_HEADS:, :] + b_ref[...])
    logf = -(jnp.maximum(y, 0.0) + jnp.log(1.0 + jnp.exp(-jnp.abs(y))))
    u = u_ref[...]
    cs = sum(jnp.dot(p, u, preferred_element_type=F32) for p in _split3(logf))
    c = cs + carry_ref[:, 0:1]
    o_ref[...] = c * out_scale
    carry_ref[...] = jnp.broadcast_to(c[:, -1:], carry_ref.shape)


def forget_cumsum(gate, b_f, tc=512, out_scale=1.0):
    m = gate.shape[0]
    u = jnp.triu(jnp.ones((tc, tc), F32)).astype(BF16)
    return pl.pallas_call(
        functools.partial(_forget_kernel, out_scale=out_scale),
        grid=(m // tc,),
        in_specs=[pl.BlockSpec((tc, LANES), lambda i: (i, 0)),
                  pl.BlockSpec((N_HEADS, 1), lambda i: (0, 0)),
                  pl.BlockSpec((tc, tc), lambda i: (0, 0))],
        out_specs=pl.BlockSpec((N_HEADS, tc), lambda i: (0, i)),
        out_shape=jax.ShapeDtypeStruct((N_HEADS, m), F32),
        scratch_shapes=[pltpu.VMEM((N_HEADS, LANES), F32)],
        compiler_params=_cparams(("arbitrary",)),
        name="forget_cumsum",
    )(gate, b_f.reshape(N_HEADS, 1), u)


def _flash_kernel(*refs, bq, bk, dqk, scale2, window_blocks, has_decay, hps):
    q_ref, k_ref, v_ref, tile_ref = refs[:4]
    ck_ref = refs[4] if has_decay else None
    o_ref = refs[4 + has_decay]
    dv = HEAD_DIM
    n_diag = bq // bk
    qi = pl.program_id(1)
    qs = [q_ref[:, g * dqk:(g + 1) * dqk] for g in range(hps)]

    def head_step(g, j, carry, tile):
        m, l, acc = carry
        start = pl.multiple_of(j * bk, bk)
        k = k_ref[pl.ds(start, bk), g * dqk:(g + 1) * dqk]
        v = v_ref[pl.ds(start, bk), g * dv:(g + 1) * dv]
        s = lax.dot_general(qs[g], k, (((1,), (1,)), ((), ())), preferred_element_type=F32) * scale2
        if has_decay:
            s = s - ck_ref[g, pl.ds(j, 1), :]
        if tile is not None:
            s = s + tile
        m_new = jnp.maximum(m, jnp.max(s, axis=-1, keepdims=True))
        alpha = jnp.exp2(m - m_new)
        p = jnp.exp2(s - m_new)
        l = alpha * l + jnp.sum(p, axis=-1, keepdims=True)
        acc = alpha * acc + jnp.dot(p.astype(BF16), v, preferred_element_type=F32)
        return m_new, l, acc

    def step(j, carry, tile):
        return tuple(head_step(g, j, carry[g], tile) for g in range(hps))

    carry = tuple((jnp.full((bq, 1), NEG_INF, F32), jnp.zeros((bq, 1), F32), jnp.zeros((bq, dv), F32))
                  for _ in range(hps))
    j_diag = qi * n_diag
    if window_blocks is None:
        lo = 0
        body = lambda j, c: step(j, c, None)
    else:
        lo = jnp.maximum(j_diag - window_blocks, 0)
        body = lambda j, c: step(j, c, tile_ref[n_diag - 1 + j_diag - j])
    carry = lax.fori_loop(lo, j_diag, body, carry)
    for t in range(n_diag):
        carry = step(j_diag + t, carry, tile_ref[n_diag - 1 - t])
    for g, (m, l, acc) in enumerate(carry):
        o_ref[:, g * dv:(g + 1) * dv] = (acc / l).astype(o_ref.dtype)


def flash_attention(q_arr, q_cb, k_arr, k_cb, v_arr, v_cb, *, dqk, scale, tiles, window_blocks=None,
                    decay=None, hps=1):
    s_len = q_arr.shape[0]
    _, bq, bk = tiles.shape
    nq, nk = s_len // bq, s_len // bk
    assert q_cb % hps == 0 and k_cb % hps == 0 and v_cb % hps == 0
    in_specs = [pl.BlockSpec((bq, hps * dqk), lambda h, i: (i, q_cb // hps + h)),
                pl.BlockSpec((s_len, hps * dqk), lambda h, i: (0, k_cb // hps + h)),
                pl.BlockSpec((s_len, hps * HEAD_DIM), lambda h, i: (0, v_cb // hps + h)),
                pl.BlockSpec(tiles.shape, lambda h, i: (0, 0, 0))]
    args = [q_arr, k_arr, v_arr, tiles]
    if decay is not None:
        in_specs.append(pl.BlockSpec((hps, nk, bk), lambda h, i: (h, 0, 0)))
        args.append(decay.reshape(N_HEADS, nk, bk))
    return pl.pallas_call(
        functools.partial(_flash_kernel, bq=bq, bk=bk, dqk=dqk, scale2=scale * LOG2E,
                          window_blocks=window_blocks, has_decay=decay is not None, hps=hps),
        grid=(N_HEADS // hps, nq),
        in_specs=in_specs,
        out_specs=pl.BlockSpec((bq, hps * HEAD_DIM), lambda h, i: (i, h)),
        out_shape=jax.ShapeDtypeStruct((s_len, GROUP_WIDTH), BF16),
        compiler_params=_cparams(("parallel", "arbitrary")),
        name="flash_attention",
    )(*args)


def _score_tiles(bq, bk, count_fn, window):
    n_diag = bq // bk
    window_blocks = 0 if window is None else (window + bk - 1) // bk
    rel = jnp.arange(-(n_diag - 1), window_blocks + 1)
    d = rel[:, None, None] * bk + jnp.arange(bq)[None, :, None] - jnp.arange(bk)[None, None, :]
    count = jnp.where(d >= 0, count_fn(d), 0.0)
    tiles = jnp.where(count > 0, jnp.log2(jnp.maximum(count, 1.0)), NEG_INF).astype(F32)
    return tiles, (None if window is None else window_blocks)


def _causal_tiles(bq, bk):
    return _score_tiles(bq, bk, lambda d: jnp.ones(d.shape, F32), None)[0]


def _dilated_tiles(bq, bk):
    count_fn = lambda d: sum(((d <= w) & (d % dil == 0)).astype(F32) for w, dil in DILATED_PAIRS)
    return _score_tiles(bq, bk, count_fn, max(w for w, _ in DILATED_PAIRS))


def _stick_kernel(q_ref, k_ref, v_ref, t_ref, o_ref, *, bq, bk, scale, hps):
    qi = pl.program_id(1)
    d = HEAD_DIM
    n_diag = bq // bk
    qs = [q_ref[:, g * d:(g + 1) * d] for g in range(hps)]
    tri = t_ref[...]

    def head_block(g, j, r, acc, diag_t):
        start = pl.multiple_of(j * bk, bk)
        k = k_ref[pl.ds(start, bk), g * d:(g + 1) * d]
        v = v_ref[pl.ds(start, bk), g * d:(g + 1) * d]
        z = lax.dot_general(qs[g], k, (((1,), (1,)), ((), ())), preferred_element_type=F32) * scale
        sp = jnp.maximum(z, 0.0) + jnp.log(1.0 + jnp.exp(-jnp.abs(z)))
        log_1m = -sp
        if diag_t is not None:
            mask = (lax.broadcasted_iota(jnp.int32, (bq, bk), 1) + diag_t * bk
                    < lax.broadcasted_iota(jnp.int32, (bq, bk), 0))
            log_1m = jnp.where(mask, log_1m, 0.0)
        hi = log_1m.astype(BF16)
        lo = (log_1m - hi.astype(F32)).astype(BF16)
        after = (jnp.dot(hi, tri, preferred_element_type=F32) + jnp.dot(lo, tri, preferred_element_type=F32)) + r
        a = jnp.exp((z - sp) + after)
        if diag_t is not None:
            a = jnp.where(mask, a, 0.0)
        acc = acc + jnp.dot(a.astype(BF16), v, preferred_element_type=F32)
        r = r + jnp.sum(log_1m, axis=-1, keepdims=True)
        return r, acc

    def block(j, state, diag_t):
        return tuple(head_block(g, j, *state[g], diag_t) for g in range(hps))

    state = tuple((jnp.zeros((bq, 1), F32), jnp.zeros((bq, d), F32)) for _ in range(hps))
    j_diag = qi * n_diag
    for t in reversed(range(n_diag)):
        state = block(j_diag + t, state, t)

    def cond(c):
        j, state = c
        r_max = functools.reduce(jnp.maximum, [jnp.max(r) for r, _ in state])
        return jnp.logical_and(j >= 0, r_max > -F32_EXP_UNDERFLOW)

    def body(c):
        j, state = c
        return j - 1, block(j, state, None)

    _, state = lax.while_loop(cond, body, (j_diag - 1, state))
    for g, (_, acc) in enumerate(state):
        o_ref[:, g * d:(g + 1) * d] = acc.astype(o_ref.dtype)


def stick_breaking_attention(qkv, q_cb, k_cb, v_cb, *, scale, bq=512, bk=256, hps=4):
    s_len = qkv.shape[0]
    tri = jnp.tril(jnp.ones((bk, bk), F32), -1).astype(BF16)
    w = hps * HEAD_DIM
    assert q_cb % hps == 0 and k_cb % hps == 0 and v_cb % hps == 0
    return pl.pallas_call(
        functools.partial(_stick_kernel, bq=bq, bk=bk, scale=scale, hps=hps),
        grid=(N_HEADS // hps, s_len // bq),
        in_specs=[pl.BlockSpec((bq, w), lambda h, i: (i, q_cb // hps + h)),
                  pl.BlockSpec((s_len, w), lambda h, i: (0, k_cb // hps + h)),
                  pl.BlockSpec((s_len, w), lambda h, i: (0, v_cb // hps + h)),
                  pl.BlockSpec((bk, bk), lambda h, i: (0, 0))],
        out_specs=pl.BlockSpec((bq, w), lambda h, i: (i, h)),
        out_shape=jax.ShapeDtypeStruct((s_len, GROUP_WIDTH), BF16),
        compiler_params=_cparams(("parallel", "arbitrary")),
        name="stick_breaking",
    )(qkv, qkv, qkv, tri)


def _prep_mla_weights(w_uq, w_uk, w_uv):
    pad = 2 * HEAD_DIM
    wq = w_uq.reshape(Q_LORA_RANK, N_HEADS, HEAD_DIM + MLA_ROPE_DIM)
    w_q = jnp.pad(wq, ((0, 0), (0, 0), (0, pad - wq.shape[2]))).reshape(Q_LORA_RANK, N_HEADS * pad)
    wk = w_uk.reshape(KV_LORA_RANK, N_HEADS, HEAD_DIM)
    w_k = jnp.pad(wk, ((0, 0), (0, 0), (0, pad - HEAD_DIM))).reshape(KV_LORA_RANK, N_HEADS * pad)
    return w_q.astype(BF16), jnp.concatenate([w_k, w_uv], axis=1).astype(BF16)


def kernel(x, g_attn, w_in, g_q, g_kv, w_uq, w_uk, w_uv, b_f, g_out, w_o, g_mlp, w_up, w_down, g_final):
    b, s_len, d_model = x.shape
    assert b == 1 and s_len % 2048 == 0
    depth = w_in.shape[0]
    xf = x.reshape(s_len, d_model)

    mla_tables = _rope_tables(s_len, MLA_ROPE_DIM, LANES, pass_through=False)
    dil_tables = _rope_tables(s_len, PARTIAL_ROPE_DIM, HEAD_DIM, pass_through=True)
    causal = _causal_tiles(FLASH_BQ, FLASH_BK)
    dil_tiles, dil_blocks = _dilated_tiles(DILATED_BQ, DILATED_BK)
    scale = HEAD_DIM ** -0.5
    mla_scale = (HEAD_DIM + MLA_ROPE_DIM) ** -0.5
    nh = N_HEADS
    n_a = Q_LORA_RANK + KV_LORA_RANK + MLA_ROPE_DIM
    n_a128 = -(-n_a // LANES) * LANES
    gate_col0 = w_in.shape[2] - LANES
    w_in_t = jnp.swapaxes(w_in, 1, 2)

    h = rmsnorm(xf, g_attn[0], BF16)
    h_ssq = None
    for l in range(depth):
        w_q, w_kv = _prep_mla_weights(w_uq[l], w_uk[l], w_uv[l])

        seg_a = matmul_f32w(h, w_in_t, l, 0, n_a128, F32, transposed=True, row_ssq=h_ssq, tn=n_a128 // 2)
        bqk_r = matmul_f32w(h, w_in_t, l, n_a, 2 * GROUP_WIDTH, BF16, transposed=True,
                            row_ssq=h_ssq, rope=(dil_tables, PARTIAL_ROPE_DIM // 2))
        rest = matmul_f32w(h, w_in_t, l, n_a + 2 * GROUP_WIDTH, 7 * GROUP_WIDTH, BF16, transposed=True,
                           row_ssq=h_ssq)
        gate = matmul_f32w(h, w_in_t, l, gate_col0, LANES, F32, transposed=True, row_ssq=h_ssq,
                           tn=LANES)

        q_cat = mla_proj(seg_a, 0, g_q[l], w_q, mla_tables)
        kv_cat = mla_proj(seg_a, Q_LORA_RANK, g_kv[l], w_kv, mla_tables,
                          kr_cols=N_HEADS * 2 * HEAD_DIM)
        o_a = flash_attention(q_cat, 0, kv_cat, 0, kv_cat, 2 * nh, dqk=2 * HEAD_DIM, scale=mla_scale, tiles=causal,
                              hps=2)

        o_b = flash_attention(bqk_r, 0, bqk_r, nh, rest, 0, dqk=HEAD_DIM, scale=scale, tiles=dil_tiles,
                              window_blocks=dil_blocks, hps=2)

        o_c = stick_breaking_attention(rest, nh, 2 * nh, 3 * nh, scale=scale)

        c_f = forget_cumsum(gate, b_f[l], out_scale=LOG2E)
        o_d = flash_attention(rest, 4 * nh, rest, 5 * nh, rest, 6 * nh, dqk=HEAD_DIM, scale=scale, tiles=causal,
                              decay=c_f, hps=2)

        mix = group_norm_concat(o_a, o_b, o_c, o_d, g_out[l])
        xf, xg, ssq = matmul_f32w(mix, w_o, l, 0, d_model, F32, res=xf, emit_norm_gain=g_mlp[l], tm=512)
        ssq = jnp.sum(ssq, axis=0, keepdims=True)
        u, w_down_bf = matmul_f32w(xg, w_up, l, 0, w_up.shape[2], BF16, act="relu2", row_ssq=ssq,
                                   side_cast=(w_down, l))
        if l + 1 < depth:
            xf, h, h_ssq = matmul(u, w_down_bf, F32, res=xf, emit_norm_gain=g_attn[l + 1])
            h_ssq = jnp.sum(h_ssq, axis=0, keepdims=True)
        else:
            xf = matmul(u, w_down_bf, F32, res=xf)

    return rmsnorm(xf, g_final, F32).reshape(b, s_len, d_model)
```

```python
import functools
import math

import jax
import jax.numpy as jnp
from jax import lax
from jax.experimental import pallas as pl
from jax.experimental.pallas import tpu as pltpu

F32 = jnp.float32
BF16 = jnp.bfloat16

HEAD_DIM = 128
N_HEADS = 8
GROUP_WIDTH = N_HEADS * HEAD_DIM
Q_LORA_RANK = 896
KV_LORA_RANK = 512
MLA_ROPE_DIM = 64
ROPE_THETA = 500000.0
PARTIAL_ROPE_DIM = HEAD_DIM // 4
DILATED_PAIRS = ((128, 1), (512, 4), (2048, 16))
EPS = 1e-6
NEG_INF = -1e30

LANES = 128
VMEM_LIMIT_BYTES = 62 * 1024 * 1024

F32_EXP_UNDERFLOW = 104.0
LOG2E = math.log2(math.e)

FLASH_BQ, FLASH_BK = 1024, 1024
DILATED_BQ, DILATED_BK = 512, 512
CAST_ROW_CHUNK = 256


def _cparams(semantics):
    return pltpu.CompilerParams(dimension_semantics=semantics, vmem_limit_bytes=VMEM_LIMIT_BYTES)


def _rms(x, g):
    return x * lax.rsqrt(jnp.mean(x * x, axis=-1, keepdims=True) + EPS) * g


def _rmsnorm_kernel(x_ref, g_ref, o_ref):
    o_ref[...] = _rms(x_ref[...], g_ref[...]).astype(o_ref.dtype)


def rmsnorm(x, g, out_dtype, tm=512):
    m, d = x.shape
    return pl.pallas_call(
        _rmsnorm_kernel,
        grid=(m // tm,),
        in_specs=[pl.BlockSpec((tm, d), lambda i: (i, 0)), pl.BlockSpec((1, d), lambda i: (0, 0))],
        out_specs=pl.BlockSpec((tm, d), lambda i: (i, 0)),
        out_shape=jax.ShapeDtypeStruct((m, d), out_dtype),
        compiler_params=_cparams(("parallel",)),
        name="rmsnorm",
    )(x, g.reshape(1, d))


def _group_norm_kernel(a_ref, b_ref, c_ref, d_ref, g_ref, o_ref):
    for gi, r in enumerate((a_ref, b_ref, c_ref, d_ref)):
        sl = slice(gi * GROUP_WIDTH, (gi + 1) * GROUP_WIDTH)
        o_ref[:, sl] = _rms(r[...].astype(F32), g_ref[:, sl]).astype(o_ref.dtype)


def group_norm_concat(o_a, o_b, o_c, o_d, g, tm=512):
    m = o_a.shape[0]
    spec = pl.BlockSpec((tm, GROUP_WIDTH), lambda i: (i, 0))
    return pl.pallas_call(
        _group_norm_kernel,
        grid=(m // tm,),
        in_specs=[spec, spec, spec, spec, pl.BlockSpec((1, 4 * GROUP_WIDTH), lambda i: (0, 0))],
        out_specs=pl.BlockSpec((tm, 4 * GROUP_WIDTH), lambda i: (i, 0)),
        out_shape=jax.ShapeDtypeStruct((m, 4 * GROUP_WIDTH), BF16),
        compiler_params=_cparams(("parallel",)),
        name="group_norm",
    )(o_a, o_b, o_c, o_d, g.reshape(1, -1))


def _mm_kernel(*refs, nk, act, has_res, emit_norm):
    refs = list(refs)
    a_ref, w_ref = refs.pop(0), refs.pop(0)
    res_ref = refs.pop(0) if has_res else None
    gain_ref = refs.pop(0) if emit_norm else None
    o_ref = refs.pop(0)
    xg_ref, ssq_ref = (refs.pop(0), refs.pop(0)) if emit_norm else (None, None)

    def finish(acc):
        if act == "relu2":
            acc = jnp.square(jnp.maximum(acc, 0.0))
        if has_res:
            acc = res_ref[...] + acc
        o_ref[...] = acc.astype(o_ref.dtype)

    if nk == 1:
        finish(jnp.dot(a_ref[...], w_ref[...], preferred_element_type=F32))
    elif act is None and o_ref.dtype == F32:
        k = pl.program_id(2)

        @pl.when(k == 0)
        def _():
            o_ref[...] = res_ref[...] if has_res else jnp.zeros_like(o_ref)

        o_ref[...] += jnp.dot(a_ref[...], w_ref[...], preferred_element_type=F32)

        if emit_norm:
            @pl.when(k == nk - 1)
            def _():
                x = o_ref[...]
                xg_ref[...] = (x * gain_ref[...]).astype(BF16)
                ssq_ref[...] = jnp.sum(x * x, axis=1, keepdims=True)
    else:
        acc_ref = refs[0]
        k = pl.program_id(2)

        @pl.when(k == 0)
        def _():
            acc_ref[...] = jnp.zeros_like(acc_ref)

        acc_ref[...] += jnp.dot(a_ref[...], w_ref[...], preferred_element_type=F32)

        @pl.when(k == nk - 1)
        def _():
            finish(acc_ref[...])


def _pick(n, prefs):
    for p in prefs:
        if n % p == 0:
            return p
    return n


def matmul(a, w, out_dtype, act=None, res=None, layer=None, emit_norm_gain=None, tm=None, tn=None, tk=None,
           name="matmul"):
    m, kdim = a.shape
    n = w.shape[-1]
    tm = tm or _pick(m, (1024, 512, 256))
    tn = tn or _pick(n, (1024, 768, 512, 256, 128))
    has_res = res is not None
    tk = tk or _pick(kdim, (4096, 2048, 1024))
    nk = kdim // tk
    out_is_acc = nk > 1 and act is None and out_dtype == F32
    emit_norm = emit_norm_gain is not None
    assert out_is_acc or not emit_norm
    w_spec = (pl.BlockSpec((tk, tn), lambda i, j, k: (k, j)) if layer is None else
              pl.BlockSpec((None, tk, tn), lambda i, j, k: (layer, k, j)))
    in_specs = [pl.BlockSpec((tm, tk), lambda i, j, k: (i, k)), w_spec]
    args = [a, w]
    if has_res:
        in_specs.append(pl.BlockSpec((tm, tn), lambda i, j, k: (i, j)))
        args.append(res)
    out_specs = [pl.BlockSpec((tm, tn), lambda i, j, k: (i, j))]
    out_shape = [jax.ShapeDtypeStruct((m, n), out_dtype)]
    if emit_norm:
        in_specs.append(pl.BlockSpec((1, tn), lambda i, j, k: (0, j)))
        args.append(emit_norm_gain.reshape(1, n))
        out_specs += [pl.BlockSpec((tm, tn), lambda i, j, k: (i, j)),
                      pl.BlockSpec((None, tm, 1), lambda i, j, k: (j, i, 0))]
        out_shape += [jax.ShapeDtypeStruct((m, n), BF16), jax.ShapeDtypeStruct((n // tn, m, 1), F32)]
    scratch = [pltpu.VMEM((tm, tn), F32)] if (nk > 1 and not out_is_acc) else []
    out = pl.pallas_call(
        functools.partial(_mm_kernel, nk=nk, act=act, has_res=has_res, emit_norm=emit_norm),
        grid=(m // tm, n // tn, nk),
        in_specs=in_specs,
        out_specs=out_specs,
        out_shape=out_shape,
        scratch_shapes=scratch,
        compiler_params=_cparams(("parallel", "parallel", "arbitrary")),
        name=name,
    )(*args)
    return out[0] if len(out) == 1 else tuple(out)


def _mmw_kernel(*refs, layer, col0, tn, nj, transposed, act, has_res, rope_half, norm_dim, emit_norm, has_side,
                row_chunk):
    refs = list(refs)
    a_ref, w_hbm = refs.pop(0), refs.pop(0)
    res_ref = refs.pop(0) if has_res else None
    tab_refs = [refs.pop(0) for _ in range(3)] if rope_half else None
    ssq_in_ref = refs.pop(0) if norm_dim else None
    gain_ref = refs.pop(0) if emit_norm else None
    side_in_ref = refs.pop(0) if has_side else None
    o_ref = refs.pop(0)
    xg_ref, ssq_out_ref = (refs.pop(0), refs.pop(0)) if emit_norm else (None, None)
    side_out_ref = refs.pop(0) if has_side else None
    wf32_ref, wbf_ref, sem = refs
    j = pl.program_id(0)

    if has_side:
        side_out_ref[...] = side_in_ref[...].astype(BF16)

    def w_copy(jt):
        start = pl.multiple_of(col0 + jt * tn, 8 if transposed else LANES)
        src = w_hbm.at[layer, pl.ds(start, tn), :] if transposed else w_hbm.at[layer, :, pl.ds(start, tn)]
        return pltpu.make_async_copy(src, wf32_ref, sem.at[0])

    @pl.when(pl.program_id(1) == 0)
    def _():
        @pl.when(j == 0)
        def _():
            w_copy(0).start()

        w_copy(j).wait()

        def cast_rows(r, c):
            rows = pl.ds(pl.multiple_of(r * row_chunk, row_chunk), row_chunk)
            wbf_ref[rows, :] = wf32_ref[rows, :].astype(BF16)
            return c
        lax.fori_loop(0, wf32_ref.shape[0] // row_chunk, cast_rows, 0)

        @pl.when(j + 1 < nj)
        def _():
            w_copy(j + 1).start()

    dims = (((1,), (1,)), ((), ())) if transposed else (((1,), (0,)), ((), ()))
    acc = lax.dot_general(a_ref[...], wbf_ref[...], dims, preferred_element_type=F32)
    if norm_dim:
        acc = acc * lax.rsqrt(jnp.sum(ssq_in_ref[...], axis=0) * (1.0 / norm_dim) + EPS)
    if act == "relu2":
        acc = jnp.square(jnp.maximum(acc, 0.0))
    if rope_half:
        acc = _rope(acc, *(t[...] for t in tab_refs), rope_half)
    if has_res:
        acc = res_ref[...] + acc
    o_ref[...] = acc.astype(o_ref.dtype)
    if emit_norm:
        xg_ref[...] = (acc * gain_ref[...]).astype(BF16)
        ssq_out_ref[...] = jnp.sum(acc * acc, axis=1, keepdims=True)


def matmul_f32w(a, w_stack, layer, col0, n, out_dtype, act=None, res=None, rope=None, transposed=False,
                side_cast=None, row_ssq=None, emit_norm_gain=None, tm=1024, tn=1024, name="matmul_f32w"):
    m, kdim = a.shape
    has_res = res is not None
    assert col0 % (8 if transposed else LANES) == 0 and n % tn == 0 and m % tm == 0
    w_shape = (tn, kdim) if transposed else (kdim, tn)
    in_specs = [pl.BlockSpec((tm, kdim), lambda j, i: (i, 0)), pl.BlockSpec(memory_space=pl.ANY)]
    args = [a, w_stack]
    if has_res:
        in_specs.append(pl.BlockSpec((tm, tn), lambda j, i: (i, j)))
        args.append(res)
    rope_half = 0
    if rope is not None:
        tables, rope_half = rope
        in_specs += [pl.BlockSpec((tm, tables[0].shape[1]), lambda j, i: (i, 0))] * 3
        args += list(tables)
    if row_ssq is not None:
        in_specs.append(pl.BlockSpec((row_ssq.shape[0], tm, 1), lambda j, i: (0, i, 0)))
        args.append(row_ssq)
    out_specs = [pl.BlockSpec((tm, tn), lambda j, i: (i, j))]
    out_shape = [jax.ShapeDtypeStruct((m, n), out_dtype)]
    nj, ni = n // tn, m // tm
    if emit_norm_gain is not None:
        in_specs.append(pl.BlockSpec((1, tn), lambda j, i: (0, j)))
        args.append(emit_norm_gain.reshape(1, n))
        out_specs += [pl.BlockSpec((tm, tn), lambda j, i: (i, j)), pl.BlockSpec((None, tm, 1), lambda j, i: (j, i, 0))]
        out_shape += [jax.ShapeDtypeStruct((m, n), BF16), jax.ShapeDtypeStruct((nj, m, 1), F32)]
    if side_cast is not None:
        side_stack, side_layer = side_cast
        _, side_r, side_c = side_stack.shape
        slab = side_r // (nj * ni)
        assert slab * nj * ni == side_r and slab % 16 == 0
        in_specs.append(pl.BlockSpec((None, slab, side_c), lambda j, i: (side_layer, j * ni + i, 0)))
        args.append(side_stack)
        out_specs.append(pl.BlockSpec((slab, side_c), lambda j, i: (j * ni + i, 0)))
        out_shape.append(jax.ShapeDtypeStruct((side_r, side_c), BF16))
    out = pl.pallas_call(
        functools.partial(_mmw_kernel, layer=layer, col0=col0, tn=tn, nj=nj, transposed=transposed, act=act,
                          has_res=has_res, rope_half=rope_half, norm_dim=kdim if row_ssq is not None else 0,
                          emit_norm=emit_norm_gain is not None, has_side=side_cast is not None,
                          row_chunk=min(CAST_ROW_CHUNK, w_shape[0])),
        grid=(nj, ni),
        in_specs=in_specs,
        out_specs=out_specs,
        out_shape=out_shape,
        scratch_shapes=[pltpu.VMEM(w_shape, F32), pltpu.VMEM(w_shape, BF16), pltpu.SemaphoreType.DMA((1,))],
        compiler_params=_cparams(("arbitrary", "arbitrary")),
        name=name,
    )(*args)
    return out[0] if len(out) == 1 else tuple(out)


def _rope_tables(s_len, r, period, pass_through):
    half = r // 2
    inv = ROPE_THETA ** (-jnp.arange(half, dtype=F32) * (2.0 / r))
    ang = jnp.arange(s_len, dtype=F32)[:, None] * inv[None, :]
    cos, sin = jnp.cos(ang), jnp.sin(ang)
    fill = jnp.ones if pass_through else jnp.zeros
    zeros = jnp.zeros((s_len, period - r), F32)
    zh = jnp.zeros((s_len, half), F32)
    c = jnp.concatenate([cos, cos, fill((s_len, period - r), F32)], axis=1)
    sa = jnp.concatenate([-sin, zh, zeros], axis=1)
    sb = jnp.concatenate([zh, sin, zeros], axis=1)
    return c, sa, sb


def _rope(x, c, sa, sb, half):
    w = x.shape[-1]
    reps = w // c.shape[-1]
    if reps > 1:
        c, sa, sb = (jnp.tile(t, (1, reps)) for t in (c, sa, sb))
    return x * c + pltpu.roll(x, w - half, 1) * sa + pltpu.roll(x, half, 1) * sb


def _mla_proj_kernel(seg_ref, g_ref, w_ref, c_ref, sa_ref, sb_ref, o_ref, *, col0, width, kr_col0, kr_cols):
    half = MLA_ROPE_DIM // 2
    group = 2 * HEAD_DIM
    a = _rms(seg_ref[:, col0:col0 + width], g_ref[...]).astype(BF16)
    acc = jnp.dot(a, w_ref[...], preferred_element_type=F32)
    c, sa, sb = c_ref[...], sa_ref[...], sb_ref[...]
    if kr_cols is None:
        for lo in range(0, acc.shape[1], group):
            o_ref[:, lo:lo + HEAD_DIM] = acc[:, lo:lo + HEAD_DIM].astype(o_ref.dtype)
            o_ref[:, lo + HEAD_DIM:lo + group] = _rope(acc[:, lo + HEAD_DIM:lo + group], c, sa, sb, half).astype(o_ref.dtype)
    else:
        kr = _rope(seg_ref[:, kr_col0:kr_col0 + LANES], c, sa, sb, half)
        for lo in range(0, kr_cols, group):
            o_ref[:, lo:lo + HEAD_DIM] = acc[:, lo:lo + HEAD_DIM].astype(o_ref.dtype)
            o_ref[:, lo + HEAD_DIM:lo + group] = (acc[:, lo + HEAD_DIM:lo + group] + kr).astype(o_ref.dtype)
        o_ref[:, kr_cols:] = acc[:, kr_cols:].astype(o_ref.dtype)


def mla_proj(seg_a, col0, g, w, tables, kr_cols=None, tm=1024):
    m, seg_w = seg_a.shape
    width, n = w.shape
    tspec = pl.BlockSpec((tm, tables[0].shape[1]), lambda i: (i, 0))
    return pl.pallas_call(
        functools.partial(_mla_proj_kernel, col0=col0, width=width, kr_col0=Q_LORA_RANK + KV_LORA_RANK,
                          kr_cols=kr_cols),
        grid=(m // tm,),
        in_specs=[pl.BlockSpec((tm, seg_w), lambda i: (i, 0)),
                  pl.BlockSpec((1, width), lambda i: (0, 0)),
                  pl.BlockSpec((width, n), lambda i: (0, 0)),
                  tspec, tspec, tspec],
        out_specs=pl.BlockSpec((tm, n), lambda i: (i, 0)),
        out_shape=jax.ShapeDtypeStruct((m, n), BF16),
        compiler_params=_cparams(("parallel",)),
        name="mla_proj",
    )(seg_a, g.reshape(1, -1), w, *tables)


def _split3(x):
    hi = x.astype(BF16)
    r1 = x - hi.astype(F32)
    mid = r1.astype(BF16)
    lo = (r1 - mid.astype(F32)).astype(BF16)
    return hi, mid, lo


def _forget_kernel(x_ref, b_ref, u_ref, o_ref, carry_ref, *, out_scale):
    @pl.when(pl.program_id(0) == 0)
    def _():
        carry_ref[...] = jnp.zeros_like(carry_ref)

    xt = jnp.transpose(x_ref[...])
    y = -(xt[LANES - N_HEADS:, :] + b_ref[...])
    logf = -(jnp.maximum(y, 0.0) + jnp.log(1.0 + jnp.exp(-jnp.abs(y))))
    u = u_ref[...]
    cs = sum(jnp.dot(p, u, preferred_element_type=F32) for p in _split3(logf))
    c = cs + carry_ref[:, 0:1]
    o_ref[...] = c * out_scale
    carry_ref[...] = jnp.broadcast_to(c[:, -1:], carry_ref.shape)


def forget_cumsum(gate, b_f, tc=512, out_scale=1.0):
    m = gate.shape[0]
    u = jnp.triu(jnp.ones((tc, tc), F32)).astype(BF16)
    return pl.pallas_call(
        functools.partial(_forget_kernel, out_scale=out_scale),
        grid=(m // tc,),
        in_specs=[pl.BlockSpec((tc, LANES), lambda i: (i, 0)),
                  pl.BlockSpec((N_HEADS, 1), lambda i: (0, 0)),
                  pl.BlockSpec((tc, tc), lambda i: (0, 0))],
        out_specs=pl.BlockSpec((N_HEADS, tc), lambda i: (0, i)),
        out_shape=jax.ShapeDtypeStruct((N_HEADS, m), F32),
        scratch_shapes=[pltpu.VMEM((N_HEADS, LANES), F32)],
        compiler_params=_cparams(("arbitrary",)),
        name="forget_cumsum",
    )(gate, b_f.reshape(N_HEADS, 1), u)


def _flash_kernel(*refs, bq, bk, dqk, scale2, window_blocks, has_decay, hps):
    q_ref, k_ref, v_ref, tile_ref = refs[:4]
    ck_ref = refs[4] if has_decay else None
    o_ref = refs[4 + has_decay]
    dv = HEAD_DIM
    n_diag = bq // bk
    qi = pl.program_id(1)
    qs = [q_ref[:, g * dqk:(g + 1) * dqk] for g in range(hps)]

    def head_step(g, j, carry, tile):
        m, l, acc = carry
        start = pl.multiple_of(j * bk, bk)
        k = k_ref[pl.ds(start, bk), g * dqk:(g + 1) * dqk]
        v = v_ref[pl.ds(start, bk), g * dv:(g + 1) * dv]
        s = lax.dot_general(qs[g], k, (((1,), (1,)), ((), ())), preferred_element_type=F32) * scale2
        if has_decay:
            s = s - ck_ref[g, pl.ds(j, 1), :]
        if tile is not None:
            s = s + tile
        m_new = jnp.maximum(m, jnp.max(s, axis=-1, keepdims=True))
        alpha = jnp.exp2(m - m_new)
        p = jnp.exp2(s - m_new)
        l = alpha * l + jnp.sum(p, axis=-1, keepdims=True)
        acc = alpha * acc + jnp.dot(p.astype(BF16), v, preferred_element_type=F32)
        return m_new, l, acc

    def step(j, carry, tile):
        return tuple(head_step(g, j, carry[g], tile) for g in range(hps))

    carry = tuple((jnp.full((bq, 1), NEG_INF, F32), jnp.zeros((bq, 1), F32), jnp.zeros((bq, dv), F32))
                  for _ in range(hps))
    j_diag = qi * n_diag
    if window_blocks is None:
        lo = 0
        body = lambda j, c: step(j, c, None)
    else:
        lo = jnp.maximum(j_diag - window_blocks, 0)
        body = lambda j, c: step(j, c, tile_ref[n_diag - 1 + j_diag - j])
    carry = lax.fori_loop(lo, j_diag, body, carry)
    for t in range(n_diag):
        carry = step(j_diag + t, carry, tile_ref[n_diag - 1 - t])
    for g, (m, l, acc) in enumerate(carry):
        o_ref[:, g * dv:(g + 1) * dv] = (acc / l).astype(o_ref.dtype)


def flash_attention(q_arr, q_cb, k_arr, k_cb, v_arr, v_cb, *, dqk, scale, tiles, window_blocks=None,
                    decay=None, hps=1):
    s_len = q_arr.shape[0]
    _, bq, bk = tiles.shape
    nq, nk = s_len // bq, s_len // bk
    assert q_cb % hps == 0 and k_cb % hps == 0 and v_cb % hps == 0
    in_specs = [pl.BlockSpec((bq, hps * dqk), lambda h, i: (i, q_cb // hps + h)),
                pl.BlockSpec((s_len, hps * dqk), lambda h, i: (0, k_cb // hps + h)),
                pl.BlockSpec((s_len, hps * HEAD_DIM), lambda h, i: (0, v_cb // hps + h)),
                pl.BlockSpec(tiles.shape, lambda h, i: (0, 0, 0))]
    args = [q_arr, k_arr, v_arr, tiles]
    if decay is not None:
        in_specs.append(pl.BlockSpec((hps, nk, bk), lambda h, i: (h, 0, 0)))
        args.append(decay.reshape(N_HEADS, nk, bk))
    return pl.pallas_call(
        functools.partial(_flash_kernel, bq=bq, bk=bk, dqk=dqk, scale2=scale * LOG2E,
                          window_blocks=window_blocks, has_decay=decay is not None, hps=hps),
        grid=(N_HEADS // hps, nq),
        in_specs=in_specs,
        out_specs=pl.BlockSpec((bq, hps * HEAD_DIM), lambda h, i: (i, h)),
        out_shape=jax.ShapeDtypeStruct((s_len, GROUP_WIDTH), BF16),
        compiler_params=_cparams(("parallel", "arbitrary")),
        name="flash_attention",
    )(*args)


def _score_tiles(bq, bk, count_fn, window):
    n_diag = bq // bk
    window_blocks = 0 if window is None else (window + bk - 1) // bk
    rel = jnp.arange(-(n_diag - 1), window_blocks + 1)
    d = rel[:, None, None] * bk + jnp.arange(bq)[None, :, None] - jnp.arange(bk)[None, None, :]
    count = jnp.where(d >= 0, count_fn(d), 0.0)
    tiles = jnp.where(count > 0, jnp.log2(jnp.maximum(count, 1.0)), NEG_INF).astype(F32)
    return tiles, (None if window is None else window_blocks)


def _causal_tiles(bq, bk):
    return _score_tiles(bq, bk, lambda d: jnp.ones(d.shape, F32), None)[0]


def _dilated_tiles(bq, bk):
    count_fn = lambda d: sum(((d <= w) & (d % dil == 0)).astype(F32) for w, dil in DILATED_PAIRS)
    return _score_tiles(bq, bk, count_fn, max(w for w, _ in DILATED_PAIRS))


def _stick_kernel(q_ref, k_ref, v_ref, t_ref, o_ref, *, bq, bk, scale, hps):
    qi = pl.program_id(1)
    d = HEAD_DIM
    n_diag = bq // bk
    qs = [q_ref[:, g * d:(g + 1) * d] for g in range(hps)]
    tri2 = jnp.concatenate([t_ref[...], t_ref[...]], axis=0)

    def head_block(g, j, r, acc, diag_t):
        start = pl.multiple_of(j * bk, bk)
        k = k_ref[pl.ds(start, bk), g * d:(g + 1) * d]
        v = v_ref[pl.ds(start, bk), g * d:(g + 1) * d]
        z = lax.dot_general(qs[g], k, (((1,), (1,)), ((), ())), preferred_element_type=F32) * scale
        sp = jnp.maximum(z, 0.0) + jnp.log(1.0 + jnp.exp(-jnp.abs(z)))
        log_1m = -sp
        if diag_t is not None:
            mask = (lax.broadcasted_iota(jnp.int32, (bq, bk), 1) + diag_t * bk
                    < lax.broadcasted_iota(jnp.int32, (bq, bk), 0))
            log_1m = jnp.where(mask, log_1m, 0.0)
        hi = log_1m.astype(BF16)
        lo = (log_1m - hi.astype(F32)).astype(BF16)
        after = jnp.dot(jnp.concatenate([hi, lo], axis=1), tri2, preferred_element_type=F32) + r
        a = jnp.exp((z - sp) + after)
        if diag_t is not None:
            a = jnp.where(mask, a, 0.0)
        acc = acc + jnp.dot(a.astype(BF16), v, preferred_element_type=F32)
        r = r + jnp.sum(log_1m, axis=-1, keepdims=True)
        return r, acc

    def block(j, state, diag_t):
        return tuple(head_block(g, j, *state[g], diag_t) for g in range(hps))

    state = tuple((jnp.zeros((bq, 1), F32), jnp.zeros((bq, d), F32)) for _ in range(hps))
    j_diag = qi * n_diag
    for t in reversed(range(n_diag)):
        state = block(j_diag + t, state, t)

    def cond(c):
        j, state = c
        r_max = functools.reduce(jnp.maximum, [jnp.max(r) for r, _ in state])
        return jnp.logical_and(j >= 0, r_max > -F32_EXP_UNDERFLOW)

    def body(c):
        j, state = c
        return j - 1, block(j, state, None)

    _, state = lax.while_loop(cond, body, (j_diag - 1, state))
    for g, (_, acc) in enumerate(state):
        o_ref[:, g * d:(g + 1) * d] = acc.astype(o_ref.dtype)


def stick_breaking_attention(qkv, q_cb, k_cb, v_cb, *, scale, bq=512, bk=256, hps=4):
    s_len = qkv.shape[0]
    tri = jnp.tril(jnp.ones((bk, bk), F32), -1).astype(BF16)
    w = hps * HEAD_DIM
    assert q_cb % hps == 0 and k_cb % hps == 0 and v_cb % hps == 0
    return pl.pallas_call(
        functools.partial(_stick_kernel, bq=bq, bk=bk, scale=scale, hps=hps),
        grid=(N_HEADS // hps, s_len // bq),
        in_specs=[pl.BlockSpec((bq, w), lambda h, i: (i, q_cb // hps + h)),
                  pl.BlockSpec((s_len, w), lambda h, i: (0, k_cb // hps + h)),
                  pl.BlockSpec((s_len, w), lambda h, i: (0, v_cb // hps + h)),
                  pl.BlockSpec((bk, bk), lambda h, i: (0, 0))],
        out_specs=pl.BlockSpec((bq, w), lambda h, i: (i, h)),
        out_shape=jax.ShapeDtypeStruct((s_len, GROUP_WIDTH), BF16),
        compiler_params=_cparams(("parallel", "arbitrary")),
        name="stick_breaking",
    )(qkv, qkv, qkv, tri)


def _prep_mla_weights(w_uq, w_uk, w_uv):
    pad = 2 * HEAD_DIM
    wq = w_uq.reshape(Q_LORA_RANK, N_HEADS, HEAD_DIM + MLA_ROPE_DIM)
    w_q = jnp.pad(wq, ((0, 0), (0, 0), (0, pad - wq.shape[2]))).reshape(Q_LORA_RANK, N_HEADS * pad)
    wk = w_uk.reshape(KV_LORA_RANK, N_HEADS, HEAD_DIM)
    w_k = jnp.pad(wk, ((0, 0), (0, 0), (0, pad - HEAD_DIM))).reshape(KV_LORA_RANK, N_HEADS * pad)
    return w_q.astype(BF16), jnp.concatenate([w_k, w_uv], axis=1).astype(BF16)


def kernel(x, g_attn, w_in, g_q, g_kv, w_uq, w_uk, w_uv, b_f, g_out, w_o, g_mlp, w_up, w_down, g_final):
    b, s_len, d_model = x.shape
    assert b == 1 and s_len % 2048 == 0
    depth = w_in.shape[0]
    xf = x.reshape(s_len, d_model)

    mla_tables = _rope_tables(s_len, MLA_ROPE_DIM, LANES, pass_through=False)
    dil_tables = _rope_tables(s_len, PARTIAL_ROPE_DIM, HEAD_DIM, pass_through=True)
    causal = _causal_tiles(FLASH_BQ, FLASH_BK)
    dil_tiles, dil_blocks = _dilated_tiles(DILATED_BQ, DILATED_BK)
    scale = HEAD_DIM ** -0.5
    mla_scale = (HEAD_DIM + MLA_ROPE_DIM) ** -0.5
    nh = N_HEADS
    n_a = Q_LORA_RANK + KV_LORA_RANK + MLA_ROPE_DIM
    n_a128 = -(-n_a // LANES) * LANES
    gate_col0 = w_in.shape[2] - LANES
    w_in_t = jnp.swapaxes(w_in, 1, 2)

    h = rmsnorm(xf, g_attn[0], BF16)
    h_ssq = None
    for l in range(depth):
        w_q, w_kv = _prep_mla_weights(w_uq[l], w_uk[l], w_uv[l])

        seg_a = matmul_f32w(h, w_in_t, l, 0, n_a128, F32, transposed=True, row_ssq=h_ssq, tn=n_a128 // 2)
        bqk_r = matmul_f32w(h, w_in_t, l, n_a, 2 * GROUP_WIDTH, BF16, transposed=True,
                            row_ssq=h_ssq, rope=(dil_tables, PARTIAL_ROPE_DIM // 2))
        rest = matmul_f32w(h, w_in_t, l, n_a + 2 * GROUP_WIDTH, 7 * GROUP_WIDTH, BF16, transposed=True,
                           row_ssq=h_ssq)
        gate = matmul_f32w(h, w_in_t, l, gate_col0, LANES, F32, transposed=True, row_ssq=h_ssq,
                           tn=LANES)

        q_cat = mla_proj(seg_a, 0, g_q[l], w_q, mla_tables)
        kv_cat = mla_proj(seg_a, Q_LORA_RANK, g_kv[l], w_kv, mla_tables,
                          kr_cols=N_HEADS * 2 * HEAD_DIM)
        o_a = flash_attention(q_cat, 0, kv_cat, 0, kv_cat, 2 * nh, dqk=2 * HEAD_DIM, scale=mla_scale, tiles=causal,
                              hps=2)

        o_b = flash_attention(bqk_r, 0, bqk_r, nh, rest, 0, dqk=HEAD_DIM, scale=scale, tiles=dil_tiles,
                              window_blocks=dil_blocks, hps=2)

        o_c = stick_breaking_attention(rest, nh, 2 * nh, 3 * nh, scale=scale)

        c_f = forget_cumsum(gate, b_f[l], out_scale=LOG2E)
        o_d = flash_attention(rest, 4 * nh, rest, 5 * nh, rest, 6 * nh, dqk=HEAD_DIM, scale=scale, tiles=causal,
                              decay=c_f, hps=2)

        mix = group_norm_concat(o_a, o_b, o_c, o_d, g_out[l])
        xf, xg, ssq = matmul_f32w(mix, w_o, l, 0, d_model, F32, res=xf, emit_norm_gain=g_mlp[l], tm=512)
        ssq = jnp.sum(ssq, axis=0, keepdims=True)
        u, w_down_bf = matmul_f32w(xg, w_up, l, 0, w_up.shape[2], BF16, act="relu2", row_ssq=ssq,
                                   side_cast=(w_down, l))
        if l + 1 < depth:
            xf, h, h_ssq = matmul(u, w_down_bf, F32, res=xf, emit_norm_gain=g_attn[l + 1])
            h_ssq = jnp.sum(h_ssq, axis=0, keepdims=True)
        else:
            xf = matmul(u, w_down_bf, F32, res=xf)

    return rmsnorm(xf, g_final, F32).reshape(b, s_len, d_model)
```

```python
import functools
import math

import jax
import jax.numpy as jnp
from jax import lax
from jax.experimental import pallas as pl
from jax.experimental.pallas import tpu as pltpu

F32 = jnp.float32
BF16 = jnp.bfloat16

HEAD_DIM = 128
N_HEADS = 8
GROUP_WIDTH = N_HEADS * HEAD_DIM
Q_LORA_RANK = 896
KV_LORA_RANK = 512
MLA_ROPE_DIM = 64
ROPE_THETA = 500000.0
PARTIAL_ROPE_DIM = HEAD_DIM // 4
DILATED_PAIRS = ((128, 1), (512, 4), (2048, 16))
EPS = 1e-6
NEG_INF = -1e30

LANES = 128
VMEM_LIMIT_BYTES = 62 * 1024 * 1024

F32_EXP_UNDERFLOW = 104.0
LOG2E = math.log2(math.e)

FLASH_BQ, FLASH_BK = 1024, 1024
DILATED_BQ, DILATED_BK = 512, 512


def _cparams(semantics):
    return pltpu.CompilerParams(dimension_semantics=semantics, vmem_limit_bytes=VMEM_LIMIT_BYTES)


def _rms(x, g):
    return x * lax.rsqrt(jnp.mean(x * x, axis=-1, keepdims=True) + EPS) * g


def _rmsnorm_kernel(x_ref, g_ref, o_ref):
    o_ref[...] = _rms(x_ref[...], g_ref[...]).astype(o_ref.dtype)


def rmsnorm(x, g, out_dtype, tm=512):
    m, d = x.shape
    return pl.pallas_call(
        _rmsnorm_kernel,
        grid=(m // tm,),
        in_specs=[pl.BlockSpec((tm, d), lambda i: (i, 0)), pl.BlockSpec((1, d), lambda i: (0, 0))],
        out_specs=pl.BlockSpec((tm, d), lambda i: (i, 0)),
        out_shape=jax.ShapeDtypeStruct((m, d), out_dtype),
        compiler_params=_cparams(("parallel",)),
        name="rmsnorm",
    )(x, g.reshape(1, d))


def _group_norm_kernel(a_ref, b_ref, c_ref, d_ref, g_ref, o_ref):
    for gi, r in enumerate((a_ref, b_ref, c_ref, d_ref)):
        sl = slice(gi * GROUP_WIDTH, (gi + 1) * GROUP_WIDTH)
        o_ref[:, sl] = _rms(r[...].astype(F32), g_ref[:, sl]).astype(o_ref.dtype)


def group_norm_concat(o_a, o_b, o_c, o_d, g, tm=512):
    m = o_a.shape[0]
    spec = pl.BlockSpec((tm, GROUP_WIDTH), lambda i: (i, 0))
    return pl.pallas_call(
        _group_norm_kernel,
        grid=(m // tm,),
        in_specs=[spec, spec, spec, spec, pl.BlockSpec((1, 4 * GROUP_WIDTH), lambda i: (0, 0))],
        out_specs=pl.BlockSpec((tm, 4 * GROUP_WIDTH), lambda i: (i, 0)),
        out_shape=jax.ShapeDtypeStruct((m, 4 * GROUP_WIDTH), BF16),
        compiler_params=_cparams(("parallel",)),
        name="group_norm",
    )(o_a, o_b, o_c, o_d, g.reshape(1, -1))


def _mm_kernel(*refs, nk, act, has_res, emit_norm):
    refs = list(refs)
    a_ref, w_ref = refs.pop(0), refs.pop(0)
    res_ref = refs.pop(0) if has_res else None
    gain_ref = refs.pop(0) if emit_norm else None
    o_ref = refs.pop(0)
    xg_ref, ssq_ref = (refs.pop(0), refs.pop(0)) if emit_norm else (None, None)

    def finish(acc):
        if act == "relu2":
            acc = jnp.square(jnp.maximum(acc, 0.0))
        if has_res:
            acc = res_ref[...] + acc
        o_ref[...] = acc.astype(o_ref.dtype)

    if nk == 1:
        finish(jnp.dot(a_ref[...], w_ref[...], preferred_element_type=F32))
    elif act is None and o_ref.dtype == F32:
        k = pl.program_id(2)

        @pl.when(k == 0)
        def _():
            o_ref[...] = res_ref[...] if has_res else jnp.zeros_like(o_ref)

        o_ref[...] += jnp.dot(a_ref[...], w_ref[...], preferred_element_type=F32)

        if emit_norm:
            @pl.when(k == nk - 1)
            def _():
                x = o_ref[...]
                xg_ref[...] = (x * gain_ref[...]).astype(BF16)
                ssq_ref[...] = jnp.sum(x * x, axis=1, keepdims=True)
    else:
        acc_ref = refs[0]
        k = pl.program_id(2)

        @pl.when(k == 0)
        def _():
            acc_ref[...] = jnp.zeros_like(acc_ref)

        acc_ref[...] += jnp.dot(a_ref[...], w_ref[...], preferred_element_type=F32)

        @pl.when(k == nk - 1)
        def _():
            finish(acc_ref[...])


def _pick(n, prefs):
    for p in prefs:
        if n % p == 0:
            return p
    return n


def matmul(a, w, out_dtype, act=None, res=None, layer=None, emit_norm_gain=None, tm=None, tn=None, tk=None,
           name="matmul"):
    m, kdim = a.shape
    n = w.shape[-1]
    tm = tm or _pick(m, (1024, 512, 256))
    tn = tn or _pick(n, (1024, 768, 512, 256, 128))
    has_res = res is not None
    tk = tk or _pick(kdim, (4096, 2048, 1024))
    nk = kdim // tk
    out_is_acc = nk > 1 and act is None and out_dtype == F32
    emit_norm = emit_norm_gain is not None
    assert out_is_acc or not emit_norm
    w_spec = (pl.BlockSpec((tk, tn), lambda i, j, k: (k, j)) if layer is None else
              pl.BlockSpec((None, tk, tn), lambda i, j, k: (layer, k, j)))
    in_specs = [pl.BlockSpec((tm, tk), lambda i, j, k: (i, k)), w_spec]
    args = [a, w]
    if has_res:
        in_specs.append(pl.BlockSpec((tm, tn), lambda i, j, k: (i, j)))
        args.append(res)
    out_specs = [pl.BlockSpec((tm, tn), lambda i, j, k: (i, j))]
    out_shape = [jax.ShapeDtypeStruct((m, n), out_dtype)]
    if emit_norm:
        in_specs.append(pl.BlockSpec((1, tn), lambda i, j, k: (0, j)))
        args.append(emit_norm_gain.reshape(1, n))
        out_specs += [pl.BlockSpec((tm, tn), lambda i, j, k: (i, j)),
                      pl.BlockSpec((None, tm, 1), lambda i, j, k: (j, i, 0))]
        out_shape += [jax.ShapeDtypeStruct((m, n), BF16), jax.ShapeDtypeStruct((n // tn, m, 1), F32)]
    scratch = [pltpu.VMEM((tm, tn), F32)] if (nk > 1 and not out_is_acc) else []
    out = pl.pallas_call(
        functools.partial(_mm_kernel, nk=nk, act=act, has_res=has_res, emit_norm=emit_norm),
        grid=(m // tm, n // tn, nk),
        in_specs=in_specs,
        out_specs=out_specs,
        out_shape=out_shape,
        scratch_shapes=scratch,
        compiler_params=_cparams(("parallel", "parallel", "arbitrary")),
        name=name,
    )(*args)
    return out[0] if len(out) == 1 else tuple(out)


def _mmw_kernel(*refs, layer, col0, tn, nj, transposed, act, has_res, rope_half, norm_dim, emit_norm, has_side,
                patch, row_chunk):
    refs = list(refs)
    a_ref, w_hbm = refs.pop(0), refs.pop(0)
    res_ref = refs.pop(0) if has_res else None
    tab_refs = [refs.pop(0) for _ in range(3)] if rope_half else None
    ssq_in_ref = refs.pop(0) if norm_dim else None
    gain_ref = refs.pop(0) if emit_norm else None
    side_in_ref = refs.pop(0) if has_side else None
    patch_ref = refs.pop(0) if patch else None
    o_ref = refs.pop(0)
    xg_ref, ssq_out_ref = (refs.pop(0), refs.pop(0)) if emit_norm else (None, None)
    side_out_ref = refs.pop(0) if has_side else None
    wf32_ref, wbf_ref, sem = refs
    j = pl.program_id(0)

    if has_side:
        side_out_ref[...] = side_in_ref[...].astype(BF16)

    def w_copy(jt):
        start = pl.multiple_of(col0 + jt * tn, 8 if transposed else LANES)
        src = w_hbm.at[layer, pl.ds(start, tn), :] if transposed else w_hbm.at[layer, :, pl.ds(start, tn)]
        return pltpu.make_async_copy(src, wf32_ref, sem.at[0])

    @pl.when(pl.program_id(1) == 0)
    def _():
        @pl.when(j == 0)
        def _():
            w_copy(0).start()

        w_copy(j).wait()

        def cast_rows(r, c):
            rows = pl.ds(pl.multiple_of(r * row_chunk, row_chunk), row_chunk)
            wbf_ref[rows, :] = wf32_ref[rows, :].astype(BF16)
            return c
        lax.fori_loop(0, wf32_ref.shape[0] // row_chunk, cast_rows, 0)

        if patch:
            patch_j, patch_row = patch

            @pl.when(j == patch_j)
            def _():
                wbf_ref[patch_row:patch_row + patch_ref.shape[0], :] = patch_ref[...].astype(BF16)

        @pl.when(j + 1 < nj)
        def _():
            w_copy(j + 1).start()

    dims = (((1,), (1,)), ((), ())) if transposed else (((1,), (0,)), ((), ()))
    acc = lax.dot_general(a_ref[...], wbf_ref[...], dims, preferred_element_type=F32)
    if norm_dim:
        acc = acc * lax.rsqrt(jnp.sum(ssq_in_ref[...], axis=0) * (1.0 / norm_dim) + EPS)
    if act == "relu2":
        acc = jnp.square(jnp.maximum(acc, 0.0))
    if rope_half:
        acc = _rope(acc, *(t[...] for t in tab_refs), rope_half)
    if has_res:
        acc = res_ref[...] + acc
    o_ref[...] = acc.astype(o_ref.dtype)
    if emit_norm:
        xg_ref[...] = (acc * gain_ref[...]).astype(BF16)
        ssq_out_ref[...] = jnp.sum(acc * acc, axis=1, keepdims=True)


def matmul_f32w(a, w_stack, layer, col0, n, out_dtype, act=None, res=None, rope=None, transposed=False,
                side_cast=None, row_ssq=None, emit_norm_gain=None, patch_cols=None, tm=1024, tn=1024,
                name="matmul_f32w"):
    m, kdim = a.shape
    has_res = res is not None
    assert col0 % (8 if transposed else LANES) == 0 and n % tn == 0 and m % tm == 0
    w_shape = (tn, kdim) if transposed else (kdim, tn)
    in_specs = [pl.BlockSpec((tm, kdim), lambda j, i: (i, 0)), pl.BlockSpec(memory_space=pl.ANY)]
    args = [a, w_stack]
    if has_res:
        in_specs.append(pl.BlockSpec((tm, tn), lambda j, i: (i, j)))
        args.append(res)
    rope_half = 0
    if rope is not None:
        tables, rope_half = rope
        in_specs += [pl.BlockSpec((tm, tables[0].shape[1]), lambda j, i: (i, 0))] * 3
        args += list(tables)
    if row_ssq is not None:
        in_specs.append(pl.BlockSpec((row_ssq.shape[0], tm, 1), lambda j, i: (0, i, 0)))
        args.append(row_ssq)
    out_specs = [pl.BlockSpec((tm, tn), lambda j, i: (i, j))]
    out_shape = [jax.ShapeDtypeStruct((m, n), out_dtype)]
    nj, ni = n // tn, m // tm
    patch = None
    patch_spec = []
    if patch_cols is not None:
        c, rows = patch_cols
        assert transposed and c % 16 == 0 and rows.shape[0] % 16 == 0 and (c % tn) + rows.shape[0] <= tn
        patch = (c // tn, c % tn)
        patch_spec = [(pl.BlockSpec(rows.shape, lambda j, i: (0, 0)), rows)]
    if emit_norm_gain is not None:
        in_specs.append(pl.BlockSpec((1, tn), lambda j, i: (0, j)))
        args.append(emit_norm_gain.reshape(1, n))
        out_specs += [pl.BlockSpec((tm, tn), lambda j, i: (i, j)), pl.BlockSpec((None, tm, 1), lambda j, i: (j, i, 0))]
        out_shape += [jax.ShapeDtypeStruct((m, n), BF16), jax.ShapeDtypeStruct((nj, m, 1), F32)]
    if side_cast is not None:
        side_stack, side_layer = side_cast
        _, side_r, side_c = side_stack.shape
        slab = side_r // (nj * ni)
        assert slab * nj * ni == side_r and slab % 16 == 0
        in_specs.append(pl.BlockSpec((None, slab, side_c), lambda j, i: (side_layer, j * ni + i, 0)))
        args.append(side_stack)
        out_specs.append(pl.BlockSpec((slab, side_c), lambda j, i: (j * ni + i, 0)))
        out_shape.append(jax.ShapeDtypeStruct((side_r, side_c), BF16))
    for spec, arr in patch_spec:
        in_specs.append(spec)
        args.append(arr)
    out = pl.pallas_call(
        functools.partial(_mmw_kernel, layer=layer, col0=col0, tn=tn, nj=nj, transposed=transposed, act=act,
                          has_res=has_res, rope_half=rope_half, norm_dim=kdim if row_ssq is not None else 0,
                          emit_norm=emit_norm_gain is not None, has_side=side_cast is not None, patch=patch,
                          row_chunk=min(256, w_shape[0])),
        grid=(nj, ni),
        in_specs=in_specs,
        out_specs=out_specs,
        out_shape=out_shape,
        scratch_shapes=[pltpu.VMEM(w_shape, F32), pltpu.VMEM(w_shape, BF16), pltpu.SemaphoreType.DMA((1,))],
        compiler_params=_cparams(("arbitrary", "arbitrary")),
        name=name,
    )(*args)
    return out[0] if len(out) == 1 else tuple(out)


def _rope_tables(s_len, r, period, pass_through):
    half = r // 2
    inv = ROPE_THETA ** (-jnp.arange(half, dtype=F32) * (2.0 / r))
    ang = jnp.arange(s_len, dtype=F32)[:, None] * inv[None, :]
    cos, sin = jnp.cos(ang), jnp.sin(ang)
    fill = jnp.ones if pass_through else jnp.zeros
    zeros = jnp.zeros((s_len, period - r), F32)
    zh = jnp.zeros((s_len, half), F32)
    c = jnp.concatenate([cos, cos, fill((s_len, period - r), F32)], axis=1)
    sa = jnp.concatenate([-sin, zh, zeros], axis=1)
    sb = jnp.concatenate([zh, sin, zeros], axis=1)
    return c, sa, sb


def _rope(x, c, sa, sb, half):
    w = x.shape[-1]
    reps = w // c.shape[-1]
    if reps > 1:
        c, sa, sb = (jnp.tile(t, (1, reps)) for t in (c, sa, sb))
    return x * c + pltpu.roll(x, w - half, 1) * sa + pltpu.roll(x, half, 1) * sb


def _mla_proj_kernel(seg_ref, g_ref, w_ref, c_ref, sa_ref, sb_ref, o_ref, *, col0, width, kr_col0, kr_cols):
    half = MLA_ROPE_DIM // 2
    group = 2 * HEAD_DIM
    a = _rms(seg_ref[:, col0:col0 + width], g_ref[...]).astype(BF16)
    acc = jnp.dot(a, w_ref[...], preferred_element_type=F32)
    c, sa, sb = c_ref[...], sa_ref[...], sb_ref[...]
    if kr_cols is None:
        for lo in range(0, acc.shape[1], group):
            o_ref[:, lo:lo + HEAD_DIM] = acc[:, lo:lo + HEAD_DIM].astype(o_ref.dtype)
            o_ref[:, lo + HEAD_DIM:lo + group] = _rope(acc[:, lo + HEAD_DIM:lo + group], c, sa, sb, half).astype(o_ref.dtype)
    else:
        kr = _rope(seg_ref[:, kr_col0:kr_col0 + LANES], c, sa, sb, half)
        for lo in range(0, kr_cols, group):
            o_ref[:, lo:lo + HEAD_DIM] = acc[:, lo:lo + HEAD_DIM].astype(o_ref.dtype)
            o_ref[:, lo + HEAD_DIM:lo + group] = (acc[:, lo + HEAD_DIM:lo + group] + kr).astype(o_ref.dtype)
        o_ref[:, kr_cols:] = acc[:, kr_cols:].astype(o_ref.dtype)


def mla_proj(seg_a, col0, g, w, tables, kr_cols=None, tm=1024):
    m, seg_w = seg_a.shape
    width, n = w.shape
    tspec = pl.BlockSpec((tm, tables[0].shape[1]), lambda i: (i, 0))
    return pl.pallas_call(
        functools.partial(_mla_proj_kernel, col0=col0, width=width, kr_col0=Q_LORA_RANK + KV_LORA_RANK,
                          kr_cols=kr_cols),
        grid=(m // tm,),
        in_specs=[pl.BlockSpec((tm, seg_w), lambda i: (i, 0)),
                  pl.BlockSpec((1, width), lambda i: (0, 0)),
                  pl.BlockSpec((width, n), lambda i: (0, 0)),
                  tspec, tspec, tspec],
        out_specs=pl.BlockSpec((tm, n), lambda i: (i, 0)),
        out_shape=jax.ShapeDtypeStruct((m, n), BF16),
        compiler_params=_cparams(("parallel",)),
        name="mla_proj",
    )(seg_a, g.reshape(1, -1), w, *tables)


def _split3(x):
    hi = x.astype(BF16)
    r1 = x - hi.astype(F32)
    mid = r1.astype(BF16)
    lo = (r1 - mid.astype(F32)).astype(BF16)
    return hi, mid, lo


def _forget_kernel(x_ref, b_ref, u_ref, o_ref, carry_ref, *, lane0, out_scale):
    @pl.when(pl.program_id(0) == 0)
    def _():
        carry_ref[...] = jnp.zeros_like(carry_ref)

    xt = jnp.transpose(x_ref[...])
    y = -(xt[lane0:lane0 + N_HEADS, :] + b_ref[...])
    logf = -(jnp.maximum(y, 0.0) + jnp.log(1.0 + jnp.exp(-jnp.abs(y))))
    u = u_ref[...]
    cs = sum(jnp.dot(p, u, preferred_element_type=F32) for p in _split3(logf))
    c = cs + carry_ref[:, 0:1]
    o_ref[...] = c * out_scale
    carry_ref[...] = jnp.broadcast_to(c[:, -1:], carry_ref.shape)


def forget_cumsum(gate, col0, b_f, tc=512, out_scale=1.0):
    m = gate.shape[0]
    blk, lane0 = divmod(col0, LANES)
    u = jnp.triu(jnp.ones((tc, tc), F32)).astype(BF16)
    return pl.pallas_call(
        functools.partial(_forget_kernel, lane0=lane0, out_scale=out_scale),
        grid=(m // tc,),
        in_specs=[pl.BlockSpec((tc, LANES), lambda i: (i, blk)),
                  pl.BlockSpec((N_HEADS, 1), lambda i: (0, 0)),
                  pl.BlockSpec((tc, tc), lambda i: (0, 0))],
        out_specs=pl.BlockSpec((N_HEADS, tc), lambda i: (0, i)),
        out_shape=jax.ShapeDtypeStruct((N_HEADS, m), F32),
        scratch_shapes=[pltpu.VMEM((N_HEADS, LANES), F32)],
        compiler_params=_cparams(("arbitrary",)),
        name="forget_cumsum",
    )(gate, b_f.reshape(N_HEADS, 1), u)


def _flash_kernel(*refs, bq, bk, dqk, scale2, window_blocks, has_decay, hps):
    q_ref, k_ref, v_ref, tile_ref = refs[:4]
    ck_ref = refs[4] if has_decay else None
    o_ref = refs[4 + has_decay]
    dv = HEAD_DIM
    n_diag = bq // bk
    qi = pl.program_id(1)
    qs = [q_ref[:, g * dqk:(g + 1) * dqk] for g in range(hps)]

    def head_step(g, j, carry, tile):
        m, l, acc = carry
        start = pl.multiple_of(j * bk, bk)
        k = k_ref[pl.ds(start, bk), g * dqk:(g + 1) * dqk]
        v = v_ref[pl.ds(start, bk), g * dv:(g + 1) * dv]
        s = lax.dot_general(qs[g], k, (((1,), (1,)), ((), ())), preferred_element_type=F32) * scale2
        if has_decay:
            s = s - ck_ref[g, pl.ds(j, 1), :]
        if tile is not None:
            s = s + tile
        m_new = jnp.maximum(m, jnp.max(s, axis=-1, keepdims=True))
        alpha = jnp.exp2(m - m_new)
        p = jnp.exp2(s - m_new)
        l = alpha * l + jnp.sum(p, axis=-1, keepdims=True)
        acc = alpha * acc + jnp.dot(p.astype(BF16), v, preferred_element_type=F32)
        return m_new, l, acc

    def step(j, carry, tile):
        return tuple(head_step(g, j, carry[g], tile) for g in range(hps))

    carry = tuple((jnp.full((bq, 1), NEG_INF, F32), jnp.zeros((bq, 1), F32), jnp.zeros((bq, dv), F32))
                  for _ in range(hps))
    j_diag = qi * n_diag
    if window_blocks is None:
        lo = 0
        body = lambda j, c: step(j, c, None)
    else:
        lo = jnp.maximum(j_diag - window_blocks, 0)
        body = lambda j, c: step(j, c, tile_ref[n_diag - 1 + j_diag - j])
    carry = lax.fori_loop(lo, j_diag, body, carry)
    for t in range(n_diag):
        carry = step(j_diag + t, carry, tile_ref[n_diag - 1 - t])
    for g, (m, l, acc) in enumerate(carry):
        o_ref[:, g * dv:(g + 1) * dv] = (acc / l).astype(o_ref.dtype)


def flash_attention(q_arr, q_cb, k_arr, k_cb, v_arr, v_cb, *, dqk, scale, tiles, window_blocks=None,
                    decay=None, hps=1):
    s_len = q_arr.shape[0]
    _, bq, bk = tiles.shape
    nq, nk = s_len // bq, s_len // bk
    assert q_cb % hps == 0 and k_cb % hps == 0 and v_cb % hps == 0
    in_specs = [pl.BlockSpec((bq, hps * dqk), lambda h, i: (i, q_cb // hps + h)),
                pl.BlockSpec((s_len, hps * dqk), lambda h, i: (0, k_cb // hps + h)),
                pl.BlockSpec((s_len, hps * HEAD_DIM), lambda h, i: (0, v_cb // hps + h)),
                pl.BlockSpec(tiles.shape, lambda h, i: (0, 0, 0))]
    args = [q_arr, k_arr, v_arr, tiles]
    if decay is not None:
        in_specs.append(pl.BlockSpec((hps, nk, bk), lambda h, i: (h, 0, 0)))
        args.append(decay.reshape(N_HEADS, nk, bk))
    return pl.pallas_call(
        functools.partial(_flash_kernel, bq=bq, bk=bk, dqk=dqk, scale2=scale * LOG2E,
                          window_blocks=window_blocks, has_decay=decay is not None, hps=hps),
        grid=(N_HEADS // hps, nq),
        in_specs=in_specs,
        out_specs=pl.BlockSpec((bq, hps * HEAD_DIM), lambda h, i: (i, h)),
        out_shape=jax.ShapeDtypeStruct((s_len, GROUP_WIDTH), BF16),
        compiler_params=_cparams(("parallel", "arbitrary")),
        name="flash_attention",
    )(*args)


def _score_tiles(bq, bk, count_fn, window):
    n_diag = bq // bk
    window_blocks = 0 if window is None else (window + bk - 1) // bk
    rel = jnp.arange(-(n_diag - 1), window_blocks + 1)
    d = rel[:, None, None] * bk + jnp.arange(bq)[None, :, None] - jnp.arange(bk)[None, None, :]
    count = jnp.where(d >= 0, count_fn(d), 0.0)
    tiles = jnp.where(count > 0, jnp.log2(jnp.maximum(count, 1.0)), NEG_INF).astype(F32)
    return tiles, (None if window is None else window_blocks)


def _causal_tiles(bq, bk):
    return _score_tiles(bq, bk, lambda d: jnp.ones(d.shape, F32), None)[0]


def _dilated_tiles(bq, bk):
    count_fn = lambda d: sum(((d <= w) & (d % dil == 0)).astype(F32) for w, dil in DILATED_PAIRS)
    return _score_tiles(bq, bk, count_fn, max(w for w, _ in DILATED_PAIRS))


def _stick_kernel(q_ref, k_ref, v_ref, t_ref, o_ref, *, bq, bk, scale, hps):
    qi = pl.program_id(1)
    d = HEAD_DIM
    n_diag = bq // bk
    qs = [q_ref[:, g * d:(g + 1) * d] for g in range(hps)]
    tri = t_ref[...]

    def head_block(g, j, r, acc, diag_t):
        start = pl.multiple_of(j * bk, bk)
        k = k_ref[pl.ds(start, bk), g * d:(g + 1) * d]
        v = v_ref[pl.ds(start, bk), g * d:(g + 1) * d]
        z = lax.dot_general(qs[g], k, (((1,), (1,)), ((), ())), preferred_element_type=F32) * scale
        sp = jnp.maximum(z, 0.0) + jnp.log(1.0 + jnp.exp(-jnp.abs(z)))
        log_1m = -sp
        if diag_t is not None:
            mask = (lax.broadcasted_iota(jnp.int32, (bq, bk), 1) + diag_t * bk
                    < lax.broadcasted_iota(jnp.int32, (bq, bk), 0))
            log_1m = jnp.where(mask, log_1m, 0.0)
        hi = log_1m.astype(BF16)
        lo = (log_1m - hi.astype(F32)).astype(BF16)
        after = (jnp.dot(hi, tri, preferred_element_type=F32) + jnp.dot(lo, tri, preferred_element_type=F32)) + r
        a = jnp.exp((z - sp) + after)
        if diag_t is not None:
            a = jnp.where(mask, a, 0.0)
        acc = acc + jnp.dot(a.astype(BF16), v, preferred_element_type=F32)
        r = r + jnp.sum(log_1m, axis=-1, keepdims=True)
        return r, acc

    def block(j, state, diag_t):
        return tuple(head_block(g, j, *state[g], diag_t) for g in range(hps))

    state = tuple((jnp.zeros((bq, 1), F32), jnp.zeros((bq, d), F32)) for _ in range(hps))
    j_diag = qi * n_diag
    for t in reversed(range(n_diag)):
        state = block(j_diag + t, state, t)

    def cond(c):
        j, state = c
        r_max = functools.reduce(jnp.maximum, [jnp.max(r) for r, _ in state])
        return jnp.logical_and(j >= 0, r_max > -F32_EXP_UNDERFLOW)

    def body(c):
        j, state = c
        return j - 1, block(j, state, None)

    _, state = lax.while_loop(cond, body, (j_diag - 1, state))
    for g, (_, acc) in enumerate(state):
        o_ref[:, g * d:(g + 1) * d] = acc.astype(o_ref.dtype)


def stick_breaking_attention(qkv, q_cb, k_cb, v_cb, *, scale, bq=512, bk=256, hps=4):
    s_len = qkv.shape[0]
    tri = jnp.tril(jnp.ones((bk, bk), F32), -1).astype(BF16)
    w = hps * HEAD_DIM
    assert q_cb % hps == 0 and k_cb % hps == 0 and v_cb % hps == 0
    return pl.pallas_call(
        functools.partial(_stick_kernel, bq=bq, bk=bk, scale=scale, hps=hps),
        grid=(N_HEADS // hps, s_len // bq),
        in_specs=[pl.BlockSpec((bq, w), lambda h, i: (i, q_cb // hps + h)),
                  pl.BlockSpec((s_len, w), lambda h, i: (0, k_cb // hps + h)),
                  pl.BlockSpec((s_len, w), lambda h, i: (0, v_cb // hps + h)),
                  pl.BlockSpec((bk, bk), lambda h, i: (0, 0))],
        out_specs=pl.BlockSpec((bq, w), lambda h, i: (i, h)),
        out_shape=jax.ShapeDtypeStruct((s_len, GROUP_WIDTH), BF16),
        compiler_params=_cparams(("parallel", "arbitrary")),
        name="stick_breaking",
    )(qkv, qkv, qkv, tri)


def _prep_mla_weights(w_uq, w_uk, w_uv):
    pad = 2 * HEAD_DIM
    wq = w_uq.reshape(Q_LORA_RANK, N_HEADS, HEAD_DIM + MLA_ROPE_DIM)
    w_q = jnp.pad(wq, ((0, 0), (0, 0), (0, pad - wq.shape[2]))).reshape(Q_LORA_RANK, N_HEADS * pad)
    wk = w_uk.reshape(KV_LORA_RANK, N_HEADS, HEAD_DIM)
    w_k = jnp.pad(wk, ((0, 0), (0, 0), (0, pad - HEAD_DIM))).reshape(KV_LORA_RANK, N_HEADS * pad)
    return w_q.astype(BF16), jnp.concatenate([w_k, w_uv], axis=1).astype(BF16)


def kernel(x, g_attn, w_in, g_q, g_kv, w_uq, w_uk, w_uv, b_f, g_out, w_o, g_mlp, w_up, w_down, g_final):
    b, s_len, d_model = x.shape
    assert b == 1 and s_len % 2048 == 0
    depth = w_in.shape[0]
    xf = x.reshape(s_len, d_model)

    mla_tables = _rope_tables(s_len, MLA_ROPE_DIM, LANES, pass_through=False)
    dil_tables = _rope_tables(s_len, PARTIAL_ROPE_DIM, HEAD_DIM, pass_through=True)
    causal = _causal_tiles(FLASH_BQ, FLASH_BK)
    dil_tiles, dil_blocks = _dilated_tiles(DILATED_BQ, DILATED_BK)
    scale = HEAD_DIM ** -0.5
    mla_scale = (HEAD_DIM + MLA_ROPE_DIM) ** -0.5
    nh = N_HEADS
    n_a = Q_LORA_RANK + KV_LORA_RANK + MLA_ROPE_DIM
    n_a128 = -(-n_a // LANES) * LANES
    gate0 = n_a + 9 * GROUP_WIDTH
    w_in_t = jnp.swapaxes(w_in, 1, 2)

    h = rmsnorm(xf, g_attn[0], BF16)
    h_ssq = None
    for l in range(depth):
        w_q, w_kv = _prep_mla_weights(w_uq[l], w_uk[l], w_uv[l])

        gate_rows = jnp.concatenate([w_in_t[l, gate0:gate0 + N_HEADS], w_in_t[l, n_a + N_HEADS:n_a + 16]], axis=0)
        seg_a = matmul_f32w(h, w_in_t, l, 0, n_a128, F32, transposed=True, row_ssq=h_ssq, tn=n_a128 // 2,
                            patch_cols=(n_a, gate_rows))
        bqk_r = matmul_f32w(h, w_in_t, l, n_a, 2 * GROUP_WIDTH, BF16, transposed=True,
                            row_ssq=h_ssq, rope=(dil_tables, PARTIAL_ROPE_DIM // 2))
        rest = matmul_f32w(h, w_in_t, l, n_a + 2 * GROUP_WIDTH, 7 * GROUP_WIDTH, BF16, transposed=True,
                           row_ssq=h_ssq)

        q_cat = mla_proj(seg_a, 0, g_q[l], w_q, mla_tables)
        kv_cat = mla_proj(seg_a, Q_LORA_RANK, g_kv[l], w_kv, mla_tables,
                          kr_cols=N_HEADS * 2 * HEAD_DIM)
        o_a = flash_attention(q_cat, 0, kv_cat, 0, kv_cat, 2 * nh, dqk=2 * HEAD_DIM, scale=mla_scale, tiles=causal,
                              hps=2)

        o_b = flash_attention(bqk_r, 0, bqk_r, nh, rest, 0, dqk=HEAD_DIM, scale=scale, tiles=dil_tiles,
                              window_blocks=dil_blocks, hps=2)

        o_c = stick_breaking_attention(rest, nh, 2 * nh, 3 * nh, scale=scale)

        c_f = forget_cumsum(seg_a, n_a, b_f[l], out_scale=LOG2E)
        o_d = flash_attention(rest, 4 * nh, rest, 5 * nh, rest, 6 * nh, dqk=HEAD_DIM, scale=scale, tiles=causal,
                              decay=c_f, hps=2)

        mix = group_norm_concat(o_a, o_b, o_c, o_d, g_out[l])
        xf, xg, ssq = matmul_f32w(mix, w_o, l, 0, d_model, F32, res=xf, emit_norm_gain=g_mlp[l], tm=512)
        ssq = jnp.sum(ssq, axis=0, keepdims=True)
        u, w_down_bf = matmul_f32w(xg, w_up, l, 0, w_up.shape[2], BF16, act="relu2", row_ssq=ssq,
                                   side_cast=(w_down, l))
        if l + 1 < depth:
            xf, h, h_ssq = matmul(u, w_down_bf, F32, res=xf, emit_norm_gain=g_attn[l + 1])
            h_ssq = jnp.sum(h_ssq, axis=0, keepdims=True)
        else:
            xf = matmul(u, w_down_bf, F32, res=xf)

    return rmsnorm(xf, g_final, F32).reshape(b, s_len, d_model)
```

```python
import functools
import math

import jax
import jax.numpy as jnp
from jax import lax
from jax.experimental import pallas as pl
from jax.experimental.pallas import tpu as pltpu

F32 = jnp.float32
BF16 = jnp.bfloat16

HEAD_DIM = 128
N_HEADS = 8
GROUP_WIDTH = N_HEADS * HEAD_DIM
Q_LORA_RANK = 896
KV_LORA_RANK = 512
MLA_ROPE_DIM = 64
ROPE_THETA = 500000.0
PARTIAL_ROPE_DIM = HEAD_DIM // 4
DILATED_PAIRS = ((128, 1), (512, 4), (2048, 16))
EPS = 1e-6
NEG_INF = -1e30

LANES = 128
VMEM_LIMIT_BYTES = 62 * 1024 * 1024

F32_EXP_UNDERFLOW = 104.0
LOG2E = math.log2(math.e)

FLASH_BQ, FLASH_BK = 1024, 1024
DILATED_BQ, DILATED_BK = 512, 512
CAST_ROW_CHUNK = 256


def _cparams(semantics):
    return pltpu.CompilerParams(dimension_semantics=semantics, vmem_limit_bytes=VMEM_LIMIT_BYTES)


def _rms(x, g):
    return x * lax.rsqrt(jnp.mean(x * x, axis=-1, keepdims=True) + EPS) * g


def _rmsnorm_kernel(x_ref, g_ref, o_ref):
    o_ref[...] = _rms(x_ref[...], g_ref[...]).astype(o_ref.dtype)


def rmsnorm(x, g, out_dtype, tm=512):
    m, d = x.shape
    return pl.pallas_call(
        _rmsnorm_kernel,
        grid=(m // tm,),
        in_specs=[pl.BlockSpec((tm, d), lambda i: (i, 0)), pl.BlockSpec((1, d), lambda i: (0, 0))],
        out_specs=pl.BlockSpec((tm, d), lambda i: (i, 0)),
        out_shape=jax.ShapeDtypeStruct((m, d), out_dtype),
        compiler_params=_cparams(("parallel",)),
        name="rmsnorm",
    )(x, g.reshape(1, d))


def _group_norm_kernel(a_ref, b_ref, c_ref, d_ref, g_ref, o_ref):
    for gi, r in enumerate((a_ref, b_ref, c_ref, d_ref)):
        sl = slice(gi * GROUP_WIDTH, (gi + 1) * GROUP_WIDTH)
        o_ref[:, sl] = _rms(r[...].astype(F32), g_ref[:, sl]).astype(o_ref.dtype)


def group_norm_concat(o_a, o_b, o_c, o_d, g, tm=512):
    m = o_a.shape[0]
    spec = pl.BlockSpec((tm, GROUP_WIDTH), lambda i: (i, 0))
    return pl.pallas_call(
        _group_norm_kernel,
        grid=(m // tm,),
        in_specs=[spec, spec, spec, spec, pl.BlockSpec((1, 4 * GROUP_WIDTH), lambda i: (0, 0))],
        out_specs=pl.BlockSpec((tm, 4 * GROUP_WIDTH), lambda i: (i, 0)),
        out_shape=jax.ShapeDtypeStruct((m, 4 * GROUP_WIDTH), BF16),
        compiler_params=_cparams(("parallel",)),
        name="group_norm",
    )(o_a, o_b, o_c, o_d, g.reshape(1, -1))


def _mm_kernel(*refs, nk, act, has_res, emit_norm):
    refs = list(refs)
    a_ref, w_ref = refs.pop(0), refs.pop(0)
    res_ref = refs.pop(0) if has_res else None
    gain_ref = refs.pop(0) if emit_norm else None
    o_ref = refs.pop(0)
    xg_ref, ssq_ref = (refs.pop(0), refs.pop(0)) if emit_norm else (None, None)

    def finish(acc):
        if act == "relu2":
            acc = jnp.square(jnp.maximum(acc, 0.0))
        if has_res:
            acc = res_ref[...] + acc
        o_ref[...] = acc.astype(o_ref.dtype)

    if nk == 1:
        finish(jnp.dot(a_ref[...], w_ref[...], preferred_element_type=F32))
    elif act is None and o_ref.dtype == F32:
        k = pl.program_id(2)

        @pl.when(k == 0)
        def _():
            o_ref[...] = res_ref[...] if has_res else jnp.zeros_like(o_ref)

        o_ref[...] += jnp.dot(a_ref[...], w_ref[...], preferred_element_type=F32)

        if emit_norm:
            @pl.when(k == nk - 1)
            def _():
                x = o_ref[...]
                xg_ref[...] = (x * gain_ref[...]).astype(BF16)
                ssq_ref[...] = jnp.sum(x * x, axis=1, keepdims=True)
    else:
        acc_ref = refs[0]
        k = pl.program_id(2)

        @pl.when(k == 0)
        def _():
            acc_ref[...] = jnp.zeros_like(acc_ref)

        acc_ref[...] += jnp.dot(a_ref[...], w_ref[...], preferred_element_type=F32)

        @pl.when(k == nk - 1)
        def _():
            finish(acc_ref[...])


def _pick(n, prefs):
    for p in prefs:
        if n % p == 0:
            return p
    return n


def matmul(a, w, out_dtype, act=None, res=None, layer=None, emit_norm_gain=None, tm=None, tn=None, tk=None,
           name="matmul"):
    m, kdim = a.shape
    n = w.shape[-1]
    tm = tm or _pick(m, (1024, 512, 256))
    tn = tn or _pick(n, (1024, 768, 512, 256, 128))
    has_res = res is not None
    tk = tk or _pick(kdim, (4096, 2048, 1024))
    nk = kdim // tk
    out_is_acc = nk > 1 and act is None and out_dtype == F32
    emit_norm = emit_norm_gain is not None
    assert out_is_acc or not emit_norm
    w_spec = (pl.BlockSpec((tk, tn), lambda i, j, k: (k, j)) if layer is None else
              pl.BlockSpec((None, tk, tn), lambda i, j, k: (layer, k, j)))
    in_specs = [pl.BlockSpec((tm, tk), lambda i, j, k: (i, k)), w_spec]
    args = [a, w]
    if has_res:
        in_specs.append(pl.BlockSpec((tm, tn), lambda i, j, k: (i, j)))
        args.append(res)
    out_specs = [pl.BlockSpec((tm, tn), lambda i, j, k: (i, j))]
    out_shape = [jax.ShapeDtypeStruct((m, n), out_dtype)]
    if emit_norm:
        in_specs.append(pl.BlockSpec((1, tn), lambda i, j, k: (0, j)))
        args.append(emit_norm_gain.reshape(1, n))
        out_specs += [pl.BlockSpec((tm, tn), lambda i, j, k: (i, j)),
                      pl.BlockSpec((None, tm, 1), lambda i, j, k: (j, i, 0))]
        out_shape += [jax.ShapeDtypeStruct((m, n), BF16), jax.ShapeDtypeStruct((n // tn, m, 1), F32)]
    scratch = [pltpu.VMEM((tm, tn), F32)] if (nk > 1 and not out_is_acc) else []
    out = pl.pallas_call(
        functools.partial(_mm_kernel, nk=nk, act=act, has_res=has_res, emit_norm=emit_norm),
        grid=(m // tm, n // tn, nk),
        in_specs=in_specs,
        out_specs=out_specs,
        out_shape=out_shape,
        scratch_shapes=scratch,
        compiler_params=_cparams(("parallel", "parallel", "arbitrary")),
        name=name,
    )(*args)
    return out[0] if len(out) == 1 else tuple(out)


def _mmw_kernel(*refs, layer, col0, tn, nj, transposed, act, has_res, rope_half, norm_dim, emit_norm, has_side,
                patch, row_chunk):
    refs = list(refs)
    a_ref, w_hbm = refs.pop(0), refs.pop(0)
    res_ref = refs.pop(0) if has_res else None
    tab_refs = [refs.pop(0) for _ in range(3)] if rope_half else None
    ssq_in_ref = refs.pop(0) if norm_dim else None
    gain_ref = refs.pop(0) if emit_norm else None
    side_in_ref = refs.pop(0) if has_side else None
    patch_ref = refs.pop(0) if patch else None
    o_ref = refs.pop(0)
    xg_ref, ssq_out_ref = (refs.pop(0), refs.pop(0)) if emit_norm else (None, None)
    side_out_ref = refs.pop(0) if has_side else None
    wf32_ref, wbf_ref, sem = refs
    j = pl.program_id(0)

    if has_side:
        side_out_ref[...] = side_in_ref[...].astype(BF16)

    def w_copy(jt):
        start = pl.multiple_of(col0 + jt * tn, 8 if transposed else LANES)
        src = w_hbm.at[layer, pl.ds(start, tn), :] if transposed else w_hbm.at[layer, :, pl.ds(start, tn)]
        return pltpu.make_async_copy(src, wf32_ref, sem.at[0])

    @pl.when(pl.program_id(1) == 0)
    def _():
        @pl.when(j == 0)
        def _():
            w_copy(0).start()

        w_copy(j).wait()

        def cast_rows(r, c):
            rows = pl.ds(pl.multiple_of(r * row_chunk, row_chunk), row_chunk)
            wbf_ref[rows, :] = wf32_ref[rows, :].astype(BF16)
            return c
        lax.fori_loop(0, wf32_ref.shape[0] // row_chunk, cast_rows, 0)

        if patch:
            patch_j, patch_row = patch

            @pl.when(j == patch_j)
            def _():
                wbf_ref[patch_row:patch_row + patch_ref.shape[0], :] = patch_ref[...].astype(BF16)

        @pl.when(j + 1 < nj)
        def _():
            w_copy(j + 1).start()

    dims = (((1,), (1,)), ((), ())) if transposed else (((1,), (0,)), ((), ()))
    acc = lax.dot_general(a_ref[...], wbf_ref[...], dims, preferred_element_type=F32)
    if norm_dim:
        acc = acc * lax.rsqrt(jnp.sum(ssq_in_ref[...], axis=0) * (1.0 / norm_dim) + EPS)
    if act == "relu2":
        acc = jnp.square(jnp.maximum(acc, 0.0))
    if rope_half:
        acc = _rope(acc, *(t[...] for t in tab_refs), rope_half)
    if has_res:
        acc = res_ref[...] + acc
    o_ref[...] = acc.astype(o_ref.dtype)
    if emit_norm:
        xg_ref[...] = (acc * gain_ref[...]).astype(BF16)
        ssq_out_ref[...] = jnp.sum(acc * acc, axis=1, keepdims=True)


def matmul_f32w(a, w_stack, layer, col0, n, out_dtype, act=None, res=None, rope=None, transposed=False,
                side_cast=None, row_ssq=None, emit_norm_gain=None, patch_cols=None, tm=1024, tn=1024,
                name="matmul_f32w"):
    m, kdim = a.shape
    has_res = res is not None
    assert col0 % (8 if transposed else LANES) == 0 and n % tn == 0 and m % tm == 0
    w_shape = (tn, kdim) if transposed else (kdim, tn)
    in_specs = [pl.BlockSpec((tm, kdim), lambda j, i: (i, 0)), pl.BlockSpec(memory_space=pl.ANY)]
    args = [a, w_stack]
    if has_res:
        in_specs.append(pl.BlockSpec((tm, tn), lambda j, i: (i, j)))
        args.append(res)
    rope_half = 0
    if rope is not None:
        tables, rope_half = rope
        in_specs += [pl.BlockSpec((tm, tables[0].shape[1]), lambda j, i: (i, 0))] * 3
        args += list(tables)
    if row_ssq is not None:
        in_specs.append(pl.BlockSpec((row_ssq.shape[0], tm, 1), lambda j, i: (0, i, 0)))
        args.append(row_ssq)
    out_specs = [pl.BlockSpec((tm, tn), lambda j, i: (i, j))]
    out_shape = [jax.ShapeDtypeStruct((m, n), out_dtype)]
    nj, ni = n // tn, m // tm
    patch = None
    patch_spec = []
    if patch_cols is not None:
        c, rows = patch_cols
        assert transposed and c % 16 == 0 and rows.shape[0] % 16 == 0 and (c % tn) + rows.shape[0] <= tn
        patch = (c // tn, c % tn)
        patch_spec = [(pl.BlockSpec(rows.shape, lambda j, i: (0, 0)), rows)]
    if emit_norm_gain is not None:
        in_specs.append(pl.BlockSpec((1, tn), lambda j, i: (0, j)))
        args.append(emit_norm_gain.reshape(1, n))
        out_specs += [pl.BlockSpec((tm, tn), lambda j, i: (i, j)), pl.BlockSpec((None, tm, 1), lambda j, i: (j, i, 0))]
        out_shape += [jax.ShapeDtypeStruct((m, n), BF16), jax.ShapeDtypeStruct((nj, m, 1), F32)]
    if side_cast is not None:
        side_stack, side_layer = side_cast
        _, side_r, side_c = side_stack.shape
        slab = side_r // (nj * ni)
        assert slab * nj * ni == side_r and slab % 16 == 0
        in_specs.append(pl.BlockSpec((None, slab, side_c), lambda j, i: (side_layer, j * ni + i, 0)))
        args.append(side_stack)
        out_specs.append(pl.BlockSpec((slab, side_c), lambda j, i: (j * ni + i, 0)))
        out_shape.append(jax.ShapeDtypeStruct((side_r, side_c), BF16))
    for spec, arr in patch_spec:
        in_specs.append(spec)
        args.append(arr)
    out = pl.pallas_call(
        functools.partial(_mmw_kernel, layer=layer, col0=col0, tn=tn, nj=nj, transposed=transposed, act=act,
                          has_res=has_res, rope_half=rope_half, norm_dim=kdim if row_ssq is not None else 0,
                          emit_norm=emit_norm_gain is not None, has_side=side_cast is not None, patch=patch,
                          row_chunk=min(CAST_ROW_CHUNK, w_shape[0])),
        grid=(nj, ni),
        in_specs=in_specs,
        out_specs=out_specs,
        out_shape=out_shape,
        scratch_shapes=[pltpu.VMEM(w_shape, F32), pltpu.VMEM(w_shape, BF16), pltpu.SemaphoreType.DMA((1,))],
        compiler_params=_cparams(("arbitrary", "arbitrary")),
        name=name,
    )(*args)
    return out[0] if len(out) == 1 else tuple(out)


def _rope_tables(s_len, r, period, pass_through):
    half = r // 2
    inv = ROPE_THETA ** (-jnp.arange(half, dtype=F32) * (2.0 / r))
    ang = jnp.arange(s_len, dtype=F32)[:, None] * inv[None, :]
    cos, sin = jnp.cos(ang), jnp.sin(ang)
    fill = jnp.ones if pass_through else jnp.zeros
    zeros = jnp.zeros((s_len, period - r), F32)
    zh = jnp.zeros((s_len, half), F32)
    c = jnp.concatenate([cos, cos, fill((s_len, period - r), F32)], axis=1)
    sa = jnp.concatenate([-sin, zh, zeros], axis=1)
    sb = jnp.concatenate([zh, sin, zeros], axis=1)
    return c, sa, sb


def _rope(x, c, sa, sb, half):
    w = x.shape[-1]
    reps = w // c.shape[-1]
    if reps > 1:
        c, sa, sb = (jnp.tile(t, (1, reps)) for t in (c, sa, sb))
    return x * c + pltpu.roll(x, w - half, 1) * sa + pltpu.roll(x, half, 1) * sb


def _mla_proj_kernel(seg_ref, g_ref, w_ref, c_ref, sa_ref, sb_ref, o_ref, *, col0, width, kr_col0, kr_cols):
    half = MLA_ROPE_DIM // 2
    group = 2 * HEAD_DIM
    a = _rms(seg_ref[:, col0:col0 + width], g_ref[...]).astype(BF16)
    acc = jnp.dot(a, w_ref[...], preferred_element_type=F32)
    c, sa, sb = c_ref[...], sa_ref[...], sb_ref[...]
    if kr_cols is None:
        for lo in range(0, acc.shape[1], group):
            o_ref[:, lo:lo + HEAD_DIM] = acc[:, lo:lo + HEAD_DIM].astype(o_ref.dtype)
            o_ref[:, lo + HEAD_DIM:lo + group] = _rope(acc[:, lo + HEAD_DIM:lo + group], c, sa, sb, half).astype(o_ref.dtype)
    else:
        kr = _rope(seg_ref[:, kr_col0:kr_col0 + LANES], c, sa, sb, half)
        for lo in range(0, kr_cols, group):
            o_ref[:, lo:lo + HEAD_DIM] = acc[:, lo:lo + HEAD_DIM].astype(o_ref.dtype)
            o_ref[:, lo + HEAD_DIM:lo + group] = (acc[:, lo + HEAD_DIM:lo + group] + kr).astype(o_ref.dtype)
        o_ref[:, kr_cols:] = acc[:, kr_cols:].astype(o_ref.dtype)


def mla_proj(seg_a, col0, g, w, tables, kr_cols=None, tm=1024):
    m, seg_w = seg_a.shape
    width, n = w.shape
    tspec = pl.BlockSpec((tm, tables[0].shape[1]), lambda i: (i, 0))
    return pl.pallas_call(
        functools.partial(_mla_proj_kernel, col0=col0, width=width, kr_col0=Q_LORA_RANK + KV_LORA_RANK,
                          kr_cols=kr_cols),
        grid=(m // tm,),
        in_specs=[pl.BlockSpec((tm, seg_w), lambda i: (i, 0)),
                  pl.BlockSpec((1, width), lambda i: (0, 0)),
                  pl.BlockSpec((width, n), lambda i: (0, 0)),
                  tspec, tspec, tspec],
        out_specs=pl.BlockSpec((tm, n), lambda i: (i, 0)),
        out_shape=jax.ShapeDtypeStruct((m, n), BF16),
        compiler_params=_cparams(("parallel",)),
        name="mla_proj",
    )(seg_a, g.reshape(1, -1), w, *tables)


def _split3(x):
    hi = x.astype(BF16)
    r1 = x - hi.astype(F32)
    mid = r1.astype(BF16)
    lo = (r1 - mid.astype(F32)).astype(BF16)
    return hi, mid, lo


def _forget_kernel(x_ref, b_ref, u_ref, o_ref, carry_ref, *, lane0, out_scale):
    @pl.when(pl.program_id(0) == 0)
    def _():
        carry_ref[...] = jnp.zeros_like(carry_ref)

    xt = jnp.transpose(x_ref[...])
    y = -(xt[lane0:lane0 + N_HEADS, :] + b_ref[...])
    logf = -(jnp.maximum(y, 0.0) + jnp.log(1.0 + jnp.exp(-jnp.abs(y))))
    u = u_ref[...]
    cs = sum(jnp.dot(p, u, preferred_element_type=F32) for p in _split3(logf))
    c = cs + carry_ref[:, 0:1]
    o_ref[...] = c * out_scale
    carry_ref[...] = jnp.broadcast_to(c[:, -1:], carry_ref.shape)


def forget_cumsum(gate, col0, b_f, tc=512, out_scale=1.0):
    m = gate.shape[0]
    blk, lane0 = divmod(col0, LANES)
    u = jnp.triu(jnp.ones((tc, tc), F32)).astype(BF16)
    return pl.pallas_call(
        functools.partial(_forget_kernel, lane0=lane0, out_scale=out_scale),
        grid=(m // tc,),
        in_specs=[pl.BlockSpec((tc, LANES), lambda i: (i, blk)),
                  pl.BlockSpec((N_HEADS, 1), lambda i: (0, 0)),
                  pl.BlockSpec((tc, tc), lambda i: (0, 0))],
        out_specs=pl.BlockSpec((N_HEADS, tc), lambda i: (0, i)),
        out_shape=jax.ShapeDtypeStruct((N_HEADS, m), F32),
        scratch_shapes=[pltpu.VMEM((N_HEADS, LANES), F32)],
        compiler_params=_cparams(("arbitrary",)),
        name="forget_cumsum",
    )(gate, b_f.reshape(N_HEADS, 1), u)


def _flash_kernel(*refs, bq, bk, dqk, scale2, window_blocks, has_decay, hps):
    q_ref, k_ref, v_ref, tile_ref = refs[:4]
    ck_ref = refs[4] if has_decay else None
    o_ref = refs[4 + has_decay]
    dv = HEAD_DIM
    n_diag = bq // bk
    qi = pl.program_id(1)
    qs = [q_ref[:, g * dqk:(g + 1) * dqk] for g in range(hps)]

    def head_step(g, j, carry, tile):
        m, l, acc = carry
        start = pl.multiple_of(j * bk, bk)
        k = k_ref[pl.ds(start, bk), g * dqk:(g + 1) * dqk]
        v = v_ref[pl.ds(start, bk), g * dv:(g + 1) * dv]
        s = lax.dot_general(qs[g], k, (((1,), (1,)), ((), ())), preferred_element_type=F32) * scale2
        if has_decay:
            s = s - ck_ref[g, pl.ds(j, 1), :]
        if tile is not None:
            s = s + tile
        m_new = jnp.maximum(m, jnp.max(s, axis=-1, keepdims=True))
        alpha = jnp.exp2(m - m_new)
        p = jnp.exp2(s - m_new)
        l = alpha * l + jnp.sum(p, axis=-1, keepdims=True)
        acc = alpha * acc + jnp.dot(p.astype(BF16), v, preferred_element_type=F32)
        return m_new, l, acc

    def step(j, carry, tile):
        return tuple(head_step(g, j, carry[g], tile) for g in range(hps))

    carry = tuple((jnp.full((bq, 1), NEG_INF, F32), jnp.zeros((bq, 1), F32), jnp.zeros((bq, dv), F32))
                  for _ in range(hps))
    j_diag = qi * n_diag
    if window_blocks is None:
        lo = 0
        body = lambda j, c: step(j, c, None)
    else:
        lo = jnp.maximum(j_diag - window_blocks, 0)
        body = lambda j, c: step(j, c, tile_ref[n_diag - 1 + j_diag - j])
    carry = lax.fori_loop(lo, j_diag, body, carry)
    for t in range(n_diag):
        carry = step(j_diag + t, carry, tile_ref[n_diag - 1 - t])
    for g, (m, l, acc) in enumerate(carry):
        o_ref[:, g * dv:(g + 1) * dv] = (acc / l).astype(o_ref.dtype)


def flash_attention(q_arr, q_cb, k_arr, k_cb, v_arr, v_cb, *, dqk, scale, tiles, window_blocks=None,
                    decay=None, hps=1):
    s_len = q_arr.shape[0]
    _, bq, bk = tiles.shape
    nq, nk = s_len // bq, s_len // bk
    assert q_cb % hps == 0 and k_cb % hps == 0 and v_cb % hps == 0
    in_specs = [pl.BlockSpec((bq, hps * dqk), lambda h, i: (i, q_cb // hps + h)),
                pl.BlockSpec((s_len, hps * dqk), lambda h, i: (0, k_cb // hps + h)),
                pl.BlockSpec((s_len, hps * HEAD_DIM), lambda h, i: (0, v_cb // hps + h)),
                pl.BlockSpec(tiles.shape, lambda h, i: (0, 0, 0))]
    args = [q_arr, k_arr, v_arr, tiles]
    if decay is not None:
        in_specs.append(pl.BlockSpec((hps, nk, bk), lambda h, i: (h, 0, 0)))
        args.append(decay.reshape(N_HEADS, nk, bk))
    return pl.pallas_call(
        functools.partial(_flash_kernel, bq=bq, bk=bk, dqk=dqk, scale2=scale * LOG2E,
                          window_blocks=window_blocks, has_decay=decay is not None, hps=hps),
        grid=(N_HEADS // hps, nq),
        in_specs=in_specs,
        out_specs=pl.BlockSpec((bq, hps * HEAD_DIM), lambda h, i: (i, h)),
        out_shape=jax.ShapeDtypeStruct((s_len, GROUP_WIDTH), BF16),
        compiler_params=_cparams(("parallel", "arbitrary")),
        name="flash_attention",
    )(*args)


def _score_tiles(bq, bk, count_fn, window):
    n_diag = bq // bk
    window_blocks = 0 if window is None else (window + bk - 1) // bk
    rel = jnp.arange(-(n_diag - 1), window_blocks + 1)
    d = rel[:, None, None] * bk + jnp.arange(bq)[None, :, None] - jnp.arange(bk)[None, None, :]
    count = jnp.where(d >= 0, count_fn(d), 0.0)
    tiles = jnp.where(count > 0, jnp.log2(jnp.maximum(count, 1.0)), NEG_INF).astype(F32)
    return tiles, (None if window is None else window_blocks)


def _causal_tiles(bq, bk):
    return _score_tiles(bq, bk, lambda d: jnp.ones(d.shape, F32), None)[0]


def _dilated_tiles(bq, bk):
    count_fn = lambda d: sum(((d <= w) & (d % dil == 0)).astype(F32) for w, dil in DILATED_PAIRS)
    return _score_tiles(bq, bk, count_fn, max(w for w, _ in DILATED_PAIRS))


def _stick_kernel(q_ref, k_ref, v_ref, t_ref, o_ref, *, bq, bk, scale, hps):
    qi = pl.program_id(1)
    d = HEAD_DIM
    n_diag = bq // bk
    qs = [q_ref[:, g * d:(g + 1) * d] for g in range(hps)]
    tri2 = jnp.concatenate([t_ref[...], t_ref[...]], axis=0)

    def head_block(g, j, r, acc, diag_t):
        start = pl.multiple_of(j * bk, bk)
        k = k_ref[pl.ds(start, bk), g * d:(g + 1) * d]
        v = v_ref[pl.ds(start, bk), g * d:(g + 1) * d]
        z = lax.dot_general(qs[g], k, (((1,), (1,)), ((), ())), preferred_element_type=F32) * scale
        sp = jnp.maximum(z, 0.0) + jnp.log(1.0 + jnp.exp(-jnp.abs(z)))
        log_1m = -sp
        if diag_t is not None:
            mask = (lax.broadcasted_iota(jnp.int32, (bq, bk), 1) + diag_t * bk
                    < lax.broadcasted_iota(jnp.int32, (bq, bk), 0))
            log_1m = jnp.where(mask, log_1m, 0.0)
        hi = log_1m.astype(BF16)
        lo = (log_1m - hi.astype(F32)).astype(BF16)
        after = jnp.dot(jnp.concatenate([hi, lo], axis=1), tri2, preferred_element_type=F32) + r
        a = jnp.exp((z - sp) + after)
        if diag_t is not None:
            a = jnp.where(mask, a, 0.0)
        acc = acc + jnp.dot(a.astype(BF16), v, preferred_element_type=F32)
        r = r + jnp.sum(log_1m, axis=-1, keepdims=True)
        return r, acc

    def block(j, state, diag_t):
        return tuple(head_block(g, j, *state[g], diag_t) for g in range(hps))

    state = tuple((jnp.zeros((bq, 1), F32), jnp.zeros((bq, d), F32)) for _ in range(hps))
    j_diag = qi * n_diag
    for t in reversed(range(n_diag)):
        state = block(j_diag + t, state, t)

    def cond(c):
        j, state = c
        r_max = functools.reduce(jnp.maximum, [jnp.max(r) for r, _ in state])
        return jnp.logical_and(j >= 0, r_max > -F32_EXP_UNDERFLOW)

    def body(c):
        j, state = c
        return j - 1, block(j, state, None)

    _, state = lax.while_loop(cond, body, (j_diag - 1, state))
    for g, (_, acc) in enumerate(state):
        o_ref[:, g * d:(g + 1) * d] = acc.astype(o_ref.dtype)


def stick_breaking_attention(qkv, q_cb, k_cb, v_cb, *, scale, bq=512, bk=256, hps=4):
    s_len = qkv.shape[0]
    tri = jnp.tril(jnp.ones((bk, bk), F32), -1).astype(BF16)
    w = hps * HEAD_DIM
    assert q_cb % hps == 0 and k_cb % hps == 0 and v_cb % hps == 0
    return pl.pallas_call(
        functools.partial(_stick_kernel, bq=bq, bk=bk, scale=scale, hps=hps),
        grid=(N_HEADS // hps, s_len // bq),
        in_specs=[pl.BlockSpec((bq, w), lambda h, i: (i, q_cb // hps + h)),
                  pl.BlockSpec((s_len, w), lambda h, i: (0, k_cb // hps + h)),
                  pl.BlockSpec((s_len, w), lambda h, i: (0, v_cb // hps + h)),
                  pl.BlockSpec((bk, bk), lambda h, i: (0, 0))],
        out_specs=pl.BlockSpec((bq, w), lambda h, i: (i, h)),
        out_shape=jax.ShapeDtypeStruct((s_len, GROUP_WIDTH), BF16),
        compiler_params=_cparams(("parallel", "arbitrary")),
        name="stick_breaking",
    )(qkv, qkv, qkv, tri)


def _prep_mla_weights(w_uq, w_uk, w_uv):
    pad = 2 * HEAD_DIM
    wq = w_uq.reshape(Q_LORA_RANK, N_HEADS, HEAD_DIM + MLA_ROPE_DIM)
    w_q = jnp.pad(wq, ((0, 0), (0, 0), (0, pad - wq.shape[2]))).reshape(Q_LORA_RANK, N_HEADS * pad)
    wk = w_uk.reshape(KV_LORA_RANK, N_HEADS, HEAD_DIM)
    w_k = jnp.pad(wk, ((0, 0), (0, 0), (0, pad - HEAD_DIM))).reshape(KV_LORA_RANK, N_HEADS * pad)
    return w_q.astype(BF16), jnp.concatenate([w_k, w_uv], axis=1).astype(BF16)


def kernel(x, g_attn, w_in, g_q, g_kv, w_uq, w_uk, w_uv, b_f, g_out, w_o, g_mlp, w_up, w_down, g_final):
    b, s_len, d_model = x.shape
    assert b == 1 and s_len % 2048 == 0
    depth = w_in.shape[0]
    xf = x.reshape(s_len, d_model)

    mla_tables = _rope_tables(s_len, MLA_ROPE_DIM, LANES, pass_through=False)
    dil_tables = _rope_tables(s_len, PARTIAL_ROPE_DIM, HEAD_DIM, pass_through=True)
    causal = _causal_tiles(FLASH_BQ, FLASH_BK)
    dil_tiles, dil_blocks = _dilated_tiles(DILATED_BQ, DILATED_BK)
    scale = HEAD_DIM ** -0.5
    mla_scale = (HEAD_DIM + MLA_ROPE_DIM) ** -0.5
    nh = N_HEADS
    n_a = Q_LORA_RANK + KV_LORA_RANK + MLA_ROPE_DIM
    n_a128 = -(-n_a // LANES) * LANES
    gate0 = n_a + 9 * GROUP_WIDTH
    w_in_t = jnp.swapaxes(w_in, 1, 2)

    h = rmsnorm(xf, g_attn[0], BF16)
    h_ssq = None
    for l in range(depth):
        w_q, w_kv = _prep_mla_weights(w_uq[l], w_uk[l], w_uv[l])

        gate_rows = jnp.concatenate([w_in_t[l, gate0:gate0 + N_HEADS], w_in_t[l, n_a + N_HEADS:n_a + 16]], axis=0)
        seg_a = matmul_f32w(h, w_in_t, l, 0, n_a128, F32, transposed=True, row_ssq=h_ssq, tn=n_a128 // 2,
                            patch_cols=(n_a, gate_rows))
        bqk_r = matmul_f32w(h, w_in_t, l, n_a, 2 * GROUP_WIDTH, BF16, transposed=True,
                            row_ssq=h_ssq, rope=(dil_tables, PARTIAL_ROPE_DIM // 2))
        rest = matmul_f32w(h, w_in_t, l, n_a + 2 * GROUP_WIDTH, 7 * GROUP_WIDTH, BF16, transposed=True,
                           row_ssq=h_ssq)

        q_cat = mla_proj(seg_a, 0, g_q[l], w_q, mla_tables)
        kv_cat = mla_proj(seg_a, Q_LORA_RANK, g_kv[l], w_kv, mla_tables,
                          kr_cols=N_HEADS * 2 * HEAD_DIM)
        o_a = flash_attention(q_cat, 0, kv_cat, 0, kv_cat, 2 * nh, dqk=2 * HEAD_DIM, scale=mla_scale, tiles=causal,
                              hps=2)

        o_b = flash_attention(bqk_r, 0, bqk_r, nh, rest, 0, dqk=HEAD_DIM, scale=scale, tiles=dil_tiles,
                              window_blocks=dil_blocks, hps=2)

        o_c = stick_breaking_attention(rest, nh, 2 * nh, 3 * nh, scale=scale)

        c_f = forget_cumsum(seg_a, n_a, b_f[l], out_scale=LOG2E)
        o_d = flash_attention(rest, 4 * nh, rest, 5 * nh, rest, 6 * nh, dqk=HEAD_DIM, scale=scale, tiles=causal,
                              decay=c_f, hps=2)

        mix = group_norm_concat(o_a, o_b, o_c, o_d, g_out[l])
        xf, xg, ssq = matmul_f32w(mix, w_o, l, 0, d_model, F32, res=xf, emit_norm_gain=g_mlp[l], tm=512)
        ssq = jnp.sum(ssq, axis=0, keepdims=True)
        u, w_down_bf = matmul_f32w(xg, w_up, l, 0, w_up.shape[2], BF16, act="relu2", row_ssq=ssq,
                                   side_cast=(w_down, l))
        if l + 1 < depth:
            xf, h, h_ssq = matmul(u, w_down_bf, F32, res=xf, emit_norm_gain=g_attn[l + 1])
            h_ssq = jnp.sum(h_ssq, axis=0, keepdims=True)
        else:
            xf = matmul(u, w_down_bf, F32, res=xf)

    return rmsnorm(xf, g_final, F32).reshape(b, s_len, d_model)
```
